```python
import math
import jax
import jax.numpy as jnp
from jax import lax
import numpy as np

D_MODEL = 1024
BATCH = 16
SEQ = 2048
DEPTH = 4

HEAD_DIM = 64
ATTN_WIDTH = D_MODEL // 2
N_Q_HEADS = ATTN_WIDTH // HEAD_DIM
N_KV_HEADS = 2
GQA_GROUP = N_Q_HEADS // N_KV_HEADS
KV_WIDTH = N_KV_HEADS * HEAD_DIM
WINDOW = 128
BLOCK = 128
N_BUCKETS = 32
MAX_DISTANCE = 128
SSM_GROUP_CH = 16
SSM_WIDTH = D_MODEL // 2
SSM_GROUPS = SSM_WIDTH // SSM_GROUP_CH
SSM_STATE = 64
DT_MIN = 1e-3
DT_MAX = 1e-1
N_BRANCHES = 2
IN_COLS = ATTN_WIDTH + 2 * KV_WIDTH + SSM_WIDTH + N_BRANCHES * D_MODEL
D_FF = 2816
N_EXPERTS = 8
TOP_K = 2
D_FF_EXPERT = 3584
N_DENSE = (DEPTH + 1) // 2
N_MOE = DEPTH // 2
PLE_DIM = 256
DEEPNORM_ALPHA = (2 * DEPTH) ** 0.25
DEEPNORM_BETA = (8 * DEPTH) ** -0.25
LN_EPS = 1e-5
NEG_INF = -1e30

kernel_name = "hybrid_swa_s5_moe_deepnorm"


def layer_norm(x, g, b):
    xf = x.astype(jnp.float32)
    mu = jnp.mean(xf, axis=-1, keepdims=True)
    var = jnp.mean(jnp.square(xf - mu), axis=-1, keepdims=True)
    y = (xf - mu) * lax.rsqrt(var + LN_EPS) * g.astype(jnp.float32) + b.astype(jnp.float32)
    return y.astype(x.dtype)


def t5_causal_bucket(dist):
    dist = jnp.maximum(dist, 0)
    max_exact = N_BUCKETS // 2
    d_f = jnp.maximum(dist, 1).astype(jnp.float32)
    large = max_exact + (jnp.log(d_f / max_exact) / math.log(MAX_DISTANCE / max_exact)
                         * (N_BUCKETS - max_exact)).astype(jnp.int32)
    large = jnp.minimum(large, N_BUCKETS - 1)
    return jnp.where(dist < max_exact, dist, large)


def sliding_window_attention(q, k, v, sinks, rel_bias):
    bsz, seq, _ = q.shape
    nb = seq // BLOCK
    q = q.reshape(bsz, nb, BLOCK, N_KV_HEADS, GQA_GROUP, HEAD_DIM) * (HEAD_DIM ** -0.5)
    k = k.reshape(bsz, seq, N_KV_HEADS, HEAD_DIM)
    v = v.reshape(bsz, seq, N_KV_HEADS, HEAD_DIM)
    pad = ((0, 0), (BLOCK, 0), (0, 0), (0, 0))
    kp = jnp.pad(k, pad)
    vp = jnp.pad(v, pad)
    blk = (bsz, nb, BLOCK, N_KV_HEADS, HEAD_DIM)
    kb = jnp.concatenate([kp[:, :seq].reshape(blk), k.reshape(blk)], axis=2)
    vb = jnp.concatenate([vp[:, :seq].reshape(blk), v.reshape(blk)], axis=2)

    logits = jnp.einsum('bnqhgd,bnshd->bnhgqs', q, kb).astype(jnp.float32)

    qi = jnp.arange(BLOCK, dtype=jnp.int32)[:, None]
    kj = jnp.arange(2 * BLOCK, dtype=jnp.int32)[None, :]
    dist = qi + BLOCK - kj
    bias = rel_bias.astype(jnp.float32)[t5_causal_bucket(dist)]
    bias = jnp.transpose(bias, (2, 0, 1)).reshape(N_KV_HEADS, GQA_GROUP, BLOCK, 2 * BLOCK)
    key_pos = jnp.arange(nb, dtype=jnp.int32)[:, None, None] * BLOCK - BLOCK + kj[None]
    mask = (dist >= 0)[None] & (dist < WINDOW)[None] & (key_pos >= 0)
    logits = jnp.where(mask[None, :, None, None], logits + bias, NEG_INF)

    s = sinks.astype(jnp.float32).reshape(N_KV_HEADS, GQA_GROUP)[None, None, :, :, None]
    m = jnp.maximum(jnp.max(logits, axis=-1), s)
    e = jnp.exp(logits - m[..., None])
    denom = jnp.sum(e, axis=-1) + jnp.exp(s - m)
    w = (e / denom[..., None]).astype(v.dtype)
    o = jnp.einsum('bnhgqs,bnshd->bnqhgd', w, vb)
    return o.reshape(bsz, seq, ATTN_WIDTH)


def s5_mixer(u, lam_re, lam_im, log_dt, b_re, b_im, c_re, c_im, d, w_glu, b_glu):
    bsz, seq, _ = u.shape
    f32 = jnp.float32
    uf = u.astype(f32).reshape(bsz, seq, SSM_GROUPS, SSM_GROUP_CH)
    lam = lax.complex(lam_re.astype(f32), lam_im.astype(f32))
    dt = jnp.exp(log_dt.astype(f32))[:, None]
    lam_bar = jnp.exp(lam * dt)
    b = lax.complex(b_re.astype(f32), b_im.astype(f32))
    b_bar = ((lam_bar - 1.0) / lam)[..., None] * b
    bu = jnp.einsum('bsgh,gph->bsgp', uf.astype(jnp.complex64), b_bar)
    a = jnp.broadcast_to(lam_bar, (1, seq, SSM_GROUPS, SSM_STATE))

    def combine(left, right):
        a_l, x_l = left
        a_r, x_r = right
        return a_r * a_l, a_r * x_l + x_r

    _, states = lax.associative_scan(combine, (a, bu), axis=1)
    c = lax.complex(c_re.astype(f32), c_im.astype(f32))
    y = jnp.real(jnp.einsum('bsgp,ghp->bsgh', states, c))
    y = y + d.astype(f32).reshape(SSM_GROUPS, SSM_GROUP_CH) * uf
    y = jax.nn.gelu(y.reshape(bsz, seq, SSM_WIDTH)).astype(u.dtype)
    return y * jax.nn.sigmoid(y @ w_glu + b_glu)


def swiglu(h, w_gate_up, w_down):
    gate, up = jnp.split(h @ w_gate_up, 2, axis=-1)
    return (jax.nn.silu(gate) * up) @ w_down


def moe_swiglu(h, router, w_gate_up, w_down):
    bsz, seq, dm = h.shape
    hf = h.reshape(bsz * seq, dm)
    logits = (hf @ router).astype(jnp.float32)
    top_v, top_i = lax.top_k(logits, TOP_K)
    gates = jax.nn.softmax(top_v, axis=-1)
    combine = jnp.sum(jax.nn.one_hot(top_i, N_EXPERTS, dtype=jnp.float32) * gates[..., None], axis=1)
    combine = combine.astype(h.dtype)
    out = jnp.zeros_like(hf)
    for e in range(N_EXPERTS):
        out = out + combine[:, e:e + 1] * swiglu(hf, w_gate_up[e], w_down[e])
    return out.reshape(bsz, seq, dm)


def setup_inputs(seed: int = 0) -> dict:
    key = jax.random.key(seed)
    ks = jax.random.split(key, 32)
    nrm = lambda k, shape, scale: jax.random.normal(k, shape, jnp.float32) * scale
    n_idx = jnp.arange(SSM_STATE, dtype=jnp.float32)
    return {
        "x": nrm(ks[0], (BATCH, SEQ, D_MODEL), 1.0),
        "p": nrm(ks[1], (DEPTH, BATCH, SEQ, PLE_DIM), 1.0),
        "rel_bias": nrm(ks[2], (N_BUCKETS, N_Q_HEADS), 0.5),
        "w_in": nrm(ks[3], (DEPTH, D_MODEL, IN_COLS), D_MODEL ** -0.5),
        "attn_sinks": nrm(ks[4], (DEPTH, N_Q_HEADS), 0.5),
        "ssm_lambda_re": -0.5 + nrm(ks[5], (DEPTH, SSM_GROUPS, SSM_STATE), 0.01),
        "ssm_lambda_im": math.pi * n_idx + nrm(ks[6], (DEPTH, SSM_GROUPS, SSM_STATE), 0.01),
        "ssm_log_dt": jax.random.uniform(ks[7], (DEPTH, SSM_GROUPS), jnp.float32,
                                         math.log(DT_MIN), math.log(DT_MAX)),
        "ssm_b_re": nrm(ks[8], (DEPTH, SSM_GROUPS, SSM_STATE, SSM_GROUP_CH), (2 * SSM_GROUP_CH) ** -0.5),
        "ssm_b_im": nrm(ks[9], (DEPTH, SSM_GROUPS, SSM_STATE, SSM_GROUP_CH), (2 * SSM_GROUP_CH) ** -0.5),
        "ssm_c_re": nrm(ks[10], (DEPTH, SSM_GROUPS, SSM_GROUP_CH, SSM_STATE), SSM_STATE ** -0.5),
        "ssm_c_im": nrm(ks[11], (DEPTH, SSM_GROUPS, SSM_GROUP_CH, SSM_STATE), SSM_STATE ** -0.5),
        "ssm_d": nrm(ks[12], (DEPTH, SSM_WIDTH), 1.0),
        "w_glu": nrm(ks[13], (DEPTH, SSM_WIDTH, SSM_WIDTH), SSM_WIDTH ** -0.5),
        "b_glu": nrm(ks[14], (DEPTH, SSM_WIDTH), 0.02),
        "w_branch_attn": nrm(ks[15], (DEPTH, ATTN_WIDTH, D_MODEL), ATTN_WIDTH ** -0.5),
        "w_branch_ssm": nrm(ks[16], (DEPTH, SSM_WIDTH, D_MODEL), SSM_WIDTH ** -0.5),
        "w_out": nrm(ks[17], (DEPTH, D_MODEL, D_MODEL), DEEPNORM_BETA * D_MODEL ** -0.5),
        "ln1_g": 1.0 + nrm(ks[18], (DEPTH, D_MODEL), 0.02),
        "ln1_b": nrm(ks[19], (DEPTH, D_MODEL), 0.02),
        "ffn_w_gate_up": nrm(ks[20], (N_DENSE, D_MODEL, 2 * D_FF), D_MODEL ** -0.5),
        "ffn_w_down": nrm(ks[21], (N_DENSE, D_FF, D_MODEL), DEEPNORM_BETA * D_FF ** -0.5),
        "moe_router": nrm(ks[22], (N_MOE, D_MODEL, N_EXPERTS), D_MODEL ** -0.5),
        "moe_w_gate_up": nrm(ks[23], (N_MOE, N_EXPERTS, D_MODEL, 2 * D_FF_EXPERT), D_MODEL ** -0.5),
        "moe_w_down": nrm(ks[24], (N_MOE, N_EXPERTS, D_FF_EXPERT, D_MODEL), DEEPNORM_BETA * D_FF_EXPERT ** -0.5),
        "ple_w_proj": nrm(ks[25], (DEPTH, PLE_DIM, D_MODEL), DEEPNORM_BETA * PLE_DIM ** -0.5),
        "ple_w_gate": nrm(ks[26], (DEPTH, D_MODEL, D_MODEL), D_MODEL ** -0.5),
        "ln2_g": 1.0 + nrm(ks[27], (DEPTH, D_MODEL), 0.02),
        "ln2_b": nrm(ks[28], (DEPTH, D_MODEL), 0.02),
    }


def reference(x, p, rel_bias, w_in, attn_sinks, ssm_lambda_re, ssm_lambda_im, ssm_log_dt,
              ssm_b_re, ssm_b_im, ssm_c_re, ssm_c_im, ssm_d, w_glu, b_glu,
              w_branch_attn, w_branch_ssm, w_out, ln1_g, ln1_b,
              ffn_w_gate_up, ffn_w_down, moe_router, moe_w_gate_up, moe_w_down,
              ple_w_proj, ple_w_gate, ln2_g, ln2_b):
    splits = np.cumsum([ATTN_WIDTH, KV_WIDTH, KV_WIDTH, SSM_WIDTH, D_MODEL])[:].tolist()
    for i in range(DEPTH):
        z = x @ w_in[i]
        q, k, v, u, g_attn, g_ssm = jnp.split(z, splits, axis=-1)
        a_out = sliding_window_attention(q, k, v, attn_sinks[i], rel_bias)
        s_out = s5_mixer(u, ssm_lambda_re[i], ssm_lambda_im[i], ssm_log_dt[i],
                         ssm_b_re[i], ssm_b_im[i], ssm_c_re[i], ssm_c_im[i], ssm_d[i],
                         w_glu[i], b_glu[i])
        merged = (jax.nn.sigmoid(g_attn) * (a_out @ w_branch_attn[i])
                  + jax.nn.sigmoid(g_ssm) * (s_out @ w_branch_ssm[i]))
        x = layer_norm(DEEPNORM_ALPHA * x + merged @ w_out[i], ln1_g[i], ln1_b[i])
        if i % 2 == 0:
            ffn = swiglu(x, ffn_w_gate_up[i // 2], ffn_w_down[i // 2])
        else:
            ffn = moe_swiglu(x, moe_router[i // 2], moe_w_gate_up[i // 2], moe_w_down[i // 2])
        ple = jax.nn.sigmoid(x @ ple_w_gate[i]) * (p[i] @ ple_w_proj[i])
        x = layer_norm(DEEPNORM_ALPHA * x + ffn + ple, ln2_g[i], ln2_b[i])
    return x
```

```python
import functools
import math

import jax
import jax.numpy as jnp
import numpy as np
from jax import lax
from jax.experimental import pallas as pl
from jax.experimental.pallas import tpu as pltpu

F32 = jnp.float32
BF16 = jnp.bfloat16

D_MODEL = 1024
BATCH = 16
SEQ = 2048
DEPTH = 4
HEAD_DIM = 64
N_Q_HEADS = 8
N_KV_HEADS = 2
ATTN_WIDTH = N_Q_HEADS * HEAD_DIM
KV_WIDTH = N_KV_HEADS * HEAD_DIM
ATTN_BLOCK = 128
N_BUCKETS = 32
MAX_DISTANCE = 128
SSM_WIDTH = 512
SSM_GROUP_CH = 16
SSM_GROUPS = SSM_WIDTH // SSM_GROUP_CH
SSM_STATE = 64
D_FF = 2816
N_EXPERTS = 8
D_FF_EXPERT = 3584
PLE_DIM = 256
DEEPNORM_ALPHA = (2 * DEPTH) ** 0.25
LN_EPS = 1e-5
NEG_INF = -1e30

LANES = 128
VMEM_LIMIT_BYTES = 56 * 1024 * 1024

TM_PROJ = 1024
TM_MERGE = 512
TM_FFN = 512
TM_MOE = 512
FW_MOE = 512
SSM_STEPS = 32
SSM_LANE_CHUNK = 512
FFN_CHUNKS = ((0, 768), (768, 768), (1536, 768), (2304, 512))


def _cparams(sem):
    return pltpu.CompilerParams(dimension_semantics=sem, vmem_limit_bytes=VMEM_LIMIT_BYTES)


def _layer_norm(h, g, b):
    mu = jnp.mean(h, axis=-1, keepdims=True)
    c = h - mu
    var = jnp.mean(c * c, axis=-1, keepdims=True)
    return c * lax.rsqrt(var + LN_EPS) * g + b


def _sigmoid(v):
    return 1.0 / (1.0 + jnp.exp(-v))


def _gelu_tanh(v):
    return 0.5 * v * (1.0 + jnp.tanh(math.sqrt(2.0 / math.pi) * (v + 0.044715 * (v * v * v))))


def _inproj_body(x_ref, w_ref, q_ref, kv_ref, u_ref):
    xb = x_ref[...].astype(BF16)
    z = jnp.dot(xb, w_ref[...], preferred_element_type=F32)
    aw = q_ref.shape[1]
    kw = kv_ref.shape[1]
    q_ref[...] = (z[:, :aw] * (HEAD_DIM ** -0.5)).astype(BF16)
    kv_ref[...] = z[:, aw:aw + kw].astype(BF16)
    u_ref[...] = z[:, aw + kw:].astype(BF16)


def _inproj(x, w, *, tm):
    t, d = x.shape
    n = w.shape[1]
    aw, kw = ATTN_WIDTH, 2 * (N_KV_HEADS * 2 * LANES)
    uw = n - aw - kw
    return pl.pallas_call(
        _inproj_body,
        grid=(t // tm,),
        in_specs=[pl.BlockSpec((tm, d), lambda i: (i, 0)),
                  pl.BlockSpec((d, n), lambda i: (0, 0))],
        out_specs=[pl.BlockSpec((tm, aw), lambda i: (i, 0)),
                   pl.BlockSpec((tm, kw), lambda i: (i, 0)),
                   pl.BlockSpec((tm, uw), lambda i: (i, 0))],
        out_shape=[jax.ShapeDtypeStruct((t, aw), BF16),
                   jax.ShapeDtypeStruct((t, kw), BF16),
                   jax.ShapeDtypeStruct((t, uw), BF16)],
        compiler_params=_cparams(("parallel",)),
        name="inproj",
    )(x, w)


def _attn_body(sink_ref, q_ref, kvc_ref, kvp_ref, bias_ref, o_ref):
    n = pl.program_id(1)
    blk = q_ref.shape[0]
    col = lax.broadcasted_iota(jnp.int32, (1, 2 * blk), 1)
    first = jnp.where((n == 0) & (col < blk), NEG_INF, 0.0).astype(F32)
    v0 = N_KV_HEADS * 2 * LANES
    group = N_Q_HEADS // N_KV_HEADS
    for tile in range(N_Q_HEADS // 2):
        qt = q_ref[:, tile * LANES:(tile + 1) * LANES]
        acc = None
        for par in range(2):
            h = 2 * tile + par
            kt = 2 * (h // group) + par
            ks = slice(kt * LANES, (kt + 1) * LANES)
            vs = slice(v0 + kt * LANES, v0 + (kt + 1) * LANES)
            k = jnp.concatenate([kvp_ref[:, ks], kvc_ref[:, ks]], axis=0)
            v = jnp.concatenate([kvp_ref[:, vs], kvc_ref[:, vs]], axis=0)
            s = lax.dot_general(qt, k, (((1,), (1,)), ((), ())), preferred_element_type=F32)
            s = s + bias_ref[h] + first
            sink = sink_ref[h]
            m = jnp.maximum(jnp.max(s, axis=-1, keepdims=True), sink)
            e = jnp.exp(s - m)
            denom = jnp.sum(e, axis=-1, keepdims=True) + jnp.exp(sink - m)
            o = jnp.dot(e.astype(BF16), v, preferred_element_type=F32) / denom
            acc = o if acc is None else acc + o
        o_ref[:, tile * LANES:(tile + 1) * LANES] = acc.astype(BF16)


def _attention(q, kv, sinks, bias, *, seq, batch):
    t = q.shape[0]
    blk = ATTN_BLOCK
    nb = seq // blk
    aw, kw = q.shape[1], kv.shape[1]
    q2 = q.reshape(seq, batch * aw)
    kv2 = kv.reshape(seq, batch * kw)
    out = pl.pallas_call(
        _attn_body,
        grid=(batch, nb),
        in_specs=[pl.BlockSpec(memory_space=pltpu.SMEM),
                  pl.BlockSpec((blk, aw), lambda b, n: (n, b)),
                  pl.BlockSpec((blk, kw), lambda b, n: (n, b)),
                  pl.BlockSpec((blk, kw), lambda b, n: (jnp.maximum(n - 1, 0), b)),
                  pl.BlockSpec((N_Q_HEADS, blk, 2 * blk), lambda b, n: (0, 0, 0))],
        out_specs=pl.BlockSpec((blk, aw), lambda b, n: (n, b)),
        out_shape=jax.ShapeDtypeStruct((seq, batch * aw), BF16),
        compiler_params=_cparams(("parallel", "parallel")),
        name="swa_attention",
    )(sinks, q2, kv2, kv2, bias)
    return out.reshape(t, aw)


def _per_head_kv_columns(w):
    d = w.shape[0]
    wh = w.reshape(d, N_KV_HEADS, 1, 1, HEAD_DIM)
    half = jnp.arange(2)[None, None, None, :, None]
    par = jnp.arange(2)[None, None, :, None, None]
    tiles = jnp.where(half == par, wh, 0.0)
    return tiles.reshape(d, N_KV_HEADS * 2 * LANES)


def _attn_bias_table(rel_bias):
    blk = ATTN_BLOCK
    qi = np.arange(blk, dtype=np.int32)[:, None]
    kj = np.arange(2 * blk, dtype=np.int32)[None, :]
    dist = qi + blk - kj
    valid = (dist >= 0) & (dist < blk)
    d = np.maximum(dist, 0)
    max_exact = N_BUCKETS // 2
    d_f = np.maximum(d, 1).astype(np.float32)
    large = max_exact + (np.log(d_f / np.float32(max_exact)) / np.float32(math.log(MAX_DISTANCE / max_exact))
                         * np.float32(N_BUCKETS - max_exact)).astype(np.int32)
    large = np.minimum(large, N_BUCKETS - 1)
    bucket = np.where(d < max_exact, d, large)
    tbl = jnp.transpose(rel_bias.astype(F32)[bucket], (2, 0, 1))
    return jnp.where(valid[None], tbl, NEG_INF)


def _ssm_param_body(lre_ref, lim_ref, ldt_ref, bre_ref, bim_ref, lbr_ref, lbi_ref, bbr_ref, bbi_ref):
    lre, lim = lre_ref[...], lim_ref[...]
    dt = jnp.exp(ldt_ref[...])
    mag = jnp.exp(lre * dt)
    lbr = mag * jnp.cos(lim * dt)
    lbi = mag * jnp.sin(lim * dt)
    lbr_ref[...] = lbr
    lbi_ref[...] = lbi
    nr, ni = lbr - 1.0, lbi
    inv = 1.0 / (lre * lre + lim * lim)
    cr = (nr * lre + ni * lim) * inv
    ci = (ni * lre - nr * lim) * inv
    br, bi = bre_ref[...], bim_ref[...]
    bbr_ref[...] = cr * br - ci * bi
    bbi_ref[...] = cr * bi + ci * br


def _ssm_params(lam_re, lam_im, log_dt, b_re, b_im):
    g, p = lam_re.shape
    h = b_re.shape[2]
    ldt = jnp.broadcast_to(log_dt[:, None, None], (g, 1, p))
    brt = jnp.transpose(b_re, (0, 2, 1))
    bit = jnp.transpose(b_im, (0, 2, 1))
    sd = jax.ShapeDtypeStruct
    return pl.pallas_call(
        _ssm_param_body,
        out_shape=[sd((g, 1, p), F32), sd((g, 1, p), F32), sd((g, h, p), F32), sd((g, h, p), F32)],
        name="ssm_params",
    )(lam_re.reshape(g, 1, p), lam_im.reshape(g, 1, p), ldt, brt, bit)


def _block_diag_halves(w):
    g, a, b = w.shape
    hg = g // 2
    wh = w.reshape(2, hg, a, b)
    eye = jnp.eye(hg, dtype=bool)[None, :, None, :, None]
    full = jnp.where(eye, wh[:, :, :, None, :], 0.0)
    return full.reshape(2, hg * a, hg * b)


def _ssm_body(u_ref, bre_ref, bim_ref, cre_ref, cim_ref, lr_ref, li_ref, d_ref, wglu_ref, bglu_ref,
              o_ref, xr, xi, sr, si, *, steps, batch, lane_chunk):
    @pl.when(pl.program_id(0) == 0)
    def _():
        sr[...] = jnp.zeros_like(sr)
        si[...] = jnp.zeros_like(si)

    u = u_ref[...]
    hw = u.shape[1] // 2
    hs = xr.shape[1] // 2
    for hf in range(2):
        uh = u[:, hf * hw:(hf + 1) * hw]
        xr[:, hf * hs:(hf + 1) * hs] = jnp.dot(uh, bre_ref[hf], preferred_element_type=F32)
        xi[:, hf * hs:(hf + 1) * hs] = jnp.dot(uh, bim_ref[hf], preferred_element_type=F32)

    for c in range(xr.shape[1] // lane_chunk):
        cs = slice(c * lane_chunk, (c + 1) * lane_chunk)
        ar = lr_ref[:, cs]
        ai = li_ref[:, cs]

        def step(t, carry, cs=cs, ar=ar, ai=ai):
            pr, pi = carry
            rows = pl.ds(pl.multiple_of(t * batch, batch), batch)
            nr = ar * pr - ai * pi + xr[rows, cs]
            ni = ar * pi + ai * pr + xi[rows, cs]
            xr[rows, cs] = nr
            xi[rows, cs] = ni
            return nr, ni

        pr, pi = lax.fori_loop(0, steps, step, (sr[:, cs], si[:, cs]), unroll=4)
        sr[:, cs] = pr
        si[:, cs] = pi

    ys = []
    for hf in range(2):
        xrb = xr[:, hf * hs:(hf + 1) * hs].astype(BF16)
        xib = xi[:, hf * hs:(hf + 1) * hs].astype(BF16)
        ys.append(jnp.dot(xrb, cre_ref[hf], preferred_element_type=F32)
                  - jnp.dot(xib, cim_ref[hf], preferred_element_type=F32))
    y = jnp.concatenate(ys, axis=1) + d_ref[...] * u.astype(F32)
    y = _gelu_tanh(y)
    z = jnp.dot(y.astype(BF16), wglu_ref[...], preferred_element_type=F32) + bglu_ref[...]
    o_ref[...] = (y * _sigmoid(z)).astype(BF16)


def _ssm(u, bre, bim, cre, cim, lbr, lbi, d, wglu, bglu, *, batch, steps, lane_chunk):
    t, w = u.shape
    ns = lbr.shape[1]
    rows = steps * batch
    const2 = lambda i: (0, 0)
    const3 = lambda i: (0, 0, 0)
    return pl.pallas_call(
        functools.partial(_ssm_body, steps=steps, batch=batch, lane_chunk=lane_chunk),
        grid=(t // rows,),
        in_specs=[pl.BlockSpec((rows, w), lambda i: (i, 0)),
                  pl.BlockSpec(bre.shape, const3), pl.BlockSpec(bim.shape, const3),
                  pl.BlockSpec(cre.shape, const3), pl.BlockSpec(cim.shape, const3),
                  pl.BlockSpec((1, ns), const2), pl.BlockSpec((1, ns), const2),
                  pl.BlockSpec((1, w), const2), pl.BlockSpec((w, w), const2), pl.BlockSpec((1, w), const2)],
        out_specs=pl.BlockSpec((rows, w), lambda i: (i, 0)),
        out_shape=jax.ShapeDtypeStruct((t, w), BF16),
        scratch_shapes=[pltpu.VMEM((rows, ns), F32), pltpu.VMEM((rows, ns), F32),
                        pltpu.VMEM((batch, ns), F32), pltpu.VMEM((batch, ns), F32)],
        compiler_params=_cparams(("arbitrary",)),
        name="s5_scan",
    )(u, bre, bim, cre, cim, lbr, lbi, d, wglu, bglu)


def _split_bf16(v):
    hi = v.astype(BF16)
    lo = (v - hi.astype(F32)).astype(BF16)
    return hi, lo


def _merge_body(x_ref, a_ref, s_ref, wg_ref, wba_ref, wbs_ref, wo_ref, g_ref, b_ref, *rest, with_router):
    if with_router:
        rhi_ref, rlo_ref, x1_ref, route_ref = rest
    else:
        (x1_ref,) = rest
    x = x_ref[...]
    d = x.shape[1]
    gates = jnp.dot(x.astype(BF16), wg_ref[...], preferred_element_type=F32)
    pa = jnp.dot(a_ref[...], wba_ref[...], preferred_element_type=F32)
    ps = jnp.dot(s_ref[...], wbs_ref[...], preferred_element_type=F32)
    merged = _sigmoid(gates[:, :d]) * pa + _sigmoid(gates[:, d:]) * ps
    y = jnp.dot(merged.astype(BF16), wo_ref[...], preferred_element_type=F32)
    x1 = _layer_norm(DEEPNORM_ALPHA * x + y, g_ref[...], b_ref[...])
    x1_ref[...] = x1
    if with_router:
        hi, lo = _split_bf16(x1)
        rhi = rhi_ref[...]
        logits = (jnp.dot(hi, rhi, preferred_element_type=F32)
                  + jnp.dot(lo, rhi, preferred_element_type=F32)
                  + jnp.dot(hi, rlo_ref[...], preferred_element_type=F32))
        lane = lax.broadcasted_iota(jnp.int32, logits.shape, 1).astype(F32)
        big = float(LANES)
        l1 = jnp.where(lane < N_EXPERTS, logits, -jnp.inf)
        m1 = jnp.max(l1, axis=-1, keepdims=True)
        i1 = jnp.min(jnp.where(l1 == m1, lane, big), axis=-1, keepdims=True)
        l2 = jnp.where(lane == i1, -jnp.inf, l1)
        m2 = jnp.max(l2, axis=-1, keepdims=True)
        i2 = jnp.min(jnp.where(l2 == m2, lane, big), axis=-1, keepdims=True)
        ed = jnp.exp(m2 - m1)
        g1 = 1.0 / (1.0 + ed)
        g2 = ed / (1.0 + ed)
        route = jnp.where(lane == 0.0, i1,
                          jnp.where(lane == 1.0, i2,
                                    jnp.where(lane == 2.0, g1, jnp.where(lane == 3.0, g2, 0.0))))
        route_ref[...] = route


def _merge(x, a, s, wg, wba, wbs, wo, g, b, router=None, *, tm):
    t, d = x.shape
    const = lambda i: (0, 0)
    row = lambda i: (i, 0)
    in_specs = [pl.BlockSpec((tm, d), row), pl.BlockSpec((tm, a.shape[1]), row), pl.BlockSpec((tm, s.shape[1]), row),
                pl.BlockSpec(wg.shape, const), pl.BlockSpec(wba.shape, const), pl.BlockSpec(wbs.shape, const),
                pl.BlockSpec(wo.shape, const), pl.BlockSpec((1, d), const), pl.BlockSpec((1, d), const)]
    args = [x, a, s, wg, wba, wbs, wo, g, b]
    out_specs = [pl.BlockSpec((tm, d), row)]
    out_shape = [jax.ShapeDtypeStruct((t, d), F32)]
    if router is not None:
        rhi, rlo = router
        in_specs += [pl.BlockSpec(rhi.shape, const), pl.BlockSpec(rlo.shape, const)]
        args += [rhi, rlo]
        out_specs.append(pl.BlockSpec((tm, LANES), row))
        out_shape.append(jax.ShapeDtypeStruct((t, LANES), F32))
    return pl.pallas_call(
        functools.partial(_merge_body, with_router=router is not None),
        grid=(t // tm,),
        in_specs=in_specs, out_specs=out_specs, out_shape=out_shape,
        compiler_params=_cparams(("parallel",)),
        name="merge_ln1",
    )(*args)


def _ple(xb, p_ref, wpg_ref, wpp_ref):
    gate = _sigmoid(jnp.dot(xb, wpg_ref[...], preferred_element_type=F32))
    return gate * jnp.dot(p_ref[...].astype(BF16), wpp_ref[...], preferred_element_type=F32)


def _ffn_body(x_ref, p_ref, wg_ref, wu_ref, wd_ref, wpg_ref, wpp_ref, g_ref, b_ref, o_ref, acc):
    x1 = x_ref[...]
    xb = x1.astype(BF16)
    for c, (c0, cw) in enumerate(FFN_CHUNKS):
        gt = jnp.dot(xb, wg_ref[:, c0:c0 + cw], preferred_element_type=F32)
        up = jnp.dot(xb, wu_ref[:, c0:c0 + cw], preferred_element_type=F32)
        act = (gt * _sigmoid(gt) * up).astype(BF16)
        contrib = jnp.dot(act, wd_ref[c0:c0 + cw, :], preferred_element_type=F32)
        if c == 0:
            acc[...] = contrib
        else:
            acc[...] += contrib
    h = DEEPNORM_ALPHA * x1 + acc[...] + _ple(xb, p_ref, wpg_ref, wpp_ref)
    o_ref[...] = _layer_norm(h, g_ref[...], b_ref[...])


def _resident(shape, index_map):
    return pl.BlockSpec(shape, index_map, pipeline_mode=pl.Buffered(1))


def _ffn(x1, p, w_gu, w_down, li, wpg, wpp, g, b, *, tm):
    t, d = x1.shape
    ff = w_down.shape[1]
    const = lambda i: (0, 0)
    row = lambda i: (i, 0)
    return pl.pallas_call(
        _ffn_body,
        grid=(t // tm,),
        in_specs=[pl.BlockSpec((tm, d), row), pl.BlockSpec((tm, p.shape[1]), row),
                  _resident((None, d, ff), lambda i: (li, 0, 0)),
                  _resident((None, d, ff), lambda i: (li, 0, 1)),
                  _resident((None, ff, d), lambda i: (li, 0, 0)),
                  _resident(wpg.shape, const), _resident(wpp.shape, const),
                  pl.BlockSpec((1, d), const), pl.BlockSpec((1, d), const)],
        out_specs=pl.BlockSpec((tm, d), row),
        out_shape=jax.ShapeDtypeStruct((t, d), F32),
        scratch_shapes=[pltpu.VMEM((tm, d), F32)],
        compiler_params=_cparams(("parallel",)),
        name="ffn_ln2",
    )(x1, p, w_gu, w_gu, w_down, wpg, wpp, g, b)


def _moe_body(te_ref, nu_ref, src_ref, x_hbm, wg_ref, wu_ref, wd_ref, o_ref, xg, xb, acc, sem,
              *, tm, nf, nt):
    i = pl.program_id(0)
    j = pl.program_id(1)
    slot = i % 2
    rps = -(-tm // nf)

    def issue(tile, dst_slot, r0, r1):
        def one(r, c):
            row = src_ref[tile * tm + r]
            pltpu.make_async_copy(x_hbm.at[pl.ds(row, 1), :], xg.at[dst_slot, pl.ds(r, 1), :],
                                  sem.at[dst_slot]).start()
            return c
        lax.fori_loop(r0, r1, one, 0)

    @pl.when((i == 0) & (j == 0))
    def _():
        issue(0, 0, 0, tm)

    @pl.when(i + 1 < nt)
    def _():
        issue(i + 1, 1 - slot, j * rps, jnp.minimum((j + 1) * rps, tm))

    @pl.when(j == 0)
    def _():
        pltpu.make_async_copy(x_hbm.at[pl.ds(0, tm), :], xg.at[slot], sem.at[slot]).wait()
        xb[...] = xg[slot].astype(BF16)
        acc[...] = jnp.zeros_like(acc)

    @pl.when(i < nu_ref[0])
    def _():
        x = xb[...]
        gt = jnp.dot(x, wg_ref[...], preferred_element_type=F32)
        up = jnp.dot(x, wu_ref[...], preferred_element_type=F32)
        act = (gt * _sigmoid(gt) * up).astype(BF16)
        acc[...] += jnp.dot(act, wd_ref[...], preferred_element_type=F32)

    @pl.when(j == nf - 1)
    def _():
        o_ref[...] = acc[...]


def _moe_experts(x1, w_gu, w_down, li, tile_expert, n_used, src, *, tm, fw):
    t, d = x1.shape
    ffe = w_down.shape[2]
    nf = ffe // fw
    r = src.shape[0]
    nt = r // tm

    def jj(i, j, nu):
        return jnp.where(i < nu[0], j, nf - 1)

    return pl.pallas_call(
        functools.partial(_moe_body, tm=tm, nf=nf, nt=nt),
        grid_spec=pltpu.PrefetchScalarGridSpec(
            num_scalar_prefetch=3,
            grid=(nt, nf),
            in_specs=[pl.BlockSpec(memory_space=pl.ANY),
                      pl.BlockSpec((None, None, d, fw), lambda i, j, te, nu, s: (li, te[i], 0, jj(i, j, nu))),
                      pl.BlockSpec((None, None, d, fw), lambda i, j, te, nu, s: (li, te[i], 0, nf + jj(i, j, nu))),
                      pl.BlockSpec((None, None, fw, d), lambda i, j, te, nu, s: (li, te[i], jj(i, j, nu), 0))],
            out_specs=pl.BlockSpec((tm, d), lambda i, j, te, nu, s: (i, 0)),
            scratch_shapes=[pltpu.VMEM((2, tm, d), F32), pltpu.VMEM((tm, d), BF16), pltpu.VMEM((tm, d), F32),
                            pltpu.SemaphoreType.DMA((2,))]),
        out_shape=jax.ShapeDtypeStruct((r, d), F32),
        compiler_params=_cparams(("arbitrary", "arbitrary")),
        name="moe_experts",
    )(tile_expert, n_used, src, x1, w_gu, w_gu, w_down)


def _moe_out_body(pos_ref, y_hbm, x_ref, route_ref, p_ref, wpg_ref, wpp_ref, g_ref, b_ref, o_ref, yg, sem,
                  *, tm, nt):
    i = pl.program_id(0)
    slot = i % 2

    def issue(tile, dst_slot):
        def one(r, c):
            base = 2 * (tile * tm + r)
            for k in range(2):
                pltpu.make_async_copy(y_hbm.at[pl.ds(pos_ref[base + k], 1), :],
                                      yg.at[dst_slot, k, pl.ds(r, 1), :], sem.at[dst_slot]).start()
            return c
        lax.fori_loop(0, tm, one, 0)

    @pl.when(i == 0)
    def _():
        issue(0, 0)

    @pl.when(i + 1 < nt)
    def _():
        issue(i + 1, 1 - slot)

    for k in range(2):
        pltpu.make_async_copy(y_hbm.at[pl.ds(0, tm), :], yg.at[slot, k], sem.at[slot]).wait()

    x1 = x_ref[...]
    route = route_ref[...]
    ffn = route[:, 2:3] * yg[slot, 0] + route[:, 3:4] * yg[slot, 1]
    h = DEEPNORM_ALPHA * x1 + ffn + _ple(x1.astype(BF16), p_ref, wpg_ref, wpp_ref)
    o_ref[...] = _layer_norm(h, g_ref[...], b_ref[...])


def _moe_out(x1, route, p, y_sorted, pos, wpg, wpp, g, b, *, tm):
    t, d = x1.shape
    nt = t // tm
    const = lambda i, s: (0, 0)
    row = lambda i, s: (i, 0)
    return pl.pallas_call(
        functools.partial(_moe_out_body, tm=tm, nt=nt),
        grid_spec=pltpu.PrefetchScalarGridSpec(
            num_scalar_prefetch=1,
            grid=(nt,),
            in_specs=[pl.BlockSpec(memory_space=pl.ANY),
                      pl.BlockSpec((tm, d), row), pl.BlockSpec((tm, LANES), row), pl.BlockSpec((tm, p.shape[1]), row),
                      pl.BlockSpec(wpg.shape, const), pl.BlockSpec(wpp.shape, const),
                      pl.BlockSpec((1, d), const), pl.BlockSpec((1, d), const)],
            out_specs=pl.BlockSpec((tm, d), row),
            scratch_shapes=[pltpu.VMEM((2, 2, tm, d), F32), pltpu.SemaphoreType.DMA((2,))]),
        out_shape=jax.ShapeDtypeStruct((t, d), F32),
        compiler_params=_cparams(("arbitrary",)),
        name="moe_combine_ln2",
    )(pos, y_sorted, x1, route, p, wpg, wpp, g, b)


def _routing_tables(route, *, tm, n_experts):
    t = route.shape[0]
    eids = route[:, :2].astype(jnp.int32).reshape(-1)
    onehot = (eids[:, None] == jnp.arange(n_experts, dtype=jnp.int32)[None, :]).astype(jnp.int32)
    csum = jnp.cumsum(onehot, axis=0)
    rank = jnp.sum((csum - onehot) * onehot, axis=1)
    cnt = csum[-1]
    ntile = (cnt + tm - 1) // tm
    tile_end = jnp.cumsum(ntile)
    row_off = (tile_end - ntile) * tm
    pos = jnp.sum(onehot * row_off[None, :], axis=1) + rank
    r = 2 * t + n_experts * tm
    src = jnp.zeros((r,), jnp.int32).at[pos].set(jnp.arange(2 * t, dtype=jnp.int32) // 2)
    n_used = tile_end[-1:]
    tiles = jnp.arange(r // tm, dtype=jnp.int32)
    te = jnp.sum((tiles[:, None] >= tile_end[None, :]).astype(jnp.int32), axis=1)
    last = jnp.sum((n_used - 1 >= tile_end).astype(jnp.int32))
    te = jnp.where(tiles < n_used, te, last).astype(jnp.int32)
    return te, n_used.astype(jnp.int32), src, pos.astype(jnp.int32)


def kernel(x, p, rel_bias, w_in, attn_sinks, ssm_lambda_re, ssm_lambda_im, ssm_log_dt, ssm_b_re, ssm_b_im,
           ssm_c_re, ssm_c_im, ssm_d, w_glu, b_glu, w_branch_attn, w_branch_ssm, w_out, ln1_g, ln1_b,
           ffn_w_gate_up, ffn_w_down, moe_router, moe_w_gate_up, moe_w_down, ple_w_proj, ple_w_gate,
           ln2_g, ln2_b):
    bsz, seq, d = x.shape
    t = bsz * seq
    depth = w_in.shape[0]

    xt = jnp.transpose(x, (1, 0, 2)).reshape(t, d)
    pt = jnp.transpose(p, (0, 2, 1, 3)).reshape(depth, t, p.shape[-1])

    bias = _attn_bias_table(rel_bias)
    ffn_gu = ffn_w_gate_up.astype(BF16)
    ffn_dn = ffn_w_down.astype(BF16)
    moe_gu = moe_w_gate_up.astype(BF16)
    moe_dn = moe_w_down.astype(BF16)

    aw, kw = ATTN_WIDTH, KV_WIDTH
    for i in range(depth):
        wi = w_in[i]
        u0 = aw + 2 * kw
        w_proj = jnp.concatenate([wi[:, :aw], _per_head_kv_columns(wi[:, aw:aw + kw]),
                                  _per_head_kv_columns(wi[:, aw + kw:u0]), wi[:, u0:u0 + SSM_WIDTH]],
                                 axis=1).astype(BF16)
        w_gates = wi[:, u0 + SSM_WIDTH:].astype(BF16)

        q, kv, u = _inproj(xt, w_proj, tm=TM_PROJ)
        a_out = _attention(q, kv, attn_sinks[i], bias, seq=seq, batch=bsz)

        lbr, lbi, bbr, bbi = _ssm_params(ssm_lambda_re[i], ssm_lambda_im[i], ssm_log_dt[i],
                                         ssm_b_re[i], ssm_b_im[i])
        bre = _block_diag_halves(bbr).astype(BF16)
        bim = _block_diag_halves(bbi).astype(BF16)
        cre = _block_diag_halves(jnp.transpose(ssm_c_re[i], (0, 2, 1))).astype(BF16)
        cim = _block_diag_halves(jnp.transpose(ssm_c_im[i], (0, 2, 1))).astype(BF16)
        s_out = _ssm(u, bre, bim, cre, cim, lbr.reshape(1, -1), lbi.reshape(1, -1),
                     ssm_d[i].reshape(1, -1), w_glu[i].astype(BF16), b_glu[i].reshape(1, -1),
                     batch=bsz, steps=SSM_STEPS, lane_chunk=SSM_LANE_CHUNK)

        g1, b1 = ln1_g[i].reshape(1, d), ln1_b[i].reshape(1, d)
        g2, b2 = ln2_g[i].reshape(1, d), ln2_b[i].reshape(1, d)
        wpg = ple_w_gate[i].astype(BF16)
        wpp = ple_w_proj[i].astype(BF16)
        merge_w = (w_gates, w_branch_attn[i].astype(BF16), w_branch_ssm[i].astype(BF16), w_out[i].astype(BF16))
        if i % 2 == 0:
            (x1,) = _merge(xt, a_out, s_out, *merge_w, g1, b1, tm=TM_MERGE)
            xt = _ffn(x1, pt[i], ffn_gu, ffn_dn, i // 2, wpg, wpp, g2, b2, tm=TM_FFN)
        else:
            rt = jnp.pad(moe_router[i // 2], ((0, 0), (0, LANES - N_EXPERTS)))
            rhi = rt.astype(BF16)
            rlo = (rt - rhi.astype(F32)).astype(BF16)
            x1, route = _merge(xt, a_out, s_out, *merge_w, g1, b1, (rhi, rlo), tm=TM_MERGE)
            te, n_used, src, pos = _routing_tables(route, tm=TM_MOE, n_experts=N_EXPERTS)
            y_sorted = _moe_experts(x1, moe_gu, moe_dn, i // 2, te, n_used, src, tm=TM_MOE, fw=FW_MOE)
            xt = _moe_out(x1, route, pt[i], y_sorted, pos, wpg, wpp, g2, b2, tm=TM_FFN)

    return jnp.transpose(xt.reshape(seq, bsz, d), (1, 0, 2))
```

```python
import functools
import math

import jax
import jax.numpy as jnp
import numpy as np
from jax import lax
from jax.experimental import pallas as pl
from jax.experimental.pallas import tpu as pltpu

F32 = jnp.float32
BF16 = jnp.bfloat16

D_MODEL = 1024
BATCH = 16
SEQ = 2048
DEPTH = 4
HEAD_DIM = 64
N_Q_HEADS = 8
N_KV_HEADS = 2
ATTN_WIDTH = N_Q_HEADS * HEAD_DIM
KV_WIDTH = N_KV_HEADS * HEAD_DIM
ATTN_BLOCK = 128
N_BUCKETS = 32
MAX_DISTANCE = 128
SSM_WIDTH = 512
SSM_GROUP_CH = 16
SSM_GROUPS = SSM_WIDTH // SSM_GROUP_CH
SSM_STATE = 64
D_FF = 2816
N_EXPERTS = 8
D_FF_EXPERT = 3584
PLE_DIM = 256
DEEPNORM_ALPHA = (2 * DEPTH) ** 0.25
LN_EPS = 1e-5
NEG_INF = -1e30

LANES = 128
VMEM_LIMIT_BYTES = 56 * 1024 * 1024

TM_PROJ = 1024
TM_MERGE = 512
TM_FFN = 512
TM_MOE = 512
FW_MOE = 512
SSM_STEPS = 32
SSM_PERM_STEPS = 16
SSM_LANE_CHUNK = 512
FFN_CHUNKS = ((0, 768), (768, 768), (1536, 768), (2304, 512))


def _cparams(sem):
    return pltpu.CompilerParams(dimension_semantics=sem, vmem_limit_bytes=VMEM_LIMIT_BYTES)


def _layer_norm(h, g, b):
    mu = jnp.mean(h, axis=-1, keepdims=True)
    c = h - mu
    var = jnp.mean(c * c, axis=-1, keepdims=True)
    return c * lax.rsqrt(var + LN_EPS) * g + b


def _sigmoid(v):
    return 1.0 / (1.0 + jnp.exp(-v))


def _gelu_tanh(v):
    return 0.5 * v * (1.0 + jnp.tanh(math.sqrt(2.0 / math.pi) * (v + 0.044715 * (v * v * v))))


def _inproj_body(x_ref, w_ref, q_ref, kv_ref, u_ref):
    xb = x_ref[...].astype(BF16)
    z = jnp.dot(xb, w_ref[...], preferred_element_type=F32)
    aw = q_ref.shape[1]
    kw = kv_ref.shape[1]
    q_ref[...] = (z[:, :aw] * (HEAD_DIM ** -0.5)).astype(BF16)
    kv_ref[...] = z[:, aw:aw + kw].astype(BF16)
    u_ref[...] = z[:, aw + kw:].astype(BF16)


def _inproj(x, w, *, tm):
    t, d = x.shape
    n = w.shape[1]
    aw, kw = ATTN_WIDTH, 2 * (N_KV_HEADS * 2 * LANES)
    uw = n - aw - kw
    return pl.pallas_call(
        _inproj_body,
        grid=(t // tm,),
        in_specs=[pl.BlockSpec((tm, d), lambda i: (i, 0)),
                  pl.BlockSpec((d, n), lambda i: (0, 0))],
        out_specs=[pl.BlockSpec((tm, aw), lambda i: (i, 0)),
                   pl.BlockSpec((tm, kw), lambda i: (i, 0)),
                   pl.BlockSpec((tm, uw), lambda i: (i, 0))],
        out_shape=[jax.ShapeDtypeStruct((t, aw), BF16),
                   jax.ShapeDtypeStruct((t, kw), BF16),
                   jax.ShapeDtypeStruct((t, uw), BF16)],
        compiler_params=_cparams(("parallel",)),
        name="inproj",
    )(x, w)


def _attn_body(sink_ref, q_ref, kvc_ref, kvp_ref, bias_ref, o_ref):
    n = pl.program_id(1)
    blk = q_ref.shape[0]
    col = lax.broadcasted_iota(jnp.int32, (1, 2 * blk), 1)
    first = jnp.where((n == 0) & (col < blk), NEG_INF, 0.0).astype(F32)
    v0 = N_KV_HEADS * 2 * LANES
    group = N_Q_HEADS // N_KV_HEADS
    for tile in range(N_Q_HEADS // 2):
        qt = q_ref[:, tile * LANES:(tile + 1) * LANES]
        acc = None
        for par in range(2):
            h = 2 * tile + par
            kt = 2 * (h // group) + par
            ks = slice(kt * LANES, (kt + 1) * LANES)
            vs = slice(v0 + kt * LANES, v0 + (kt + 1) * LANES)
            k = jnp.concatenate([kvp_ref[:, ks], kvc_ref[:, ks]], axis=0)
            v = jnp.concatenate([kvp_ref[:, vs], kvc_ref[:, vs]], axis=0)
            s = lax.dot_general(qt, k, (((1,), (1,)), ((), ())), preferred_element_type=F32)
            s = s + bias_ref[h] + first
            sink = sink_ref[h]
            m = jnp.maximum(jnp.max(s, axis=-1, keepdims=True), sink)
            e = jnp.exp(s - m)
            denom = jnp.sum(e, axis=-1, keepdims=True) + jnp.exp(sink - m)
            o = jnp.dot(e.astype(BF16), v, preferred_element_type=F32) / denom
            acc = o if acc is None else acc + o
        o_ref[:, tile * LANES:(tile + 1) * LANES] = acc.astype(BF16)


def _attention(q, kv, sinks, bias, *, seq, batch):
    t = q.shape[0]
    blk = ATTN_BLOCK
    nb = seq // blk
    aw, kw = q.shape[1], kv.shape[1]
    return pl.pallas_call(
        _attn_body,
        grid=(batch, nb),
        in_specs=[pl.BlockSpec(memory_space=pltpu.SMEM),
                  pl.BlockSpec((blk, aw), lambda b, n: (b * nb + n, 0)),
                  pl.BlockSpec((blk, kw), lambda b, n: (b * nb + n, 0)),
                  pl.BlockSpec((blk, kw), lambda b, n: (b * nb + jnp.maximum(n - 1, 0), 0)),
                  pl.BlockSpec((N_Q_HEADS, blk, 2 * blk), lambda b, n: (0, 0, 0))],
        out_specs=pl.BlockSpec((blk, aw), lambda b, n: (b * nb + n, 0)),
        out_shape=jax.ShapeDtypeStruct((t, aw), BF16),
        compiler_params=_cparams(("parallel", "parallel")),
        name="swa_attention",
    )(sinks, q, kv, kv, bias)


def _per_head_kv_columns(w):
    d = w.shape[0]
    wh = w.reshape(d, N_KV_HEADS, 1, 1, HEAD_DIM)
    half = jnp.arange(2)[None, None, None, :, None]
    par = jnp.arange(2)[None, None, :, None, None]
    tiles = jnp.where(half == par, wh, 0.0)
    return tiles.reshape(d, N_KV_HEADS * 2 * LANES)


def _attn_bias_table(rel_bias):
    blk = ATTN_BLOCK
    qi = np.arange(blk, dtype=np.int32)[:, None]
    kj = np.arange(2 * blk, dtype=np.int32)[None, :]
    dist = qi + blk - kj
    valid = (dist >= 0) & (dist < blk)
    d = np.maximum(dist, 0)
    max_exact = N_BUCKETS // 2
    d_f = np.maximum(d, 1).astype(np.float32)
    large = max_exact + (np.log(d_f / np.float32(max_exact)) / np.float32(math.log(MAX_DISTANCE / max_exact))
                         * np.float32(N_BUCKETS - max_exact)).astype(np.int32)
    large = np.minimum(large, N_BUCKETS - 1)
    bucket = np.where(d < max_exact, d, large)
    tbl = jnp.transpose(rel_bias.astype(F32)[bucket], (2, 0, 1))
    return jnp.where(valid[None], tbl, NEG_INF)


def _ssm_param_body(lre_ref, lim_ref, ldt_ref, bre_ref, bim_ref, lbr_ref, lbi_ref, bbr_ref, bbi_ref):
    lre, lim = lre_ref[...], lim_ref[...]
    dt = jnp.exp(ldt_ref[...])
    mag = jnp.exp(lre * dt)
    lbr = mag * jnp.cos(lim * dt)
    lbi = mag * jnp.sin(lim * dt)
    lbr_ref[...] = lbr
    lbi_ref[...] = lbi
    nr, ni = lbr - 1.0, lbi
    inv = 1.0 / (lre * lre + lim * lim)
    cr = (nr * lre + ni * lim) * inv
    ci = (ni * lre - nr * lim) * inv
    br, bi = bre_ref[...], bim_ref[...]
    bbr_ref[...] = cr * br - ci * bi
    bbi_ref[...] = cr * bi + ci * br


def _ssm_params(lam_re, lam_im, log_dt, b_re, b_im):
    g, p = lam_re.shape
    h = b_re.shape[2]
    ldt = jnp.broadcast_to(log_dt[:, None, None], (g, 1, p))
    brt = jnp.transpose(b_re, (0, 2, 1))
    bit = jnp.transpose(b_im, (0, 2, 1))
    sd = jax.ShapeDtypeStruct
    return pl.pallas_call(
        _ssm_param_body,
        out_shape=[sd((g, 1, p), F32), sd((g, 1, p), F32), sd((g, h, p), F32), sd((g, h, p), F32)],
        name="ssm_params",
    )(lam_re.reshape(g, 1, p), lam_im.reshape(g, 1, p), ldt, brt, bit)


def _block_diag_halves(w):
    g, a, b = w.shape
    hg = g // 2
    wh = w.reshape(2, hg, a, b)
    eye = jnp.eye(hg, dtype=bool)[None, :, None, :, None]
    full = jnp.where(eye, wh[:, :, :, None, :], 0.0)
    return full.reshape(2, hg * a, hg * b)


def _ssm_body(u_ref, perm_ref, permt_ref, bre_ref, bim_ref, cre_ref, cim_ref, lr_ref, li_ref, d_ref,
              wglu_ref, bglu_ref, o_ref, xr, xi, sr, si, *, steps, batch, lane_chunk):
    @pl.when(pl.program_id(0) == 0)
    def _():
        sr[...] = jnp.zeros_like(sr)
        si[...] = jnp.zeros_like(si)

    sub = SSM_PERM_STEPS
    parts = []
    for g in range(steps // sub):
        bm = jnp.concatenate([u_ref[b, g * sub:(g + 1) * sub, :] for b in range(batch)], axis=0)
        parts.append(jnp.dot(perm_ref[...], bm, preferred_element_type=F32).astype(BF16))
    u = jnp.concatenate(parts, axis=0)
    hw = u.shape[1] // 2
    hs = xr.shape[1] // 2
    for hf in range(2):
        uh = u[:, hf * hw:(hf + 1) * hw]
        xr[:, hf * hs:(hf + 1) * hs] = jnp.dot(uh, bre_ref[hf], preferred_element_type=F32)
        xi[:, hf * hs:(hf + 1) * hs] = jnp.dot(uh, bim_ref[hf], preferred_element_type=F32)

    for c in range(xr.shape[1] // lane_chunk):
        cs = slice(c * lane_chunk, (c + 1) * lane_chunk)
        ar = lr_ref[:, cs]
        ai = li_ref[:, cs]

        def step(t, carry, cs=cs, ar=ar, ai=ai):
            pr, pi = carry
            rows = pl.ds(pl.multiple_of(t * batch, batch), batch)
            nr = ar * pr - ai * pi + xr[rows, cs]
            ni = ar * pi + ai * pr + xi[rows, cs]
            xr[rows, cs] = nr
            xi[rows, cs] = ni
            return nr, ni

        pr, pi = lax.fori_loop(0, steps, step, (sr[:, cs], si[:, cs]), unroll=4)
        sr[:, cs] = pr
        si[:, cs] = pi

    ys = []
    for hf in range(2):
        xrb = xr[:, hf * hs:(hf + 1) * hs].astype(BF16)
        xib = xi[:, hf * hs:(hf + 1) * hs].astype(BF16)
        ys.append(jnp.dot(xrb, cre_ref[hf], preferred_element_type=F32)
                  - jnp.dot(xib, cim_ref[hf], preferred_element_type=F32))
    y = jnp.concatenate(ys, axis=1) + d_ref[...] * u.astype(F32)
    y = _gelu_tanh(y)
    z = jnp.dot(y.astype(BF16), wglu_ref[...], preferred_element_type=F32) + bglu_ref[...]
    out = (y * _sigmoid(z)).astype(BF16)
    rows_g = sub * batch
    for g in range(steps // sub):
        bm = jnp.dot(permt_ref[...], out[g * rows_g:(g + 1) * rows_g], preferred_element_type=F32).astype(BF16)
        for b in range(batch):
            o_ref[b, g * sub:(g + 1) * sub, :] = bm[b * sub:(b + 1) * sub]


def _time_major_permutation(batch, sub):
    n = batch * sub
    r = np.arange(n)
    p = np.zeros((n, n), np.float32)
    p[r, (r % batch) * sub + r // batch] = 1.0
    return p


def _ssm(u, bre, bim, cre, cim, lbr, lbi, d, wglu, bglu, *, batch, steps, lane_chunk):
    t, w = u.shape
    seq = t // batch
    ns = lbr.shape[1]
    rows = steps * batch
    perm = _time_major_permutation(batch, SSM_PERM_STEPS)
    const2 = lambda i: (0, 0)
    const3 = lambda i: (0, 0, 0)
    out = pl.pallas_call(
        functools.partial(_ssm_body, steps=steps, batch=batch, lane_chunk=lane_chunk),
        grid=(seq // steps,),
        in_specs=[pl.BlockSpec((batch, steps, w), lambda i: (0, i, 0)),
                  pl.BlockSpec(perm.shape, const2), pl.BlockSpec(perm.shape, const2),
                  pl.BlockSpec(bre.shape, const3), pl.BlockSpec(bim.shape, const3),
                  pl.BlockSpec(cre.shape, const3), pl.BlockSpec(cim.shape, const3),
                  pl.BlockSpec((1, ns), const2), pl.BlockSpec((1, ns), const2),
                  pl.BlockSpec((1, w), const2), pl.BlockSpec((w, w), const2), pl.BlockSpec((1, w), const2)],
        out_specs=pl.BlockSpec((batch, steps, w), lambda i: (0, i, 0)),
        out_shape=jax.ShapeDtypeStruct((batch, seq, w), BF16),
        scratch_shapes=[pltpu.VMEM((rows, ns), F32), pltpu.VMEM((rows, ns), F32),
                        pltpu.VMEM((batch, ns), F32), pltpu.VMEM((batch, ns), F32)],
        compiler_params=_cparams(("arbitrary",)),
        name="s5_scan",
    )(u.reshape(batch, seq, w), jnp.asarray(perm, BF16), jnp.asarray(perm.T, BF16),
      bre, bim, cre, cim, lbr, lbi, d, wglu, bglu)
    return out.reshape(t, w)


def _split_bf16(v):
    hi = v.astype(BF16)
    lo = (v - hi.astype(F32)).astype(BF16)
    return hi, lo


def _merge_body(x_ref, a_ref, s_ref, wg_ref, wba_ref, wbs_ref, wo_ref, g_ref, b_ref, *rest, with_router):
    if with_router:
        rhi_ref, rlo_ref, x1_ref, route_ref = rest
    else:
        (x1_ref,) = rest
    x = x_ref[...]
    d = x.shape[1]
    gates = jnp.dot(x.astype(BF16), wg_ref[...], preferred_element_type=F32)
    pa = jnp.dot(a_ref[...], wba_ref[...], preferred_element_type=F32)
    ps = jnp.dot(s_ref[...], wbs_ref[...], preferred_element_type=F32)
    merged = _sigmoid(gates[:, :d]) * pa + _sigmoid(gates[:, d:]) * ps
    y = jnp.dot(merged.astype(BF16), wo_ref[...], preferred_element_type=F32)
    x1 = _layer_norm(DEEPNORM_ALPHA * x + y, g_ref[...], b_ref[...])
    x1_ref[...] = x1
    if with_router:
        hi, lo = _split_bf16(x1)
        rhi = rhi_ref[...]
        logits = (jnp.dot(hi, rhi, preferred_element_type=F32)
                  + jnp.dot(lo, rhi, preferred_element_type=F32)
                  + jnp.dot(hi, rlo_ref[...], preferred_element_type=F32))
        lane = lax.broadcasted_iota(jnp.int32, logits.shape, 1).astype(F32)
        big = float(LANES)
        l1 = jnp.where(lane < N_EXPERTS, logits, -jnp.inf)
        m1 = jnp.max(l1, axis=-1, keepdims=True)
        i1 = jnp.min(jnp.where(l1 == m1, lane, big), axis=-1, keepdims=True)
        l2 = jnp.where(lane == i1, -jnp.inf, l1)
        m2 = jnp.max(l2, axis=-1, keepdims=True)
        i2 = jnp.min(jnp.where(l2 == m2, lane, big), axis=-1, keepdims=True)
        ed = jnp.exp(m2 - m1)
        g1 = 1.0 / (1.0 + ed)
        g2 = ed / (1.0 + ed)
        route = jnp.where(lane == 0.0, i1,
                          jnp.where(lane == 1.0, i2,
                                    jnp.where(lane == 2.0, g1, jnp.where(lane == 3.0, g2, 0.0))))
        route_ref[...] = route


def _merge(x, a, s, wg, wba, wbs, wo, g, b, router=None, *, tm):
    t, d = x.shape
    const = lambda i: (0, 0)
    row = lambda i: (i, 0)
    in_specs = [pl.BlockSpec((tm, d), row), pl.BlockSpec((tm, a.shape[1]), row), pl.BlockSpec((tm, s.shape[1]), row),
                pl.BlockSpec(wg.shape, const), pl.BlockSpec(wba.shape, const), pl.BlockSpec(wbs.shape, const),
                pl.BlockSpec(wo.shape, const), pl.BlockSpec((1, d), const), pl.BlockSpec((1, d), const)]
    args = [x, a, s, wg, wba, wbs, wo, g, b]
    out_specs = [pl.BlockSpec((tm, d), row)]
    out_shape = [jax.ShapeDtypeStruct((t, d), F32)]
    if router is not None:
        rhi, rlo = router
        in_specs += [pl.BlockSpec(rhi.shape, const), pl.BlockSpec(rlo.shape, const)]
        args += [rhi, rlo]
        out_specs.append(pl.BlockSpec((tm, LANES), row))
        out_shape.append(jax.ShapeDtypeStruct((t, LANES), F32))
    return pl.pallas_call(
        functools.partial(_merge_body, with_router=router is not None),
        grid=(t // tm,),
        in_specs=in_specs, out_specs=out_specs, out_shape=out_shape,
        compiler_params=_cparams(("parallel",)),
        name="merge_ln1",
    )(*args)


def _ple(xb, p_ref, wpg_ref, wpp_ref):
    gate = _sigmoid(jnp.dot(xb, wpg_ref[...], preferred_element_type=F32))
    return gate * jnp.dot(p_ref[...].astype(BF16), wpp_ref[...], preferred_element_type=F32)


def _ffn_body(x_ref, p_ref, wg_ref, wu_ref, wd_ref, wpg_ref, wpp_ref, g_ref, b_ref, o_ref, acc):
    x1 = x_ref[...]
    xb = x1.astype(BF16)
    for c, (c0, cw) in enumerate(FFN_CHUNKS):
        gt = jnp.dot(xb, wg_ref[:, c0:c0 + cw], preferred_element_type=F32)
        up = jnp.dot(xb, wu_ref[:, c0:c0 + cw], preferred_element_type=F32)
        act = (gt * _sigmoid(gt) * up).astype(BF16)
        contrib = jnp.dot(act, wd_ref[c0:c0 + cw, :], preferred_element_type=F32)
        if c == 0:
            acc[...] = contrib
        else:
            acc[...] += contrib
    h = DEEPNORM_ALPHA * x1 + acc[...] + _ple(xb, p_ref, wpg_ref, wpp_ref)
    o_ref[...] = _layer_norm(h, g_ref[...], b_ref[...])


def _resident(shape, index_map):
    return pl.BlockSpec(shape, index_map, pipeline_mode=pl.Buffered(1))


def _ffn(x1, p, w_gu, w_down, li, wpg, wpp, g, b, *, tm):
    t, d = x1.shape
    ff = w_down.shape[1]
    const = lambda i: (0, 0)
    row = lambda i: (i, 0)
    return pl.pallas_call(
        _ffn_body,
        grid=(t // tm,),
        in_specs=[pl.BlockSpec((tm, d), row), pl.BlockSpec((tm, p.shape[1]), row),
                  _resident((None, d, ff), lambda i: (li, 0, 0)),
                  _resident((None, d, ff), lambda i: (li, 0, 1)),
                  _resident((None, ff, d), lambda i: (li, 0, 0)),
                  _resident(wpg.shape, const), _resident(wpp.shape, const),
                  pl.BlockSpec((1, d), const), pl.BlockSpec((1, d), const)],
        out_specs=pl.BlockSpec((tm, d), row),
        out_shape=jax.ShapeDtypeStruct((t, d), F32),
        scratch_shapes=[pltpu.VMEM((tm, d), F32)],
        compiler_params=_cparams(("parallel",)),
        name="ffn_ln2",
    )(x1, p, w_gu, w_gu, w_down, wpg, wpp, g, b)


def _moe_body(te_ref, nu_ref, src_ref, x_hbm, wg_ref, wu_ref, wd_ref, o_ref, xg, xb, acc, sem,
              *, tm, nf, nt):
    i = pl.program_id(0)
    j = pl.program_id(1)
    slot = i % 2
    rps = -(-tm // nf)

    def issue(tile, dst_slot, r0, r1):
        def one(r, c):
            row = src_ref[tile * tm + r]
            pltpu.make_async_copy(x_hbm.at[pl.ds(row, 1), :], xg.at[dst_slot, pl.ds(r, 1), :],
                                  sem.at[dst_slot]).start()
            return c
        lax.fori_loop(r0, r1, one, 0)

    @pl.when((i == 0) & (j == 0))
    def _():
        issue(0, 0, 0, tm)

    @pl.when(i + 1 < nt)
    def _():
        issue(i + 1, 1 - slot, j * rps, jnp.minimum((j + 1) * rps, tm))

    @pl.when(j == 0)
    def _():
        pltpu.make_async_copy(x_hbm.at[pl.ds(0, tm), :], xg.at[slot], sem.at[slot]).wait()
        xb[...] = xg[slot].astype(BF16)
        acc[...] = jnp.zeros_like(acc)

    @pl.when(i < nu_ref[0])
    def _():
        x = xb[...]
        gt = jnp.dot(x, wg_ref[...], preferred_element_type=F32)
        up = jnp.dot(x, wu_ref[...], preferred_element_type=F32)
        act = (gt * _sigmoid(gt) * up).astype(BF16)
        acc[...] += jnp.dot(act, wd_ref[...], preferred_element_type=F32)

    @pl.when(j == nf - 1)
    def _():
        o_ref[...] = acc[...]


def _moe_experts(x1, w_gu, w_down, li, tile_expert, n_used, src, *, tm, fw):
    t, d = x1.shape
    ffe = w_down.shape[2]
    nf = ffe // fw
    r = src.shape[0]
    nt = r // tm

    def jj(i, j, nu):
        return jnp.where(i < nu[0], j, nf - 1)

    return pl.pallas_call(
        functools.partial(_moe_body, tm=tm, nf=nf, nt=nt),
        grid_spec=pltpu.PrefetchScalarGridSpec(
            num_scalar_prefetch=3,
            grid=(nt, nf),
            in_specs=[pl.BlockSpec(memory_space=pl.ANY),
                      pl.BlockSpec((None, None, d, fw), lambda i, j, te, nu, s: (li, te[i], 0, jj(i, j, nu))),
                      pl.BlockSpec((None, None, d, fw), lambda i, j, te, nu, s: (li, te[i], 0, nf + jj(i, j, nu))),
                      pl.BlockSpec((None, None, fw, d), lambda i, j, te, nu, s: (li, te[i], jj(i, j, nu), 0))],
            out_specs=pl.BlockSpec((tm, d), lambda i, j, te, nu, s: (i, 0)),
            scratch_shapes=[pltpu.VMEM((2, tm, d), F32), pltpu.VMEM((tm, d), BF16), pltpu.VMEM((tm, d), F32),
                            pltpu.SemaphoreType.DMA((2,))]),
        out_shape=jax.ShapeDtypeStruct((r, d), F32),
        compiler_params=_cparams(("arbitrary", "arbitrary")),
        name="moe_experts",
    )(tile_expert, n_used, src, x1, w_gu, w_gu, w_down)


def _moe_out_body(pos_ref, y_hbm, x_ref, route_ref, p_ref, wpg_ref, wpp_ref, g_ref, b_ref, o_ref, yg, sem,
                  *, tm, nt):
    i = pl.program_id(0)
    slot = i % 2

    def issue(tile, dst_slot):
        def one(r, c):
            base = 2 * (tile * tm + r)
            for k in range(2):
                pltpu.make_async_copy(y_hbm.at[pl.ds(pos_ref[base + k], 1), :],
                                      yg.at[dst_slot, k, pl.ds(r, 1), :], sem.at[dst_slot]).start()
            return c
        lax.fori_loop(0, tm, one, 0)

    @pl.when(i == 0)
    def _():
        issue(0, 0)

    @pl.when(i + 1 < nt)
    def _():
        issue(i + 1, 1 - slot)

    for k in range(2):
        pltpu.make_async_copy(y_hbm.at[pl.ds(0, tm), :], yg.at[slot, k], sem.at[slot]).wait()

    x1 = x_ref[...]
    route = route_ref[...]
    ffn = route[:, 2:3] * yg[slot, 0] + route[:, 3:4] * yg[slot, 1]
    h = DEEPNORM_ALPHA * x1 + ffn + _ple(x1.astype(BF16), p_ref, wpg_ref, wpp_ref)
    o_ref[...] = _layer_norm(h, g_ref[...], b_ref[...])


def _moe_out(x1, route, p, y_sorted, pos, wpg, wpp, g, b, *, tm):
    t, d = x1.shape
    nt = t // tm
    const = lambda i, s: (0, 0)
    row = lambda i, s: (i, 0)
    return pl.pallas_call(
        functools.partial(_moe_out_body, tm=tm, nt=nt),
        grid_spec=pltpu.PrefetchScalarGridSpec(
            num_scalar_prefetch=1,
            grid=(nt,),
            in_specs=[pl.BlockSpec(memory_space=pl.ANY),
                      pl.BlockSpec((tm, d), row), pl.BlockSpec((tm, LANES), row), pl.BlockSpec((tm, p.shape[1]), row),
                      pl.BlockSpec(wpg.shape, const), pl.BlockSpec(wpp.shape, const),
                      pl.BlockSpec((1, d), const), pl.BlockSpec((1, d), const)],
            out_specs=pl.BlockSpec((tm, d), row),
            scratch_shapes=[pltpu.VMEM((2, 2, tm, d), F32), pltpu.SemaphoreType.DMA((2,))]),
        out_shape=jax.ShapeDtypeStruct((t, d), F32),
        compiler_params=_cparams(("arbitrary",)),
        name="moe_combine_ln2",
    )(pos, y_sorted, x1, route, p, wpg, wpp, g, b)


def _routing_tables(route, *, tm, n_experts):
    t = route.shape[0]
    eids = route[:, :2].astype(jnp.int32).reshape(-1)
    onehot = (eids[:, None] == jnp.arange(n_experts, dtype=jnp.int32)[None, :]).astype(jnp.int32)
    csum = jnp.cumsum(onehot, axis=0)
    rank = jnp.sum((csum - onehot) * onehot, axis=1)
    cnt = csum[-1]
    ntile = (cnt + tm - 1) // tm
    tile_end = jnp.cumsum(ntile)
    row_off = (tile_end - ntile) * tm
    pos = jnp.sum(onehot * row_off[None, :], axis=1) + rank
    r = 2 * t + n_experts * tm
    src = jnp.zeros((r,), jnp.int32).at[pos].set(jnp.arange(2 * t, dtype=jnp.int32) // 2)
    n_used = tile_end[-1:]
    tiles = jnp.arange(r // tm, dtype=jnp.int32)
    te = jnp.sum((tiles[:, None] >= tile_end[None, :]).astype(jnp.int32), axis=1)
    last = jnp.sum((n_used - 1 >= tile_end).astype(jnp.int32))
    te = jnp.where(tiles < n_used, te, last).astype(jnp.int32)
    return te, n_used.astype(jnp.int32), src, pos.astype(jnp.int32)


def kernel(x, p, rel_bias, w_in, attn_sinks, ssm_lambda_re, ssm_lambda_im, ssm_log_dt, ssm_b_re, ssm_b_im,
           ssm_c_re, ssm_c_im, ssm_d, w_glu, b_glu, w_branch_attn, w_branch_ssm, w_out, ln1_g, ln1_b,
           ffn_w_gate_up, ffn_w_down, moe_router, moe_w_gate_up, moe_w_down, ple_w_proj, ple_w_gate,
           ln2_g, ln2_b):
    bsz, seq, d = x.shape
    t = bsz * seq
    depth = w_in.shape[0]

    xt = x.reshape(t, d)
    pt = p.reshape(depth, t, p.shape[-1])

    bias = _attn_bias_table(rel_bias)
    ffn_gu = ffn_w_gate_up.astype(BF16)
    ffn_dn = ffn_w_down.astype(BF16)
    moe_gu = moe_w_gate_up.astype(BF16)
    moe_dn = moe_w_down.astype(BF16)

    aw, kw = ATTN_WIDTH, KV_WIDTH
    for i in range(depth):
        wi = w_in[i]
        u0 = aw + 2 * kw
        w_proj = jnp.concatenate([wi[:, :aw], _per_head_kv_columns(wi[:, aw:aw + kw]),
                                  _per_head_kv_columns(wi[:, aw + kw:u0]), wi[:, u0:u0 + SSM_WIDTH]],
                                 axis=1).astype(BF16)
        w_gates = wi[:, u0 + SSM_WIDTH:].astype(BF16)

        q, kv, u = _inproj(xt, w_proj, tm=TM_PROJ)
        a_out = _attention(q, kv, attn_sinks[i], bias, seq=seq, batch=bsz)

        lbr, lbi, bbr, bbi = _ssm_params(ssm_lambda_re[i], ssm_lambda_im[i], ssm_log_dt[i],
                                         ssm_b_re[i], ssm_b_im[i])
        bre = _block_diag_halves(bbr).astype(BF16)
        bim = _block_diag_halves(bbi).astype(BF16)
        cre = _block_diag_halves(jnp.transpose(ssm_c_re[i], (0, 2, 1))).astype(BF16)
        cim = _block_diag_halves(jnp.transpose(ssm_c_im[i], (0, 2, 1))).astype(BF16)
        s_out = _ssm(u, bre, bim, cre, cim, lbr.reshape(1, -1), lbi.reshape(1, -1),
                     ssm_d[i].reshape(1, -1), w_glu[i].astype(BF16), b_glu[i].reshape(1, -1),
                     batch=bsz, steps=SSM_STEPS, lane_chunk=SSM_LANE_CHUNK)

        g1, b1 = ln1_g[i].reshape(1, d), ln1_b[i].reshape(1, d)
        g2, b2 = ln2_g[i].reshape(1, d), ln2_b[i].reshape(1, d)
        wpg = ple_w_gate[i].astype(BF16)
        wpp = ple_w_proj[i].astype(BF16)
        merge_w = (w_gates, w_branch_attn[i].astype(BF16), w_branch_ssm[i].astype(BF16), w_out[i].astype(BF16))
        if i % 2 == 0:
            (x1,) = _merge(xt, a_out, s_out, *merge_w, g1, b1, tm=TM_MERGE)
            xt = _ffn(x1, pt[i], ffn_gu, ffn_dn, i // 2, wpg, wpp, g2, b2, tm=TM_FFN)
        else:
            rt = jnp.pad(moe_router[i // 2], ((0, 0), (0, LANES - N_EXPERTS)))
            rhi = rt.astype(BF16)
            rlo = (rt - rhi.astype(F32)).astype(BF16)
            x1, route = _merge(xt, a_out, s_out, *merge_w, g1, b1, (rhi, rlo), tm=TM_MERGE)
            te, n_used, src, pos = _routing_tables(route, tm=TM_MOE, n_experts=N_EXPERTS)
            y_sorted = _moe_experts(x1, moe_gu, moe_dn, i // 2, te, n_used, src, tm=TM_MOE, fw=FW_MOE)
            xt = _moe_out(x1, route, pt[i], y_sorted, pos, wpg, wpp, g2, b2, tm=TM_FFN)

    return xt.reshape(bsz, seq, d)
```

```python
import functools
import math

import jax
import jax.numpy as jnp
import numpy as np
from jax import lax
from jax.experimental import pallas as pl
from jax.experimental.pallas import tpu as pltpu

F32 = jnp.float32
BF16 = jnp.bfloat16

D_MODEL = 1024
BATCH = 16
SEQ = 2048
DEPTH = 4
HEAD_DIM = 64
N_Q_HEADS = 8
N_KV_HEADS = 2
ATTN_WIDTH = N_Q_HEADS * HEAD_DIM
KV_WIDTH = N_KV_HEADS * HEAD_DIM
ATTN_BLOCK = 128
N_BUCKETS = 32
MAX_DISTANCE = 128
SSM_WIDTH = 512
SSM_GROUP_CH = 16
SSM_GROUPS = SSM_WIDTH // SSM_GROUP_CH
SSM_STATE = 64
D_FF = 2816
N_EXPERTS = 8
D_FF_EXPERT = 3584
PLE_DIM = 256
DEEPNORM_ALPHA = (2 * DEPTH) ** 0.25
LN_EPS = 1e-5
NEG_INF = -1e30

LANES = 128
VMEM_LIMIT_BYTES = 56 * 1024 * 1024

TM_PROJ = 1024
TM_MERGE = 512
TM_FFN = 512
TM_MOE = 512
FW_MOE = 512
SSM_STEPS = 32
SSM_PERM_STEPS = 16
SSM_LANE_CHUNK = 512
FFN_CHUNKS = ((0, 768), (768, 768), (1536, 768), (2304, 512))


def _cparams(sem):
    return pltpu.CompilerParams(dimension_semantics=sem, vmem_limit_bytes=VMEM_LIMIT_BYTES)


def _layer_norm(h, g, b):
    mu = jnp.mean(h, axis=-1, keepdims=True)
    c = h - mu
    var = jnp.mean(c * c, axis=-1, keepdims=True)
    return c * lax.rsqrt(var + LN_EPS) * g + b


def _sigmoid(v):
    return 1.0 / (1.0 + jnp.exp(-v))


def _gelu_tanh(v):
    return 0.5 * v * (1.0 + jnp.tanh(math.sqrt(2.0 / math.pi) * (v + 0.044715 * (v * v * v))))


def _inproj_body(x_ref, w_ref, q_ref, kv_ref, u_ref):
    xb = x_ref[...].astype(BF16)
    z = jnp.dot(xb, w_ref[...], preferred_element_type=F32)
    aw = q_ref.shape[1]
    kw = kv_ref.shape[1]
    q_ref[...] = (z[:, :aw] * (HEAD_DIM ** -0.5)).astype(BF16)
    kv_ref[...] = z[:, aw:aw + kw].astype(BF16)
    u_ref[...] = z[:, aw + kw:].astype(BF16)


def _inproj(x, w, *, tm):
    t, d = x.shape
    n = w.shape[1]
    aw, kw = ATTN_WIDTH, 2 * (N_KV_HEADS * 2 * LANES)
    uw = n - aw - kw
    return pl.pallas_call(
        _inproj_body,
        grid=(t // tm,),
        in_specs=[pl.BlockSpec((tm, d), lambda i: (i, 0)),
                  pl.BlockSpec((d, n), lambda i: (0, 0))],
        out_specs=[pl.BlockSpec((tm, aw), lambda i: (i, 0)),
                   pl.BlockSpec((tm, kw), lambda i: (i, 0)),
                   pl.BlockSpec((tm, uw), lambda i: (i, 0))],
        out_shape=[jax.ShapeDtypeStruct((t, aw), BF16),
                   jax.ShapeDtypeStruct((t, kw), BF16),
                   jax.ShapeDtypeStruct((t, uw), BF16)],
        compiler_params=_cparams(("parallel",)),
        name="inproj",
    )(x, w)


def _attn_body(sink_ref, q_ref, kvc_ref, kvp_ref, bias_ref, o_ref):
    n = pl.program_id(1)
    blk = q_ref.shape[0]
    col = lax.broadcasted_iota(jnp.int32, (1, 2 * blk), 1)
    first = jnp.where((n == 0) & (col < blk), NEG_INF, 0.0).astype(F32)
    v0 = N_KV_HEADS * 2 * LANES
    group = N_Q_HEADS // N_KV_HEADS
    for tile in range(N_Q_HEADS // 2):
        qt = q_ref[:, tile * LANES:(tile + 1) * LANES]
        acc = None
        for par in range(2):
            h = 2 * tile + par
            kt = 2 * (h // group) + par
            ks = slice(kt * LANES, (kt + 1) * LANES)
            vs = slice(v0 + kt * LANES, v0 + (kt + 1) * LANES)
            k = jnp.concatenate([kvp_ref[:, ks], kvc_ref[:, ks]], axis=0)
            v = jnp.concatenate([kvp_ref[:, vs], kvc_ref[:, vs]], axis=0)
            s = lax.dot_general(qt, k, (((1,), (1,)), ((), ())), preferred_element_type=F32)
            s = s + bias_ref[h] + first
            sink = sink_ref[h]
            m = jnp.maximum(jnp.max(s, axis=-1, keepdims=True), sink)
            e = jnp.exp(s - m)
            denom = jnp.sum(e, axis=-1, keepdims=True) + jnp.exp(sink - m)
            o = jnp.dot(e.astype(BF16), v, preferred_element_type=F32) / denom
            acc = o if acc is None else acc + o
        o_ref[:, tile * LANES:(tile + 1) * LANES] = acc.astype(BF16)


def _attention(q, kv, sinks, bias, *, seq, batch):
    t = q.shape[0]
    blk = ATTN_BLOCK
    nb = seq // blk
    aw, kw = q.shape[1], kv.shape[1]
    return pl.pallas_call(
        _attn_body,
        grid=(batch, nb),
        in_specs=[pl.BlockSpec(memory_space=pltpu.SMEM),
                  pl.BlockSpec((blk, aw), lambda b, n: (b * nb + n, 0)),
                  pl.BlockSpec((blk, kw), lambda b, n: (b * nb + n, 0)),
                  pl.BlockSpec((blk, kw), lambda b, n: (b * nb + jnp.maximum(n - 1, 0), 0)),
                  pl.BlockSpec((N_Q_HEADS, blk, 2 * blk), lambda b, n: (0, 0, 0))],
        out_specs=pl.BlockSpec((blk, aw), lambda b, n: (b * nb + n, 0)),
        out_shape=jax.ShapeDtypeStruct((t, aw), BF16),
        compiler_params=_cparams(("parallel", "parallel")),
        name="swa_attention",
    )(sinks, q, kv, kv, bias)


def _per_head_kv_columns(w):
    d = w.shape[0]
    wh = w.reshape(d, N_KV_HEADS, 1, 1, HEAD_DIM)
    half = jnp.arange(2)[None, None, None, :, None]
    par = jnp.arange(2)[None, None, :, None, None]
    tiles = jnp.where(half == par, wh, 0.0)
    return tiles.reshape(d, N_KV_HEADS * 2 * LANES)


def _attn_bias_table(rel_bias):
    blk = ATTN_BLOCK
    d = np.arange(blk, dtype=np.int32)
    max_exact = N_BUCKETS // 2
    d_f = np.maximum(d, 1).astype(np.float32)
    large = max_exact + (np.log(d_f / np.float32(max_exact)) / np.float32(math.log(MAX_DISTANCE / max_exact))
                         * np.float32(N_BUCKETS - max_exact)).astype(np.int32)
    large = np.minimum(large, N_BUCKETS - 1)
    bucket = np.where(d < max_exact, d, large)
    vals = jnp.transpose(rel_bias.astype(F32)[bucket], (1, 0))
    h = vals.shape[0]
    neg = jnp.full((h, blk), NEG_INF, F32)
    strip = jnp.concatenate([neg, vals[:, ::-1], neg], axis=1)
    rows = [strip[:, blk - 1 - i:blk - 1 - i + 2 * blk] for i in range(blk)]
    return jnp.stack(rows, axis=1)


def _ssm_param_body(lre_ref, lim_ref, ldt_ref, bre_ref, bim_ref, lbr_ref, lbi_ref, bbr_ref, bbi_ref):
    lre, lim = lre_ref[...], lim_ref[...]
    dt = jnp.exp(ldt_ref[...])
    mag = jnp.exp(lre * dt)
    lbr = mag * jnp.cos(lim * dt)
    lbi = mag * jnp.sin(lim * dt)
    lbr_ref[...] = lbr
    lbi_ref[...] = lbi
    nr, ni = lbr - 1.0, lbi
    inv = 1.0 / (lre * lre + lim * lim)
    cr = (nr * lre + ni * lim) * inv
    ci = (ni * lre - nr * lim) * inv
    br, bi = bre_ref[...], bim_ref[...]
    bbr_ref[...] = cr * br - ci * bi
    bbi_ref[...] = cr * bi + ci * br


def _ssm_params(lam_re, lam_im, log_dt, b_re, b_im):
    g, p = lam_re.shape
    h = b_re.shape[2]
    ldt = jnp.broadcast_to(log_dt[:, None, None], (g, 1, p))
    brt = jnp.transpose(b_re, (0, 2, 1))
    bit = jnp.transpose(b_im, (0, 2, 1))
    sd = jax.ShapeDtypeStruct
    return pl.pallas_call(
        _ssm_param_body,
        out_shape=[sd((g, 1, p), F32), sd((g, 1, p), F32), sd((g, h, p), F32), sd((g, h, p), F32)],
        name="ssm_params",
    )(lam_re.reshape(g, 1, p), lam_im.reshape(g, 1, p), ldt, brt, bit)


def _block_diag_halves(w):
    g, a, b = w.shape
    hg = g // 2
    wh = w.reshape(2, hg, a, b)
    eye = jnp.eye(hg, dtype=bool)[None, :, None, :, None]
    full = jnp.where(eye, wh[:, :, :, None, :], 0.0)
    return full.reshape(2, hg * a, hg * b)


def _ssm_body(u_ref, perm_ref, permt_ref, bre_ref, bim_ref, cre_ref, cim_ref, lr_ref, li_ref, d_ref,
              wglu_ref, bglu_ref, o_ref, xr, xi, sr, si, *, steps, batch, lane_chunk):
    @pl.when(pl.program_id(0) == 0)
    def _():
        sr[...] = jnp.zeros_like(sr)
        si[...] = jnp.zeros_like(si)

    sub = SSM_PERM_STEPS
    parts = []
    for g in range(steps // sub):
        bm = jnp.concatenate([u_ref[b, g * sub:(g + 1) * sub, :] for b in range(batch)], axis=0)
        parts.append(jnp.dot(perm_ref[...], bm, preferred_element_type=F32).astype(BF16))
    u = jnp.concatenate(parts, axis=0)
    hw = u.shape[1] // 2
    hs = xr.shape[1] // 2
    for hf in range(2):
        uh = u[:, hf * hw:(hf + 1) * hw]
        xr[:, hf * hs:(hf + 1) * hs] = jnp.dot(uh, bre_ref[hf], preferred_element_type=F32)
        xi[:, hf * hs:(hf + 1) * hs] = jnp.dot(uh, bim_ref[hf], preferred_element_type=F32)

    for c in range(xr.shape[1] // lane_chunk):
        cs = slice(c * lane_chunk, (c + 1) * lane_chunk)
        ar = lr_ref[:, cs]
        ai = li_ref[:, cs]

        def step(t, carry, cs=cs, ar=ar, ai=ai):
            pr, pi = carry
            rows = pl.ds(pl.multiple_of(t * batch, batch), batch)
            nr = ar * pr - ai * pi + xr[rows, cs]
            ni = ar * pi + ai * pr + xi[rows, cs]
            xr[rows, cs] = nr
            xi[rows, cs] = ni
            return nr, ni

        pr, pi = lax.fori_loop(0, steps, step, (sr[:, cs], si[:, cs]), unroll=4)
        sr[:, cs] = pr
        si[:, cs] = pi

    ys = []
    for hf in range(2):
        xrb = xr[:, hf * hs:(hf + 1) * hs].astype(BF16)
        xib = xi[:, hf * hs:(hf + 1) * hs].astype(BF16)
        ys.append(jnp.dot(xrb, cre_ref[hf], preferred_element_type=F32)
                  - jnp.dot(xib, cim_ref[hf], preferred_element_type=F32))
    y = jnp.concatenate(ys, axis=1) + d_ref[...] * u.astype(F32)
    y = _gelu_tanh(y)
    z = jnp.dot(y.astype(BF16), wglu_ref[...], preferred_element_type=F32) + bglu_ref[...]
    out = (y * _sigmoid(z)).astype(BF16)
    rows_g = sub * batch
    for g in range(steps // sub):
        bm = jnp.dot(permt_ref[...], out[g * rows_g:(g + 1) * rows_g], preferred_element_type=F32).astype(BF16)
        for b in range(batch):
            o_ref[b, g * sub:(g + 1) * sub, :] = bm[b * sub:(b + 1) * sub]


def _time_major_permutation(batch, sub):
    n = batch * sub
    r = np.arange(n)
    p = np.zeros((n, n), np.float32)
    p[r, (r % batch) * sub + r // batch] = 1.0
    return p


def _ssm(u, bre, bim, cre, cim, lbr, lbi, d, wglu, bglu, *, batch, steps, lane_chunk):
    t, w = u.shape
    seq = t // batch
    ns = lbr.shape[1]
    rows = steps * batch
    perm = _time_major_permutation(batch, SSM_PERM_STEPS)
    const2 = lambda i: (0, 0)
    const3 = lambda i: (0, 0, 0)
    out = pl.pallas_call(
        functools.partial(_ssm_body, steps=steps, batch=batch, lane_chunk=lane_chunk),
        grid=(seq // steps,),
        in_specs=[pl.BlockSpec((batch, steps, w), lambda i: (0, i, 0)),
                  pl.BlockSpec(perm.shape, const2), pl.BlockSpec(perm.shape, const2),
                  pl.BlockSpec(bre.shape, const3), pl.BlockSpec(bim.shape, const3),
                  pl.BlockSpec(cre.shape, const3), pl.BlockSpec(cim.shape, const3),
                  pl.BlockSpec((1, ns), const2), pl.BlockSpec((1, ns), const2),
                  pl.BlockSpec((1, w), const2), pl.BlockSpec((w, w), const2), pl.BlockSpec((1, w), const2)],
        out_specs=pl.BlockSpec((batch, steps, w), lambda i: (0, i, 0)),
        out_shape=jax.ShapeDtypeStruct((batch, seq, w), BF16),
        scratch_shapes=[pltpu.VMEM((rows, ns), F32), pltpu.VMEM((rows, ns), F32),
                        pltpu.VMEM((batch, ns), F32), pltpu.VMEM((batch, ns), F32)],
        compiler_params=_cparams(("arbitrary",)),
        name="s5_scan",
    )(u.reshape(batch, seq, w), jnp.asarray(perm, BF16), jnp.asarray(perm.T, BF16),
      bre, bim, cre, cim, lbr, lbi, d, wglu, bglu)
    return out.reshape(t, w)


def _split_bf16(v):
    hi = v.astype(BF16)
    lo = (v - hi.astype(F32)).astype(BF16)
    return hi, lo


def _merge_body(x_ref, a_ref, s_ref, wg_ref, wba_ref, wbs_ref, wo_ref, g_ref, b_ref, *rest, with_router):
    if with_router:
        rhi_ref, rlo_ref, x1_ref, route_ref = rest
    else:
        (x1_ref,) = rest
    x = x_ref[...]
    d = x.shape[1]
    gates = jnp.dot(x.astype(BF16), wg_ref[...], preferred_element_type=F32)
    pa = jnp.dot(a_ref[...], wba_ref[...], preferred_element_type=F32)
    ps = jnp.dot(s_ref[...], wbs_ref[...], preferred_element_type=F32)
    merged = _sigmoid(gates[:, :d]) * pa + _sigmoid(gates[:, d:]) * ps
    y = jnp.dot(merged.astype(BF16), wo_ref[...], preferred_element_type=F32)
    x1 = _layer_norm(DEEPNORM_ALPHA * x + y, g_ref[...], b_ref[...])
    x1_ref[...] = x1
    if with_router:
        hi, lo = _split_bf16(x1)
        rhi = rhi_ref[...]
        logits = (jnp.dot(hi, rhi, preferred_element_type=F32)
                  + jnp.dot(lo, rhi, preferred_element_type=F32)
                  + jnp.dot(hi, rlo_ref[...], preferred_element_type=F32))
        lane = lax.broadcasted_iota(jnp.int32, logits.shape, 1).astype(F32)
        big = float(LANES)
        l1 = jnp.where(lane < N_EXPERTS, logits, -jnp.inf)
        m1 = jnp.max(l1, axis=-1, keepdims=True)
        i1 = jnp.min(jnp.where(l1 == m1, lane, big), axis=-1, keepdims=True)
        l2 = jnp.where(lane == i1, -jnp.inf, l1)
        m2 = jnp.max(l2, axis=-1, keepdims=True)
        i2 = jnp.min(jnp.where(l2 == m2, lane, big), axis=-1, keepdims=True)
        ed = jnp.exp(m2 - m1)
        g1 = 1.0 / (1.0 + ed)
        g2 = ed / (1.0 + ed)
        route = jnp.where(lane == 0.0, i1,
                          jnp.where(lane == 1.0, i2,
                                    jnp.where(lane == 2.0, g1, jnp.where(lane == 3.0, g2, 0.0))))
        route_ref[...] = route


def _merge(x, a, s, wg, wba, wbs, wo, g, b, router=None, *, tm):
    t, d = x.shape
    const = lambda i: (0, 0)
    row = lambda i: (i, 0)
    in_specs = [pl.BlockSpec((tm, d), row), pl.BlockSpec((tm, a.shape[1]), row), pl.BlockSpec((tm, s.shape[1]), row),
                pl.BlockSpec(wg.shape, const), pl.BlockSpec(wba.shape, const), pl.BlockSpec(wbs.shape, const),
                pl.BlockSpec(wo.shape, const), pl.BlockSpec((1, d), const), pl.BlockSpec((1, d), const)]
    args = [x, a, s, wg, wba, wbs, wo, g, b]
    out_specs = [pl.BlockSpec((tm, d), row)]
    out_shape = [jax.ShapeDtypeStruct((t, d), F32)]
    if router is not None:
        rhi, rlo = router
        in_specs += [pl.BlockSpec(rhi.shape, const), pl.BlockSpec(rlo.shape, const)]
        args += [rhi, rlo]
        out_specs.append(pl.BlockSpec((tm, LANES), row))
        out_shape.append(jax.ShapeDtypeStruct((t, LANES), F32))
    return pl.pallas_call(
        functools.partial(_merge_body, with_router=router is not None),
        grid=(t // tm,),
        in_specs=in_specs, out_specs=out_specs, out_shape=out_shape,
        compiler_params=_cparams(("parallel",)),
        name="merge_ln1",
    )(*args)


def _ple(xb, p_ref, wpg_ref, wpp_ref):
    gate = _sigmoid(jnp.dot(xb, wpg_ref[...], preferred_element_type=F32))
    return gate * jnp.dot(p_ref[...].astype(BF16), wpp_ref[...], preferred_element_type=F32)


def _ffn_body(x_ref, p_ref, wg_ref, wu_ref, wd_ref, wpg_ref, wpp_ref, g_ref, b_ref, o_ref, acc):
    x1 = x_ref[...]
    xb = x1.astype(BF16)
    for c, (c0, cw) in enumerate(FFN_CHUNKS):
        gt = jnp.dot(xb, wg_ref[:, c0:c0 + cw], preferred_element_type=F32)
        up = jnp.dot(xb, wu_ref[:, c0:c0 + cw], preferred_element_type=F32)
        act = (gt * _sigmoid(gt) * up).astype(BF16)
        contrib = jnp.dot(act, wd_ref[c0:c0 + cw, :], preferred_element_type=F32)
        if c == 0:
            acc[...] = contrib
        else:
            acc[...] += contrib
    h = DEEPNORM_ALPHA * x1 + acc[...] + _ple(xb, p_ref, wpg_ref, wpp_ref)
    o_ref[...] = _layer_norm(h, g_ref[...], b_ref[...])


def _resident(shape, index_map):
    return pl.BlockSpec(shape, index_map, pipeline_mode=pl.Buffered(1))


def _ffn(x1, p, w_gu, w_down, li, wpg, wpp, g, b, *, tm):
    t, d = x1.shape
    ff = w_down.shape[1]
    const = lambda i: (0, 0)
    row = lambda i: (i, 0)
    return pl.pallas_call(
        _ffn_body,
        grid=(t // tm,),
        in_specs=[pl.BlockSpec((tm, d), row), pl.BlockSpec((tm, p.shape[1]), row),
                  _resident((None, d, ff), lambda i: (li, 0, 0)),
                  _resident((None, d, ff), lambda i: (li, 0, 1)),
                  _resident((None, ff, d), lambda i: (li, 0, 0)),
                  _resident(wpg.shape, const), _resident(wpp.shape, const),
                  pl.BlockSpec((1, d), const), pl.BlockSpec((1, d), const)],
        out_specs=pl.BlockSpec((tm, d), row),
        out_shape=jax.ShapeDtypeStruct((t, d), F32),
        scratch_shapes=[pltpu.VMEM((tm, d), F32)],
        compiler_params=_cparams(("parallel",)),
        name="ffn_ln2",
    )(x1, p, w_gu, w_gu, w_down, wpg, wpp, g, b)


def _moe_body(te_ref, nu_ref, src_ref, dst_ref, x_hbm, wg_ref, wu_ref, wd_ref, y_hbm,
              xg, xb, acc, ybuf, gsem, ssem, *, tm, nf, spare_row):
    i = pl.program_id(0)
    j = pl.program_id(1)
    nu = nu_ref[0]
    slot = i % 2
    other = 1 - slot
    per_step = tm // nf
    leftover = tm - per_step * nf

    def gather(tile, r, buf):
        row = src_ref[tile * tm + r]
        pltpu.make_async_copy(x_hbm.at[pl.ds(row, 1), :], xg.at[buf, pl.ds(r, 1), :], gsem.at[buf]).start()

    def scatter(row, r, buf):
        pltpu.make_async_copy(ybuf.at[buf, pl.ds(r, 1), :], y_hbm.at[pl.ds(row, 1), :], ssem.at[buf]).start()

    def prev_tile_dst(r):
        return jnp.where(i > 0, dst_ref[jnp.maximum(i - 1, 0) * tm + r], spare_row + r)

    @pl.when((i == 0) & (j == 0))
    def _():
        ybuf[...] = jnp.zeros_like(ybuf)

        def one(r, c):
            gather(0, r, 0)
            return c
        lax.fori_loop(0, tm, one, 0)

    @pl.when((j == 0) & (i <= nu))
    def _():
        pltpu.make_async_copy(x_hbm.at[pl.ds(0, tm), :], xg.at[slot], gsem.at[slot]).wait()
        xb[...] = xg[slot].astype(BF16)

    if leftover:
        @pl.when((j == 0) & (i < nu))
        def _():
            for k in range(leftover):
                r = per_step * nf + k
                gather(i + 1, r, other)
                scatter(prev_tile_dst(r), r, other)

    @pl.when(i < nu)
    def _():
        x = xb[...]
        gt = jnp.dot(x, wg_ref[...], preferred_element_type=F32)
        up = jnp.dot(x, wu_ref[...], preferred_element_type=F32)
        act = (gt * _sigmoid(gt) * up).astype(BF16)
        part = jnp.dot(act, wd_ref[...], preferred_element_type=F32)
        acc[...] = jnp.where(j == 0, part, acc[...] + part)
        for k in range(per_step):
            r = j * per_step + k
            gather(i + 1, r, other)
            scatter(prev_tile_dst(r), r, other)

    @pl.when((i == nu) & (j == 0))
    def _():
        def one(r, c):
            scatter(dst_ref[(i - 1) * tm + r], r, other)
            return c
        lax.fori_loop(0, tm, one, 0)

    @pl.when((j == nf - 1) & (i <= nu))
    def _():
        pltpu.make_async_copy(ybuf.at[other], y_hbm.at[pl.ds(0, tm), :], ssem.at[other]).wait()

    @pl.when((j == nf - 1) & (i < nu))
    def _():
        ybuf[slot] = acc[...]


def _moe_experts(x1, w_gu, w_down, li, tile_expert, n_used, src, dst, *, tm, fw, out_rows):
    t, d = x1.shape
    ffe = w_down.shape[2]
    nf = ffe // fw
    nt = src.shape[0] // tm

    def jj(i, j, nu):
        return jnp.where(i < nu[0], j, nf - 1)

    return pl.pallas_call(
        functools.partial(_moe_body, tm=tm, nf=nf, spare_row=out_rows - tm),
        grid_spec=pltpu.PrefetchScalarGridSpec(
            num_scalar_prefetch=4,
            grid=(nt, nf),
            in_specs=[pl.BlockSpec(memory_space=pl.ANY),
                      pl.BlockSpec((None, None, d, fw), lambda i, j, te, nu, s, ds: (li, te[i], 0, jj(i, j, nu))),
                      pl.BlockSpec((None, None, d, fw),
                                   lambda i, j, te, nu, s, ds: (li, te[i], 0, nf + jj(i, j, nu))),
                      pl.BlockSpec((None, None, fw, d), lambda i, j, te, nu, s, ds: (li, te[i], jj(i, j, nu), 0))],
            out_specs=pl.BlockSpec(memory_space=pl.ANY),
            scratch_shapes=[pltpu.VMEM((2, tm, d), F32), pltpu.VMEM((tm, d), BF16), pltpu.VMEM((tm, d), F32),
                            pltpu.VMEM((2, tm, d), F32),
                            pltpu.SemaphoreType.DMA((2,)), pltpu.SemaphoreType.DMA((2,))]),
        out_shape=jax.ShapeDtypeStruct((out_rows, d), F32),
        compiler_params=_cparams(("arbitrary", "arbitrary")),
        name="moe_experts",
    )(tile_expert, n_used, src, dst, x1, w_gu, w_gu, w_down)


def _moe_out_body(x_ref, route_ref, y0_ref, y1_ref, p_ref, wpg_ref, wpp_ref, g_ref, b_ref, o_ref):
    x1 = x_ref[...]
    route = route_ref[...]
    ffn = route[:, 2:3] * y0_ref[...] + route[:, 3:4] * y1_ref[...]
    h = DEEPNORM_ALPHA * x1 + ffn + _ple(x1.astype(BF16), p_ref, wpg_ref, wpp_ref)
    o_ref[...] = _layer_norm(h, g_ref[...], b_ref[...])


def _moe_out(x1, route, p, y, wpg, wpp, g, b, *, tm):
    t, d = x1.shape
    nt = t // tm
    const = lambda i: (0, 0)
    row = lambda i: (i, 0)
    return pl.pallas_call(
        _moe_out_body,
        grid=(nt,),
        in_specs=[pl.BlockSpec((tm, d), row), pl.BlockSpec((tm, LANES), row),
                  pl.BlockSpec((tm, d), row), pl.BlockSpec((tm, d), lambda i: (nt + i, 0)),
                  pl.BlockSpec((tm, p.shape[1]), row),
                  pl.BlockSpec(wpg.shape, const), pl.BlockSpec(wpp.shape, const),
                  pl.BlockSpec((1, d), const), pl.BlockSpec((1, d), const)],
        out_specs=pl.BlockSpec((tm, d), row),
        out_shape=jax.ShapeDtypeStruct((t, d), F32),
        compiler_params=_cparams(("parallel",)),
        name="moe_combine_ln2",
    )(x1, route, y, y, p, wpg, wpp, g, b)


def _routing_tables(route, *, tm, n_experts):
    t = route.shape[0]
    eids = route[:, :2].astype(jnp.int32).reshape(-1)
    onehot = (eids[:, None] == jnp.arange(n_experts, dtype=jnp.int32)[None, :]).astype(jnp.int32)
    csum = jnp.cumsum(onehot, axis=0)
    rank = jnp.sum((csum - onehot) * onehot, axis=1)
    cnt = csum[-1]
    ntile = (cnt + tm - 1) // tm
    tile_end = jnp.cumsum(ntile)
    row_off = (tile_end - ntile) * tm
    pos = jnp.sum(onehot * row_off[None, :], axis=1) + rank
    r = 2 * t + (n_experts + 1) * tm
    slot = jnp.arange(2 * t, dtype=jnp.int32)
    dst = (2 * t + jnp.arange(r, dtype=jnp.int32) % tm).at[pos].set((slot % 2) * t + slot // 2)
    src = jnp.where(dst < 2 * t, dst % t, 0)
    n_used = tile_end[-1:]
    tiles = jnp.arange(r // tm, dtype=jnp.int32)
    te = jnp.sum((tiles[:, None] >= tile_end[None, :]).astype(jnp.int32), axis=1)
    last = jnp.sum((n_used - 1 >= tile_end).astype(jnp.int32))
    te = jnp.where(tiles < n_used, te, last).astype(jnp.int32)
    out_rows = 2 * t + tm
    return te, n_used.astype(jnp.int32), src.astype(jnp.int32), dst.astype(jnp.int32), out_rows


def kernel(x, p, rel_bias, w_in, attn_sinks, ssm_lambda_re, ssm_lambda_im, ssm_log_dt, ssm_b_re, ssm_b_im,
           ssm_c_re, ssm_c_im, ssm_d, w_glu, b_glu, w_branch_attn, w_branch_ssm, w_out, ln1_g, ln1_b,
           ffn_w_gate_up, ffn_w_down, moe_router, moe_w_gate_up, moe_w_down, ple_w_proj, ple_w_gate,
           ln2_g, ln2_b):
    bsz, seq, d = x.shape
    t = bsz * seq
    depth = w_in.shape[0]

    xt = x.reshape(t, d)
    pt = p.reshape(depth, t, p.shape[-1])

    bias = _attn_bias_table(rel_bias)
    ffn_gu = ffn_w_gate_up.astype(BF16)
    ffn_dn = ffn_w_down.astype(BF16)
    moe_gu = moe_w_gate_up.astype(BF16)
    moe_dn = moe_w_down.astype(BF16)

    aw, kw = ATTN_WIDTH, KV_WIDTH
    for i in range(depth):
        wi = w_in[i]
        u0 = aw + 2 * kw
        w_proj = jnp.concatenate([wi[:, :aw], _per_head_kv_columns(wi[:, aw:aw + kw]),
                                  _per_head_kv_columns(wi[:, aw + kw:u0]), wi[:, u0:u0 + SSM_WIDTH]],
                                 axis=1).astype(BF16)
        w_gates = wi[:, u0 + SSM_WIDTH:].astype(BF16)

        q, kv, u = _inproj(xt, w_proj, tm=TM_PROJ)
        a_out = _attention(q, kv, attn_sinks[i], bias, seq=seq, batch=bsz)

        lbr, lbi, bbr, bbi = _ssm_params(ssm_lambda_re[i], ssm_lambda_im[i], ssm_log_dt[i],
                                         ssm_b_re[i], ssm_b_im[i])
        bre = _block_diag_halves(bbr).astype(BF16)
        bim = _block_diag_halves(bbi).astype(BF16)
        cre = _block_diag_halves(jnp.transpose(ssm_c_re[i], (0, 2, 1))).astype(BF16)
        cim = _block_diag_halves(jnp.transpose(ssm_c_im[i], (0, 2, 1))).astype(BF16)
        s_out = _ssm(u, bre, bim, cre, cim, lbr.reshape(1, -1), lbi.reshape(1, -1),
                     ssm_d[i].reshape(1, -1), w_glu[i].astype(BF16), b_glu[i].reshape(1, -1),
                     batch=bsz, steps=SSM_STEPS, lane_chunk=SSM_LANE_CHUNK)

        g1, b1 = ln1_g[i].reshape(1, d), ln1_b[i].reshape(1, d)
        g2, b2 = ln2_g[i].reshape(1, d), ln2_b[i].reshape(1, d)
        wpg = ple_w_gate[i].astype(BF16)
        wpp = ple_w_proj[i].astype(BF16)
        merge_w = (w_gates, w_branch_attn[i].astype(BF16), w_branch_ssm[i].astype(BF16), w_out[i].astype(BF16))
        if i % 2 == 0:
            (x1,) = _merge(xt, a_out, s_out, *merge_w, g1, b1, tm=TM_MERGE)
            xt = _ffn(x1, pt[i], ffn_gu, ffn_dn, i // 2, wpg, wpp, g2, b2, tm=TM_FFN)
        else:
            rt = jnp.pad(moe_router[i // 2], ((0, 0), (0, LANES - N_EXPERTS)))
            rhi = rt.astype(BF16)
            rlo = (rt - rhi.astype(F32)).astype(BF16)
            x1, route = _merge(xt, a_out, s_out, *merge_w, g1, b1, (rhi, rlo), tm=TM_MERGE)
            te, n_used, src, dst, out_rows = _routing_tables(route, tm=TM_MOE, n_experts=N_EXPERTS)
            y = _moe_experts(x1, moe_gu, moe_dn, i // 2, te, n_used, src, dst, tm=TM_MOE, fw=FW_MOE,
                             out_rows=out_rows)
            xt = _moe_out(x1, route, pt[i], y, wpg, wpp, g2, b2, tm=TM_FFN)

    return xt.reshape(bsz, seq, d)
```

```python
import functools
import math

import jax
import jax.numpy as jnp
import numpy as np
from jax import lax
from jax.experimental import pallas as pl
from jax.experimental.pallas import tpu as pltpu

F32 = jnp.float32
BF16 = jnp.bfloat16

D_MODEL = 1024
BATCH = 16
SEQ = 2048
DEPTH = 4
HEAD_DIM = 64
N_Q_HEADS = 8
N_KV_HEADS = 2
ATTN_WIDTH = N_Q_HEADS * HEAD_DIM
KV_WIDTH = N_KV_HEADS * HEAD_DIM
ATTN_BLOCK = 128
N_BUCKETS = 32
MAX_DISTANCE = 128
SSM_WIDTH = 512
SSM_GROUP_CH = 16
SSM_GROUPS = SSM_WIDTH // SSM_GROUP_CH
SSM_STATE = 64
D_FF = 2816
N_EXPERTS = 8
D_FF_EXPERT = 3584
PLE_DIM = 256
DEEPNORM_ALPHA = (2 * DEPTH) ** 0.25
LN_EPS = 1e-5
NEG_INF = -1e30

LANES = 128
VMEM_LIMIT_BYTES = 56 * 1024 * 1024

TM_PROJ = 1024
TM_MERGE = 512
TM_FFN = 512
TM_MOE = 512
FW_MOE = 512
SSM_STEPS = 32
SSM_PERM_STEPS = 16
SSM_LANE_CHUNK = 512
FFN_CHUNKS = ((0, 768), (768, 768), (1536, 768), (2304, 512))


def _cparams(sem):
    return pltpu.CompilerParams(dimension_semantics=sem, vmem_limit_bytes=VMEM_LIMIT_BYTES)


def _layer_norm(h, g, b):
    mu = jnp.mean(h, axis=-1, keepdims=True)
    c = h - mu
    var = jnp.mean(c * c, axis=-1, keepdims=True)
    return c * lax.rsqrt(var + LN_EPS) * g + b


def _sigmoid(v):
    return 1.0 / (1.0 + jnp.exp(-v))


def _gelu_tanh(v):
    return 0.5 * v * (1.0 + jnp.tanh(math.sqrt(2.0 / math.pi) * (v + 0.044715 * (v * v * v))))


def _inproj_body(x_ref, w_ref, q_ref, kv_ref, u_ref):
    xb = x_ref[...].astype(BF16)
    z = jnp.dot(xb, w_ref[...], preferred_element_type=F32)
    aw = q_ref.shape[1]
    kw = kv_ref.shape[1]
    q_ref[...] = (z[:, :aw] * (HEAD_DIM ** -0.5)).astype(BF16)
    kv_ref[...] = z[:, aw:aw + kw].astype(BF16)
    u_ref[...] = z[:, aw + kw:].astype(BF16)


def _inproj(x, w, *, tm):
    t, d = x.shape
    n = w.shape[1]
    aw, kw = ATTN_WIDTH, 2 * (N_KV_HEADS * 2 * LANES)
    uw = n - aw - kw
    return pl.pallas_call(
        _inproj_body,
        grid=(t // tm,),
        in_specs=[pl.BlockSpec((tm, d), lambda i: (i, 0)),
                  pl.BlockSpec((d, n), lambda i: (0, 0))],
        out_specs=[pl.BlockSpec((tm, aw), lambda i: (i, 0)),
                   pl.BlockSpec((tm, kw), lambda i: (i, 0)),
                   pl.BlockSpec((tm, uw), lambda i: (i, 0))],
        out_shape=[jax.ShapeDtypeStruct((t, aw), BF16),
                   jax.ShapeDtypeStruct((t, kw), BF16),
                   jax.ShapeDtypeStruct((t, uw), BF16)],
        compiler_params=_cparams(("parallel",)),
        name="inproj",
    )(x, w)


def _attn_body(sink_ref, q_ref, kvc_ref, kvp_ref, bias_ref, o_ref):
    n = pl.program_id(1)
    blk = q_ref.shape[0]
    col = lax.broadcasted_iota(jnp.int32, (1, 2 * blk), 1)
    first = jnp.where((n == 0) & (col < blk), NEG_INF, 0.0).astype(F32)
    v0 = N_KV_HEADS * 2 * LANES
    group = N_Q_HEADS // N_KV_HEADS
    for tile in range(N_Q_HEADS // 2):
        qt = q_ref[:, tile * LANES:(tile + 1) * LANES]
        acc = None
        for par in range(2):
            h = 2 * tile + par
            kt = 2 * (h // group) + par
            ks = slice(kt * LANES, (kt + 1) * LANES)
            vs = slice(v0 + kt * LANES, v0 + (kt + 1) * LANES)
            k = jnp.concatenate([kvp_ref[:, ks], kvc_ref[:, ks]], axis=0)
            v = jnp.concatenate([kvp_ref[:, vs], kvc_ref[:, vs]], axis=0)
            s = lax.dot_general(qt, k, (((1,), (1,)), ((), ())), preferred_element_type=F32)
            s = s + bias_ref[h] + first
            sink = sink_ref[h]
            m = jnp.maximum(jnp.max(s, axis=-1, keepdims=True), sink)
            e = jnp.exp(s - m)
            denom = jnp.sum(e, axis=-1, keepdims=True) + jnp.exp(sink - m)
            o = jnp.dot(e.astype(BF16), v, preferred_element_type=F32) / denom
            acc = o if acc is None else acc + o
        o_ref[:, tile * LANES:(tile + 1) * LANES] = acc.astype(BF16)


def _attention(q, kv, sinks, bias, *, seq, batch):
    t = q.shape[0]
    blk = ATTN_BLOCK
    nb = seq // blk
    aw, kw = q.shape[1], kv.shape[1]
    return pl.pallas_call(
        _attn_body,
        grid=(batch, nb),
        in_specs=[pl.BlockSpec(memory_space=pltpu.SMEM),
                  pl.BlockSpec((blk, aw), lambda b, n: (b * nb + n, 0)),
                  pl.BlockSpec((blk, kw), lambda b, n: (b * nb + n, 0)),
                  pl.BlockSpec((blk, kw), lambda b, n: (b * nb + jnp.maximum(n - 1, 0), 0)),
                  pl.BlockSpec((N_Q_HEADS, blk, 2 * blk), lambda b, n: (0, 0, 0))],
        out_specs=pl.BlockSpec((blk, aw), lambda b, n: (b * nb + n, 0)),
        out_shape=jax.ShapeDtypeStruct((t, aw), BF16),
        compiler_params=_cparams(("parallel", "parallel")),
        name="swa_attention",
    )(sinks, q, kv, kv, bias)


def _per_head_kv_columns(w):
    d = w.shape[0]
    wh = w.reshape(d, N_KV_HEADS, 1, 1, HEAD_DIM)
    half = jnp.arange(2)[None, None, None, :, None]
    par = jnp.arange(2)[None, None, :, None, None]
    tiles = jnp.where(half == par, wh, 0.0)
    return tiles.reshape(d, N_KV_HEADS * 2 * LANES)


def _attn_bias_table(rel_bias):
    blk = ATTN_BLOCK
    d = np.arange(blk, dtype=np.int32)
    max_exact = N_BUCKETS // 2
    d_f = np.maximum(d, 1).astype(np.float32)
    large = max_exact + (np.log(d_f / np.float32(max_exact)) / np.float32(math.log(MAX_DISTANCE / max_exact))
                         * np.float32(N_BUCKETS - max_exact)).astype(np.int32)
    large = np.minimum(large, N_BUCKETS - 1)
    bucket = np.where(d < max_exact, d, large)
    vals = jnp.transpose(rel_bias.astype(F32)[bucket], (1, 0))
    h = vals.shape[0]
    neg = jnp.full((h, blk), NEG_INF, F32)
    strip = jnp.concatenate([neg, vals[:, ::-1], neg], axis=1)
    rows = [strip[:, blk - 1 - i:blk - 1 - i + 2 * blk] for i in range(blk)]
    return jnp.stack(rows, axis=1)


def _ssm_param_body(lre_ref, lim_ref, ldt_ref, bre_ref, bim_ref, lbr_ref, lbi_ref, bbr_ref, bbi_ref):
    lre, lim = lre_ref[...], lim_ref[...]
    dt = jnp.exp(ldt_ref[...])
    mag = jnp.exp(lre * dt)
    lbr = mag * jnp.cos(lim * dt)
    lbi = mag * jnp.sin(lim * dt)
    lbr_ref[...] = lbr
    lbi_ref[...] = lbi
    nr, ni = lbr - 1.0, lbi
    inv = 1.0 / (lre * lre + lim * lim)
    cr = (nr * lre + ni * lim) * inv
    ci = (ni * lre - nr * lim) * inv
    br, bi = bre_ref[...], bim_ref[...]
    bbr_ref[...] = cr * br - ci * bi
    bbi_ref[...] = cr * bi + ci * br


def _ssm_params(lam_re, lam_im, log_dt, b_re, b_im):
    g, p = lam_re.shape
    h = b_re.shape[2]
    ldt = jnp.broadcast_to(log_dt[:, None, None], (g, 1, p))
    brt = jnp.transpose(b_re, (0, 2, 1))
    bit = jnp.transpose(b_im, (0, 2, 1))
    sd = jax.ShapeDtypeStruct
    return pl.pallas_call(
        _ssm_param_body,
        out_shape=[sd((g, 1, p), F32), sd((g, 1, p), F32), sd((g, h, p), F32), sd((g, h, p), F32)],
        name="ssm_params",
    )(lam_re.reshape(g, 1, p), lam_im.reshape(g, 1, p), ldt, brt, bit)


def _block_diag_halves(w):
    g, a, b = w.shape
    hg = g // 2
    wh = w.reshape(2, hg, a, b)
    eye = jnp.eye(hg, dtype=bool)[None, :, None, :, None]
    full = jnp.where(eye, wh[:, :, :, None, :], 0.0)
    return full.reshape(2, hg * a, hg * b)


def _ssm_body(u_ref, perm_ref, permt_ref, bre_ref, bim_ref, cre_ref, cim_ref, lr_ref, li_ref, d_ref,
              wglu_ref, bglu_ref, o_ref, xr, xi, sr, si, *, steps, batch, lane_chunk):
    @pl.when(pl.program_id(0) == 0)
    def _():
        sr[...] = jnp.zeros_like(sr)
        si[...] = jnp.zeros_like(si)

    sub = SSM_PERM_STEPS
    parts = []
    for g in range(steps // sub):
        bm = jnp.concatenate([u_ref[b, g * sub:(g + 1) * sub, :] for b in range(batch)], axis=0)
        parts.append(jnp.dot(perm_ref[...], bm, preferred_element_type=F32).astype(BF16))
    u = jnp.concatenate(parts, axis=0)
    hw = u.shape[1] // 2
    hs = xr.shape[1] // 2
    for hf in range(2):
        uh = u[:, hf * hw:(hf + 1) * hw]
        xr[:, hf * hs:(hf + 1) * hs] = jnp.dot(uh, bre_ref[hf], preferred_element_type=F32)
        xi[:, hf * hs:(hf + 1) * hs] = jnp.dot(uh, bim_ref[hf], preferred_element_type=F32)

    for c in range(xr.shape[1] // lane_chunk):
        cs = slice(c * lane_chunk, (c + 1) * lane_chunk)
        ar = lr_ref[:, cs]
        ai = li_ref[:, cs]

        def step(t, carry, cs=cs, ar=ar, ai=ai):
            pr, pi = carry
            rows = pl.ds(pl.multiple_of(t * batch, batch), batch)
            nr = ar * pr - ai * pi + xr[rows, cs]
            ni = ar * pi + ai * pr + xi[rows, cs]
            xr[rows, cs] = nr
            xi[rows, cs] = ni
            return nr, ni

        pr, pi = lax.fori_loop(0, steps, step, (sr[:, cs], si[:, cs]), unroll=4)
        sr[:, cs] = pr
        si[:, cs] = pi

    ys = []
    for hf in range(2):
        xrb = xr[:, hf * hs:(hf + 1) * hs].astype(BF16)
        xib = xi[:, hf * hs:(hf + 1) * hs].astype(BF16)
        ys.append(jnp.dot(xrb, cre_ref[hf], preferred_element_type=F32)
                  - jnp.dot(xib, cim_ref[hf], preferred_element_type=F32))
    y = jnp.concatenate(ys, axis=1) + d_ref[...] * u.astype(F32)
    y = _gelu_tanh(y)
    z = jnp.dot(y.astype(BF16), wglu_ref[...], preferred_element_type=F32) + bglu_ref[...]
    out = (y * _sigmoid(z)).astype(BF16)
    rows_g = sub * batch
    for g in range(steps // sub):
        bm = jnp.dot(permt_ref[...], out[g * rows_g:(g + 1) * rows_g], preferred_element_type=F32).astype(BF16)
        for b in range(batch):
            o_ref[b, g * sub:(g + 1) * sub, :] = bm[b * sub:(b + 1) * sub]


def _time_major_permutation(batch, sub):
    n = batch * sub
    r = np.arange(n)
    p = np.zeros((n, n), np.float32)
    p[r, (r % batch) * sub + r // batch] = 1.0
    return p


def _ssm(u, bre, bim, cre, cim, lbr, lbi, d, wglu, bglu, *, batch, steps, lane_chunk):
    t, w = u.shape
    seq = t // batch
    ns = lbr.shape[1]
    rows = steps * batch
    perm = _time_major_permutation(batch, SSM_PERM_STEPS)
    const2 = lambda i: (0, 0)
    const3 = lambda i: (0, 0, 0)
    out = pl.pallas_call(
        functools.partial(_ssm_body, steps=steps, batch=batch, lane_chunk=lane_chunk),
        grid=(seq // steps,),
        in_specs=[pl.BlockSpec((batch, steps, w), lambda i: (0, i, 0)),
                  pl.BlockSpec(perm.shape, const2), pl.BlockSpec(perm.shape, const2),
                  pl.BlockSpec(bre.shape, const3), pl.BlockSpec(bim.shape, const3),
                  pl.BlockSpec(cre.shape, const3), pl.BlockSpec(cim.shape, const3),
                  pl.BlockSpec((1, ns), const2), pl.BlockSpec((1, ns), const2),
                  pl.BlockSpec((1, w), const2), pl.BlockSpec((w, w), const2), pl.BlockSpec((1, w), const2)],
        out_specs=pl.BlockSpec((batch, steps, w), lambda i: (0, i, 0)),
        out_shape=jax.ShapeDtypeStruct((batch, seq, w), BF16),
        scratch_shapes=[pltpu.VMEM((rows, ns), F32), pltpu.VMEM((rows, ns), F32),
                        pltpu.VMEM((batch, ns), F32), pltpu.VMEM((batch, ns), F32)],
        compiler_params=_cparams(("arbitrary",)),
        name="s5_scan",
    )(u.reshape(batch, seq, w), jnp.asarray(perm, BF16), jnp.asarray(perm.T, BF16),
      bre, bim, cre, cim, lbr, lbi, d, wglu, bglu)
    return out.reshape(t, w)


def _split_bf16(v):
    hi = v.astype(BF16)
    lo = (v - hi.astype(F32)).astype(BF16)
    return hi, lo


def _merge_body(x_ref, a_ref, s_ref, wg_ref, wba_ref, wbs_ref, wo_ref, g_ref, b_ref, *rest, with_router):
    if with_router:
        rhi_ref, rlo_ref, x1_ref, route_ref = rest
    else:
        (x1_ref,) = rest
    x = x_ref[...]
    d = x.shape[1]
    gates = jnp.dot(x.astype(BF16), wg_ref[...], preferred_element_type=F32)
    pa = jnp.dot(a_ref[...], wba_ref[...], preferred_element_type=F32)
    ps = jnp.dot(s_ref[...], wbs_ref[...], preferred_element_type=F32)
    merged = _sigmoid(gates[:, :d]) * pa + _sigmoid(gates[:, d:]) * ps
    y = jnp.dot(merged.astype(BF16), wo_ref[...], preferred_element_type=F32)
    x1 = _layer_norm(DEEPNORM_ALPHA * x + y, g_ref[...], b_ref[...])
    x1_ref[...] = x1
    if with_router:
        hi, lo = _split_bf16(x1)
        rhi = rhi_ref[...]
        logits = (jnp.dot(hi, rhi, preferred_element_type=F32)
                  + jnp.dot(lo, rhi, preferred_element_type=F32)
                  + jnp.dot(hi, rlo_ref[...], preferred_element_type=F32))
        lane = lax.broadcasted_iota(jnp.int32, logits.shape, 1).astype(F32)
        big = float(LANES)
        l1 = jnp.where(lane < N_EXPERTS, logits, -jnp.inf)
        m1 = jnp.max(l1, axis=-1, keepdims=True)
        i1 = jnp.min(jnp.where(l1 == m1, lane, big), axis=-1, keepdims=True)
        l2 = jnp.where(lane == i1, -jnp.inf, l1)
        m2 = jnp.max(l2, axis=-1, keepdims=True)
        i2 = jnp.min(jnp.where(l2 == m2, lane, big), axis=-1, keepdims=True)
        ed = jnp.exp(m2 - m1)
        g1 = 1.0 / (1.0 + ed)
        g2 = ed / (1.0 + ed)
        route = jnp.where(lane == 0.0, i1,
                          jnp.where(lane == 1.0, i2,
                                    jnp.where(lane == 2.0, g1, jnp.where(lane == 3.0, g2, 0.0))))
        route_ref[...] = route


def _merge(x, a, s, wg, wba, wbs, wo, g, b, router=None, *, tm):
    t, d = x.shape
    const = lambda i: (0, 0)
    row = lambda i: (i, 0)
    in_specs = [pl.BlockSpec((tm, d), row), pl.BlockSpec((tm, a.shape[1]), row), pl.BlockSpec((tm, s.shape[1]), row),
                pl.BlockSpec(wg.shape, const), pl.BlockSpec(wba.shape, const), pl.BlockSpec(wbs.shape, const),
                pl.BlockSpec(wo.shape, const), pl.BlockSpec((1, d), const), pl.BlockSpec((1, d), const)]
    args = [x, a, s, wg, wba, wbs, wo, g, b]
    out_specs = [pl.BlockSpec((tm, d), row)]
    out_shape = [jax.ShapeDtypeStruct((t, d), F32)]
    if router is not None:
        rhi, rlo = router
        in_specs += [pl.BlockSpec(rhi.shape, const), pl.BlockSpec(rlo.shape, const)]
        args += [rhi, rlo]
        out_specs.append(pl.BlockSpec((tm, LANES), row))
        out_shape.append(jax.ShapeDtypeStruct((t, LANES), F32))
    return pl.pallas_call(
        functools.partial(_merge_body, with_router=router is not None),
        grid=(t // tm,),
        in_specs=in_specs, out_specs=out_specs, out_shape=out_shape,
        compiler_params=_cparams(("parallel",)),
        name="merge_ln1",
    )(*args)


def _ple(xb, p_ref, wpg_ref, wpp_ref):
    gate = _sigmoid(jnp.dot(xb, wpg_ref[...], preferred_element_type=F32))
    return gate * jnp.dot(p_ref[...].astype(BF16), wpp_ref[...], preferred_element_type=F32)


def _ffn_body(x_ref, p_ref, wg_ref, wu_ref, wd_ref, wpg_ref, wpp_ref, g_ref, b_ref, o_ref, acc):
    x1 = x_ref[...]
    xb = x1.astype(BF16)
    for c, (c0, cw) in enumerate(FFN_CHUNKS):
        gt = jnp.dot(xb, wg_ref[:, c0:c0 + cw], preferred_element_type=F32)
        up = jnp.dot(xb, wu_ref[:, c0:c0 + cw], preferred_element_type=F32)
        act = (gt * _sigmoid(gt) * up).astype(BF16)
        contrib = jnp.dot(act, wd_ref[c0:c0 + cw, :], preferred_element_type=F32)
        if c == 0:
            acc[...] = contrib
        else:
            acc[...] += contrib
    h = DEEPNORM_ALPHA * x1 + acc[...] + _ple(xb, p_ref, wpg_ref, wpp_ref)
    o_ref[...] = _layer_norm(h, g_ref[...], b_ref[...])


def _resident(shape, index_map):
    return pl.BlockSpec(shape, index_map, pipeline_mode=pl.Buffered(1))


def _ffn(x1, p, w_gu, w_down, li, wpg, wpp, g, b, *, tm):
    t, d = x1.shape
    ff = w_down.shape[1]
    const = lambda i: (0, 0)
    row = lambda i: (i, 0)
    return pl.pallas_call(
        _ffn_body,
        grid=(t // tm,),
        in_specs=[pl.BlockSpec((tm, d), row), pl.BlockSpec((tm, p.shape[1]), row),
                  _resident((None, d, ff), lambda i: (li, 0, 0)),
                  _resident((None, d, ff), lambda i: (li, 0, 1)),
                  _resident((None, ff, d), lambda i: (li, 0, 0)),
                  _resident(wpg.shape, const), _resident(wpp.shape, const),
                  pl.BlockSpec((1, d), const), pl.BlockSpec((1, d), const)],
        out_specs=pl.BlockSpec((tm, d), row),
        out_shape=jax.ShapeDtypeStruct((t, d), F32),
        scratch_shapes=[pltpu.VMEM((tm, d), F32)],
        compiler_params=_cparams(("parallel",)),
        name="ffn_ln2",
    )(x1, p, w_gu, w_gu, w_down, wpg, wpp, g, b)


def _moe_body(te_ref, nu_ref, src_ref, dst_ref, x_hbm, wgu_ref, wd_ref, y_hbm,
              xg, xb, acc, ybuf, gsem, ssem, *, tm, fw, spare_row):
    i = pl.program_id(0)
    nu = nu_ref[0]
    slot = i % 2
    other = 1 - slot
    ffe = wd_ref.shape[0]
    nc = ffe // fw
    per_chunk = tm // nc

    def gather(tile, r, buf):
        row = src_ref[tile * tm + r]
        pltpu.make_async_copy(x_hbm.at[pl.ds(row, 1), :], xg.at[buf, pl.ds(r, 1), :], gsem.at[buf]).start()

    def scatter(row, r, buf):
        pltpu.make_async_copy(ybuf.at[buf, pl.ds(r, 1), :], y_hbm.at[pl.ds(row, 1), :], ssem.at[buf]).start()

    def prev_tile_dst(r):
        return jnp.where(i > 0, dst_ref[jnp.maximum(i - 1, 0) * tm + r], spare_row + r)

    @pl.when(i == 0)
    def _():
        ybuf[...] = jnp.zeros_like(ybuf)

        def one(r, c):
            gather(0, r, 0)
            return c
        lax.fori_loop(0, tm, one, 0)

    @pl.when(i <= nu)
    def _():
        pltpu.make_async_copy(x_hbm.at[pl.ds(0, tm), :], xg.at[slot], gsem.at[slot]).wait()

    @pl.when(i < nu)
    def _():
        xb[...] = xg[slot].astype(BF16)
        x = xb[...]
        for c in range(nc):
            c0 = c * fw
            gt = jnp.dot(x, wgu_ref[:, c0:c0 + fw], preferred_element_type=F32)
            up = jnp.dot(x, wgu_ref[:, ffe + c0:ffe + c0 + fw], preferred_element_type=F32)
            act = (gt * _sigmoid(gt) * up).astype(BF16)
            part = jnp.dot(act, wd_ref[c0:c0 + fw, :], preferred_element_type=F32)
            if c == 0:
                acc[...] = part
            else:
                acc[...] += part
            r1 = tm if c == nc - 1 else (c + 1) * per_chunk
            for r in range(c * per_chunk, r1):
                gather(i + 1, r, other)
                scatter(prev_tile_dst(r), r, other)

    @pl.when(i == nu)
    def _():
        def one(r, c):
            scatter(dst_ref[(i - 1) * tm + r], r, other)
            return c
        lax.fori_loop(0, tm, one, 0)

    @pl.when((i >= 1) & (i <= nu + 1))
    def _():
        pltpu.make_async_copy(ybuf.at[slot], y_hbm.at[pl.ds(0, tm), :], ssem.at[slot]).wait()

    @pl.when(i < nu)
    def _():
        ybuf[slot] = acc[...]


def _moe_experts(x1, w_gu, w_down, li, tile_expert, n_used, src, dst, *, tm, fw, out_rows):
    t, d = x1.shape
    ffe = w_down.shape[2]
    nt = src.shape[0] // tm
    return pl.pallas_call(
        functools.partial(_moe_body, tm=tm, fw=fw, spare_row=out_rows - tm),
        grid_spec=pltpu.PrefetchScalarGridSpec(
            num_scalar_prefetch=4,
            grid=(nt,),
            in_specs=[pl.BlockSpec(memory_space=pl.ANY),
                      _resident((None, None, d, 2 * ffe), lambda i, te, nu, s, ds: (li, te[i], 0, 0)),
                      _resident((None, None, ffe, d), lambda i, te, nu, s, ds: (li, te[i], 0, 0))],
            out_specs=pl.BlockSpec(memory_space=pl.ANY),
            scratch_shapes=[pltpu.VMEM((2, tm, d), F32), pltpu.VMEM((tm, d), BF16), pltpu.VMEM((tm, d), F32),
                            pltpu.VMEM((2, tm, d), F32),
                            pltpu.SemaphoreType.DMA((2,)), pltpu.SemaphoreType.DMA((2,))]),
        out_shape=jax.ShapeDtypeStruct((out_rows, d), F32),
        compiler_params=_cparams(("arbitrary",)),
        name="moe_experts",
    )(tile_expert, n_used, src, dst, x1, w_gu, w_down)


def _moe_out_body(x_ref, route_ref, y0_ref, y1_ref, p_ref, wpg_ref, wpp_ref, g_ref, b_ref, o_ref):
    x1 = x_ref[...]
    route = route_ref[...]
    ffn = route[:, 2:3] * y0_ref[...] + route[:, 3:4] * y1_ref[...]
    h = DEEPNORM_ALPHA * x1 + ffn + _ple(x1.astype(BF16), p_ref, wpg_ref, wpp_ref)
    o_ref[...] = _layer_norm(h, g_ref[...], b_ref[...])


def _moe_out(x1, route, p, y, wpg, wpp, g, b, *, tm):
    t, d = x1.shape
    nt = t // tm
    const = lambda i: (0, 0)
    row = lambda i: (i, 0)
    return pl.pallas_call(
        _moe_out_body,
        grid=(nt,),
        in_specs=[pl.BlockSpec((tm, d), row), pl.BlockSpec((tm, LANES), row),
                  pl.BlockSpec((tm, d), row), pl.BlockSpec((tm, d), lambda i: (nt + i, 0)),
                  pl.BlockSpec((tm, p.shape[1]), row),
                  pl.BlockSpec(wpg.shape, const), pl.BlockSpec(wpp.shape, const),
                  pl.BlockSpec((1, d), const), pl.BlockSpec((1, d), const)],
        out_specs=pl.BlockSpec((tm, d), row),
        out_shape=jax.ShapeDtypeStruct((t, d), F32),
        compiler_params=_cparams(("parallel",)),
        name="moe_combine_ln2",
    )(x1, route, y, y, p, wpg, wpp, g, b)


def _routing_tables(route, *, tm, n_experts):
    t = route.shape[0]
    eids = route[:, :2].astype(jnp.int32).reshape(-1)
    onehot = (eids[:, None] == jnp.arange(n_experts, dtype=jnp.int32)[None, :]).astype(jnp.int32)
    csum = jnp.cumsum(onehot, axis=0)
    rank = jnp.sum((csum - onehot) * onehot, axis=1)
    cnt = csum[-1]
    ntile = (cnt + tm - 1) // tm
    tile_end = jnp.cumsum(ntile)
    row_off = (tile_end - ntile) * tm
    pos = jnp.sum(onehot * row_off[None, :], axis=1) + rank
    r = 2 * t + (n_experts + 2) * tm
    slot = jnp.arange(2 * t, dtype=jnp.int32)
    dst = (2 * t + jnp.arange(r, dtype=jnp.int32) % tm).at[pos].set((slot % 2) * t + slot // 2)
    src = jnp.where(dst < 2 * t, dst % t, 0)
    n_used = tile_end[-1:]
    tiles = jnp.arange(r // tm, dtype=jnp.int32)
    te = jnp.sum((tiles[:, None] >= tile_end[None, :]).astype(jnp.int32), axis=1)
    last = jnp.sum((n_used - 1 >= tile_end).astype(jnp.int32))
    te = jnp.where(tiles < n_used, te, last).astype(jnp.int32)
    out_rows = 2 * t + tm
    return te, n_used.astype(jnp.int32), src.astype(jnp.int32), dst.astype(jnp.int32), out_rows


def kernel(x, p, rel_bias, w_in, attn_sinks, ssm_lambda_re, ssm_lambda_im, ssm_log_dt, ssm_b_re, ssm_b_im,
           ssm_c_re, ssm_c_im, ssm_d, w_glu, b_glu, w_branch_attn, w_branch_ssm, w_out, ln1_g, ln1_b,
           ffn_w_gate_up, ffn_w_down, moe_router, moe_w_gate_up, moe_w_down, ple_w_proj, ple_w_gate,
           ln2_g, ln2_b):
    bsz, seq, d = x.shape
    t = bsz * seq
    depth = w_in.shape[0]

    xt = x.reshape(t, d)
    pt = p.reshape(depth, t, p.shape[-1])

    bias = _attn_bias_table(rel_bias)
    ffn_gu = ffn_w_gate_up.astype(BF16)
    ffn_dn = ffn_w_down.astype(BF16)
    moe_gu = moe_w_gate_up.astype(BF16)
    moe_dn = moe_w_down.astype(BF16)

    aw, kw = ATTN_WIDTH, KV_WIDTH
    for i in range(depth):
        wi = w_in[i]
        u0 = aw + 2 * kw
        w_proj = jnp.concatenate([wi[:, :aw], _per_head_kv_columns(wi[:, aw:aw + kw]),
                                  _per_head_kv_columns(wi[:, aw + kw:u0]), wi[:, u0:u0 + SSM_WIDTH]],
                                 axis=1).astype(BF16)
        w_gates = wi[:, u0 + SSM_WIDTH:].astype(BF16)

        q, kv, u = _inproj(xt, w_proj, tm=TM_PROJ)
        a_out = _attention(q, kv, attn_sinks[i], bias, seq=seq, batch=bsz)

        lbr, lbi, bbr, bbi = _ssm_params(ssm_lambda_re[i], ssm_lambda_im[i], ssm_log_dt[i],
                                         ssm_b_re[i], ssm_b_im[i])
        bre = _block_diag_halves(bbr).astype(BF16)
        bim = _block_diag_halves(bbi).astype(BF16)
        cre = _block_diag_halves(jnp.transpose(ssm_c_re[i], (0, 2, 1))).astype(BF16)
        cim = _block_diag_halves(jnp.transpose(ssm_c_im[i], (0, 2, 1))).astype(BF16)
        s_out = _ssm(u, bre, bim, cre, cim, lbr.reshape(1, -1), lbi.reshape(1, -1),
                     ssm_d[i].reshape(1, -1), w_glu[i].astype(BF16), b_glu[i].reshape(1, -1),
                     batch=bsz, steps=SSM_STEPS, lane_chunk=SSM_LANE_CHUNK)

        g1, b1 = ln1_g[i].reshape(1, d), ln1_b[i].reshape(1, d)
        g2, b2 = ln2_g[i].reshape(1, d), ln2_b[i].reshape(1, d)
        wpg = ple_w_gate[i].astype(BF16)
        wpp = ple_w_proj[i].astype(BF16)
        merge_w = (w_gates, w_branch_attn[i].astype(BF16), w_branch_ssm[i].astype(BF16), w_out[i].astype(BF16))
        if i % 2 == 0:
            (x1,) = _merge(xt, a_out, s_out, *merge_w, g1, b1, tm=TM_MERGE)
            xt = _ffn(x1, pt[i], ffn_gu, ffn_dn, i // 2, wpg, wpp, g2, b2, tm=TM_FFN)
        else:
            rt = jnp.pad(moe_router[i // 2], ((0, 0), (0, LANES - N_EXPERTS)))
            rhi = rt.astype(BF16)
            rlo = (rt - rhi.astype(F32)).astype(BF16)
            x1, route = _merge(xt, a_out, s_out, *merge_w, g1, b1, (rhi, rlo), tm=TM_MERGE)
            te, n_used, src, dst, out_rows = _routing_tables(route, tm=TM_MOE, n_experts=N_EXPERTS)
            y = _moe_experts(x1, moe_gu, moe_dn, i // 2, te, n_used, src, dst, tm=TM_MOE, fw=FW_MOE,
                             out_rows=out_rows)
            xt = _moe_out(x1, route, pt[i], y, wpg, wpp, g2, b2, tm=TM_FFN)

    return xt.reshape(bsz, seq, d)
```

```python
import functools
import math

import jax
import jax.numpy as jnp
import numpy as np
from jax import lax
from jax.experimental import pallas as pl
from jax.experimental.pallas import tpu as pltpu

F32 = jnp.float32
BF16 = jnp.bfloat16

D_MODEL = 1024
BATCH = 16
SEQ = 2048
DEPTH = 4
HEAD_DIM = 64
N_Q_HEADS = 8
N_KV_HEADS = 2
ATTN_WIDTH = N_Q_HEADS * HEAD_DIM
KV_WIDTH = N_KV_HEADS * HEAD_DIM
ATTN_BLOCK = 128
N_BUCKETS = 32
MAX_DISTANCE = 128
SSM_WIDTH = 512
SSM_GROUP_CH = 16
SSM_GROUPS = SSM_WIDTH // SSM_GROUP_CH
SSM_STATE = 64
D_FF = 2816
N_EXPERTS = 8
D_FF_EXPERT = 3584
PLE_DIM = 256
DEEPNORM_ALPHA = (2 * DEPTH) ** 0.25
LN_EPS = 1e-5
NEG_INF = -1e30
LOG2_E = math.log2(math.e)

LANES = 128
VMEM_LIMIT_BYTES = 56 * 1024 * 1024

TM_PROJ = 1024
ATTN_BLOCKS_PER_STEP = 8
TM_MERGE = 512
TM_FFN = 512
TM_MOE = 512
FW_MOE = 512
SLOT_TABLE_GROUPS = 16
SSM_STEPS = 32
SSM_PERM_STEPS = 16
SSM_LANE_CHUNK = 512
FFN_CHUNKS = ((0, 768), (768, 768), (1536, 768), (2304, 512))


def _cparams(sem):
    return pltpu.CompilerParams(dimension_semantics=sem, vmem_limit_bytes=VMEM_LIMIT_BYTES)


def _layer_norm(h, g, b):
    mu = jnp.mean(h, axis=-1, keepdims=True)
    c = h - mu
    var = jnp.mean(c * c, axis=-1, keepdims=True)
    return c * lax.rsqrt(var + LN_EPS) * g + b


def _sigmoid(v):
    return 1.0 / (1.0 + jnp.exp(-v))


def _gelu_tanh(v):
    return 0.5 * v * (1.0 + jnp.tanh(math.sqrt(2.0 / math.pi) * (v + 0.044715 * (v * v * v))))


def _inproj_body(x_ref, w_ref, q_ref, kv_ref, u_ref):
    xb = x_ref[...].astype(BF16)
    z = jnp.dot(xb, w_ref[...], preferred_element_type=F32)
    aw = q_ref.shape[1]
    kw = kv_ref.shape[1]
    q_ref[...] = (z[:, :aw] * (LOG2_E * HEAD_DIM ** -0.5)).astype(BF16)
    kv_ref[...] = z[:, aw:aw + kw].astype(BF16)
    u_ref[...] = z[:, aw + kw:].astype(BF16)


def _inproj(x, w, *, tm):
    t, d = x.shape
    n = w.shape[1]
    aw, kw = ATTN_WIDTH, 2 * (N_KV_HEADS * 2 * LANES)
    uw = n - aw - kw
    return pl.pallas_call(
        _inproj_body,
        grid=(t // tm,),
        in_specs=[pl.BlockSpec((tm, d), lambda i: (i, 0)),
                  pl.BlockSpec((d, n), lambda i: (0, 0))],
        out_specs=[pl.BlockSpec((tm, aw), lambda i: (i, 0)),
                   pl.BlockSpec((tm, kw), lambda i: (i, 0)),
                   pl.BlockSpec((tm, uw), lambda i: (i, 0))],
        out_shape=[jax.ShapeDtypeStruct((t, aw), BF16),
                   jax.ShapeDtypeStruct((t, kw), BF16),
                   jax.ShapeDtypeStruct((t, uw), BF16)],
        compiler_params=_cparams(("parallel",)),
        name="inproj",
    )(x, w)


def _attn_body(sink_ref, q_ref, kvc_ref, kvp_ref, bias0_ref, bias_ref, o_ref, *, blk, nbs):
    v0 = N_KV_HEADS * 2 * LANES
    group = N_Q_HEADS // N_KV_HEADS
    for j in range(nbs):
        rows = slice(j * blk, (j + 1) * blk)
        b_ref = bias0_ref if j == 0 else bias_ref
        for tile in range(N_Q_HEADS // 2):
            qt = q_ref[rows, tile * LANES:(tile + 1) * LANES]
            acc = None
            for par in range(2):
                h = 2 * tile + par
                kt = 2 * (h // group) + par
                ks = slice(kt * LANES, (kt + 1) * LANES)
                vs = slice(v0 + kt * LANES, v0 + (kt + 1) * LANES)
                if j == 0:
                    k = jnp.concatenate([kvp_ref[:, ks], kvc_ref[0:blk, ks]], axis=0)
                    v = jnp.concatenate([kvp_ref[:, vs], kvc_ref[0:blk, vs]], axis=0)
                else:
                    k = kvc_ref[(j - 1) * blk:(j + 1) * blk, ks]
                    v = kvc_ref[(j - 1) * blk:(j + 1) * blk, vs]
                s = lax.dot_general(qt, k, (((1,), (1,)), ((), ())), preferred_element_type=F32)
                s = s + b_ref[h]
                sink = sink_ref[h]
                m = jnp.maximum(jnp.max(s, axis=-1, keepdims=True), sink)
                e = jnp.exp2(s - m)
                denom = jnp.sum(e, axis=-1, keepdims=True) + jnp.exp2(sink - m)
                o = jnp.dot(e.astype(BF16), v, preferred_element_type=F32) / denom
                acc = o if acc is None else acc + o
            o_ref[rows, tile * LANES:(tile + 1) * LANES] = acc.astype(BF16)


def _attention(q, kv, sinks, bias, *, seq, batch):
    t = q.shape[0]
    blk = ATTN_BLOCK
    nbs = ATTN_BLOCKS_PER_STEP
    ns = seq // (blk * nbs)
    aw, kw = q.shape[1], kv.shape[1]
    table = (None, N_Q_HEADS, blk, 2 * blk)
    return pl.pallas_call(
        functools.partial(_attn_body, blk=blk, nbs=nbs),
        grid=(batch, ns),
        in_specs=[pl.BlockSpec(memory_space=pltpu.SMEM),
                  pl.BlockSpec((blk * nbs, aw), lambda b, n: (b * ns + n, 0)),
                  pl.BlockSpec((blk * nbs, kw), lambda b, n: (b * ns + n, 0)),
                  pl.BlockSpec((blk, kw), lambda b, n: ((b * ns + n) * nbs - jnp.minimum(n, 1), 0)),
                  pl.BlockSpec(table, lambda b, n: (jnp.minimum(n, 1), 0, 0, 0)),
                  pl.BlockSpec(table, lambda b, n: (1, 0, 0, 0))],
        out_specs=pl.BlockSpec((blk * nbs, aw), lambda b, n: (b * ns + n, 0)),
        out_shape=jax.ShapeDtypeStruct((t, aw), BF16),
        compiler_params=_cparams(("parallel", "parallel")),
        name="swa_attention",
    )(sinks, q, kv, kv, bias, bias)


def _per_head_kv_columns(w):
    d = w.shape[0]
    wh = w.reshape(d, N_KV_HEADS, 1, 1, HEAD_DIM)
    half = jnp.arange(2)[None, None, None, :, None]
    par = jnp.arange(2)[None, None, :, None, None]
    tiles = jnp.where(half == par, wh, 0.0)
    return tiles.reshape(d, N_KV_HEADS * 2 * LANES)


def _attn_bias_table(rel_bias):
    blk = ATTN_BLOCK
    d = np.arange(blk, dtype=np.int32)
    max_exact = N_BUCKETS // 2
    d_f = np.maximum(d, 1).astype(np.float32)
    large = max_exact + (np.log(d_f / np.float32(max_exact)) / np.float32(math.log(MAX_DISTANCE / max_exact))
                         * np.float32(N_BUCKETS - max_exact)).astype(np.int32)
    large = np.minimum(large, N_BUCKETS - 1)
    bucket = np.where(d < max_exact, d, large)
    vals = jnp.transpose(rel_bias.astype(F32)[bucket], (1, 0))
    h = vals.shape[0]
    neg = jnp.full((h, blk), NEG_INF, F32)
    strip = jnp.concatenate([neg, vals[:, ::-1], neg], axis=1)
    rows = [strip[:, blk - 1 - i:blk - 1 - i + 2 * blk] for i in range(blk)]
    table = jnp.stack(rows, axis=1) * LOG2_E
    first = jnp.where(np.arange(2 * blk)[None, None, :] < blk, NEG_INF, table)
    return jnp.stack([first, table], axis=0)


def _ssm_param_body(lre_ref, lim_ref, ldt_ref, bre_ref, bim_ref, lbr_ref, lbi_ref, bbr_ref, bbi_ref):
    lre, lim = lre_ref[...], lim_ref[...]
    dt = jnp.exp(ldt_ref[...])
    mag = jnp.exp(lre * dt)
    lbr = mag * jnp.cos(lim * dt)
    lbi = mag * jnp.sin(lim * dt)
    lbr_ref[...] = lbr
    lbi_ref[...] = lbi
    nr, ni = lbr - 1.0, lbi
    inv = 1.0 / (lre * lre + lim * lim)
    cr = (nr * lre + ni * lim) * inv
    ci = (ni * lre - nr * lim) * inv
    br, bi = bre_ref[...], bim_ref[...]
    bbr_ref[...] = cr * br - ci * bi
    bbi_ref[...] = cr * bi + ci * br


def _ssm_params(lam_re, lam_im, log_dt, b_re, b_im):
    g, p = lam_re.shape
    h = b_re.shape[2]
    ldt = jnp.broadcast_to(log_dt[:, None, None], (g, 1, p))
    brt = jnp.transpose(b_re, (0, 2, 1))
    bit = jnp.transpose(b_im, (0, 2, 1))
    sd = jax.ShapeDtypeStruct
    return pl.pallas_call(
        _ssm_param_body,
        out_shape=[sd((g, 1, p), F32), sd((g, 1, p), F32), sd((g, h, p), F32), sd((g, h, p), F32)],
        name="ssm_params",
    )(lam_re.reshape(g, 1, p), lam_im.reshape(g, 1, p), ldt, brt, bit)


def _block_diag_halves(w):
    g, a, b = w.shape
    hg = g // 2
    wh = w.reshape(2, hg, a, b)
    eye = jnp.eye(hg, dtype=bool)[None, :, None, :, None]
    full = jnp.where(eye, wh[:, :, :, None, :], 0.0)
    return full.reshape(2, hg * a, hg * b)


def _ssm_body(u_ref, perm_ref, permt_ref, bre_ref, bim_ref, cre_ref, cim_ref, lr_ref, li_ref, d_ref,
              wglu_ref, bglu_ref, o_ref, xr0, xi0, ub0, xr1, xi1, ub1, sr, si, *, steps, batch, lane_chunk):
    s = pl.program_id(0)

    @pl.when(s == 0)
    def _():
        for ref in (xr0, xi0, ub0, xr1, xi1, ub1, sr, si):
            ref[...] = jnp.zeros_like(ref)

    sub = SSM_PERM_STEPS
    rows_g = sub * batch

    def stage(fill, scan):
        xr_f, xi_f, ub_f = fill
        xr, xi, ub = scan
        parts = []
        for g in range(steps // sub):
            bm = jnp.concatenate([u_ref[b, g * sub:(g + 1) * sub, :] for b in range(batch)], axis=0)
            parts.append(jnp.dot(perm_ref[...], bm, preferred_element_type=F32).astype(BF16))
        u_new = jnp.concatenate(parts, axis=0)
        ub_f[...] = u_new
        hw = u_new.shape[1] // 2
        hs = xr.shape[1] // 2
        for hf in range(2):
            uh = u_new[:, hf * hw:(hf + 1) * hw]
            xr_f[:, hf * hs:(hf + 1) * hs] = jnp.dot(uh, bre_ref[hf], preferred_element_type=F32)
            xi_f[:, hf * hs:(hf + 1) * hs] = jnp.dot(uh, bim_ref[hf], preferred_element_type=F32)
        for c in range(xr.shape[1] // lane_chunk):
            cs = slice(c * lane_chunk, (c + 1) * lane_chunk)
            ar = lr_ref[:, cs]
            ai = li_ref[:, cs]
            pr, pi = sr[:, cs], si[:, cs]
            for t in range(steps):
                rows = slice(t * batch, (t + 1) * batch)
                nr = ar * pr - ai * pi + xr[rows, cs]
                ni = ar * pi + ai * pr + xi[rows, cs]
                xr[rows, cs] = nr
                xi[rows, cs] = ni
                pr, pi = nr, ni
            sr[:, cs] = pr
            si[:, cs] = pi
        ys = []
        for hf in range(2):
            xrb = xr[:, hf * hs:(hf + 1) * hs].astype(BF16)
            xib = xi[:, hf * hs:(hf + 1) * hs].astype(BF16)
            ys.append(jnp.dot(xrb, cre_ref[hf], preferred_element_type=F32)
                      - jnp.dot(xib, cim_ref[hf], preferred_element_type=F32))
        y = jnp.concatenate(ys, axis=1) + d_ref[...] * ub[...].astype(F32)
        y = _gelu_tanh(y)
        z = jnp.dot(y.astype(BF16), wglu_ref[...], preferred_element_type=F32) + bglu_ref[...]
        out = (y * _sigmoid(z)).astype(BF16)
        for g in range(steps // sub):
            bm = jnp.dot(permt_ref[...], out[g * rows_g:(g + 1) * rows_g], preferred_element_type=F32).astype(BF16)
            for b in range(batch):
                o_ref[b, g * sub:(g + 1) * sub, :] = bm[b * sub:(b + 1) * sub]

    @pl.when(s % 2 == 0)
    def _():
        stage((xr0, xi0, ub0), (xr1, xi1, ub1))

    @pl.when(s % 2 == 1)
    def _():
        stage((xr1, xi1, ub1), (xr0, xi0, ub0))


def _time_major_permutation(batch, sub):
    n = batch * sub
    r = np.arange(n)
    p = np.zeros((n, n), np.float32)
    p[r, (r % batch) * sub + r // batch] = 1.0
    return p


def _ssm(u, bre, bim, cre, cim, lbr, lbi, d, wglu, bglu, *, batch, steps, lane_chunk):
    t, w = u.shape
    seq = t // batch
    n = seq // steps
    ns = lbr.shape[1]
    rows = steps * batch
    perm = _time_major_permutation(batch, SSM_PERM_STEPS)
    const2 = lambda i: (0, 0)
    const3 = lambda i: (0, 0, 0)
    buf = [pltpu.VMEM((rows, ns), F32), pltpu.VMEM((rows, ns), F32), pltpu.VMEM((rows, w), BF16)]
    out = pl.pallas_call(
        functools.partial(_ssm_body, steps=steps, batch=batch, lane_chunk=lane_chunk),
        grid=(n + 1,),
        in_specs=[pl.BlockSpec((batch, steps, w), lambda i: (0, jnp.minimum(i, n - 1), 0)),
                  pl.BlockSpec(perm.shape, const2), pl.BlockSpec(perm.shape, const2),
                  pl.BlockSpec(bre.shape, const3), pl.BlockSpec(bim.shape, const3),
                  pl.BlockSpec(cre.shape, const3), pl.BlockSpec(cim.shape, const3),
                  pl.BlockSpec((1, ns), const2), pl.BlockSpec((1, ns), const2),
                  pl.BlockSpec((1, w), const2), pl.BlockSpec((w, w), const2), pl.BlockSpec((1, w), const2)],
        out_specs=pl.BlockSpec((batch, steps, w), lambda i: (0, jnp.maximum(i - 1, 0), 0)),
        out_shape=jax.ShapeDtypeStruct((batch, seq, w), BF16),
        scratch_shapes=buf + buf + [pltpu.VMEM((batch, ns), F32), pltpu.VMEM((batch, ns), F32)],
        compiler_params=_cparams(("arbitrary",)),
        name="s5_scan",
    )(u.reshape(batch, seq, w), jnp.asarray(perm, BF16), jnp.asarray(perm.T, BF16),
      bre, bim, cre, cim, lbr, lbi, d, wglu, bglu)
    return out.reshape(t, w)


def _split_bf16(v):
    hi = v.astype(BF16)
    lo = (v - hi.astype(F32)).astype(BF16)
    return hi, lo


def _merge_body(x_ref, a_ref, s_ref, wg_ref, wba_ref, wbs_ref, wo_ref, g_ref, b_ref, *rest, with_router):
    if with_router:
        rhi_ref, rlo_ref, x1_ref, route_ref = rest
    else:
        (x1_ref,) = rest
    x = x_ref[...]
    d = x.shape[1]
    gates = jnp.dot(x.astype(BF16), wg_ref[...], preferred_element_type=F32)
    pa = jnp.dot(a_ref[...], wba_ref[...], preferred_element_type=F32)
    ps = jnp.dot(s_ref[...], wbs_ref[...], preferred_element_type=F32)
    merged = _sigmoid(gates[:, :d]) * pa + _sigmoid(gates[:, d:]) * ps
    y = jnp.dot(merged.astype(BF16), wo_ref[...], preferred_element_type=F32)
    x1 = _layer_norm(DEEPNORM_ALPHA * x + y, g_ref[...], b_ref[...])
    x1_ref[...] = x1
    if with_router:
        hi, lo = _split_bf16(x1)
        rhi = rhi_ref[...]
        logits = (jnp.dot(hi, rhi, preferred_element_type=F32)
                  + jnp.dot(lo, rhi, preferred_element_type=F32)
                  + jnp.dot(hi, rlo_ref[...], preferred_element_type=F32))
        lane = lax.broadcasted_iota(jnp.int32, logits.shape, 1).astype(F32)
        big = float(LANES)
        l1 = jnp.where(lane < N_EXPERTS, logits, -jnp.inf)
        m1 = jnp.max(l1, axis=-1, keepdims=True)
        i1 = jnp.min(jnp.where(l1 == m1, lane, big), axis=-1, keepdims=True)
        l2 = jnp.where(lane == i1, -jnp.inf, l1)
        m2 = jnp.max(l2, axis=-1, keepdims=True)
        i2 = jnp.min(jnp.where(l2 == m2, lane, big), axis=-1, keepdims=True)
        ed = jnp.exp(m2 - m1)
        g1 = 1.0 / (1.0 + ed)
        g2 = ed / (1.0 + ed)
        route = jnp.where(lane == 0.0, i1,
                          jnp.where(lane == 1.0, i2,
                                    jnp.where(lane == 2.0, g1, jnp.where(lane == 3.0, g2, 0.0))))
        route_ref[...] = route


def _merge(x, a, s, wg, wba, wbs, wo, g, b, router=None, *, tm):
    t, d = x.shape
    const = lambda i: (0, 0)
    row = lambda i: (i, 0)
    in_specs = [pl.BlockSpec((tm, d), row), pl.BlockSpec((tm, a.shape[1]), row), pl.BlockSpec((tm, s.shape[1]), row),
                pl.BlockSpec(wg.shape, const), pl.BlockSpec(wba.shape, const), pl.BlockSpec(wbs.shape, const),
                pl.BlockSpec(wo.shape, const), pl.BlockSpec((1, d), const), pl.BlockSpec((1, d), const)]
    args = [x, a, s, wg, wba, wbs, wo, g, b]
    out_specs = [pl.BlockSpec((tm, d), row)]
    out_shape = [jax.ShapeDtypeStruct((t, d), F32)]
    if router is not None:
        rhi, rlo = router
        in_specs += [pl.BlockSpec(rhi.shape, const), pl.BlockSpec(rlo.shape, const)]
        args += [rhi, rlo]
        out_specs.append(pl.BlockSpec((tm, LANES), row))
        out_shape.append(jax.ShapeDtypeStruct((t, LANES), F32))
    return pl.pallas_call(
        functools.partial(_merge_body, with_router=router is not None),
        grid=(t // tm,),
        in_specs=in_specs, out_specs=out_specs, out_shape=out_shape,
        compiler_params=_cparams(("parallel",)),
        name="merge_ln1",
    )(*args)


def _ple(xb, p_ref, wpg_ref, wpp_ref):
    gate = _sigmoid(jnp.dot(xb, wpg_ref[...], preferred_element_type=F32))
    return gate * jnp.dot(p_ref[...].astype(BF16), wpp_ref[...], preferred_element_type=F32)


def _ffn_body(x_ref, p_ref, wg_ref, wu_ref, wd_ref, wpg_ref, wpp_ref, g_ref, b_ref, o_ref, acc):
    x1 = x_ref[...]
    xb = x1.astype(BF16)
    for c, (c0, cw) in enumerate(FFN_CHUNKS):
        gt = jnp.dot(xb, wg_ref[:, c0:c0 + cw], preferred_element_type=F32)
        up = jnp.dot(xb, wu_ref[:, c0:c0 + cw], preferred_element_type=F32)
        act = (gt * _sigmoid(gt) * up).astype(BF16)
        contrib = jnp.dot(act, wd_ref[c0:c0 + cw, :], preferred_element_type=F32)
        if c == 0:
            acc[...] = contrib
        else:
            acc[...] += contrib
    h = DEEPNORM_ALPHA * x1 + acc[...] + _ple(xb, p_ref, wpg_ref, wpp_ref)
    o_ref[...] = _layer_norm(h, g_ref[...], b_ref[...])


def _resident(shape, index_map):
    return pl.BlockSpec(shape, index_map, pipeline_mode=pl.Buffered(1))


def _ffn(x1, p, w_gu, w_down, li, wpg, wpp, g, b, *, tm):
    t, d = x1.shape
    ff = w_down.shape[1]
    const = lambda i: (0, 0)
    row = lambda i: (i, 0)
    return pl.pallas_call(
        _ffn_body,
        grid=(t // tm,),
        in_specs=[pl.BlockSpec((tm, d), row), pl.BlockSpec((tm, p.shape[1]), row),
                  _resident((None, d, ff), lambda i: (li, 0, 0)),
                  _resident((None, d, ff), lambda i: (li, 0, 1)),
                  _resident((None, ff, d), lambda i: (li, 0, 0)),
                  _resident(wpg.shape, const), _resident(wpp.shape, const),
                  pl.BlockSpec((1, d), const), pl.BlockSpec((1, d), const)],
        out_specs=pl.BlockSpec((tm, d), row),
        out_shape=jax.ShapeDtypeStruct((t, d), F32),
        scratch_shapes=[pltpu.VMEM((tm, d), F32)],
        compiler_params=_cparams(("parallel",)),
        name="ffn_ln2",
    )(x1, p, w_gu, w_gu, w_down, wpg, wpp, g, b)


def _moe_body(te_ref, nu_ref, src_ref, dst_ref, x_hbm, wgu_ref, wd_ref, y_hbm,
              xg, xb, acc, ybuf, gsem, ssem, *, tm, fw, spare_row):
    i = pl.program_id(0)
    nu = nu_ref[0]
    slot = i % 2
    other = 1 - slot
    ffe = wd_ref.shape[0]
    nc = ffe // fw
    per_chunk = tm // nc

    def gather(tile, r, buf):
        row = src_ref[tile * tm + r]
        pltpu.make_async_copy(x_hbm.at[pl.ds(row, 1), :], xg.at[buf, pl.ds(r, 1), :], gsem.at[buf]).start()

    def scatter(row, r, buf):
        pltpu.make_async_copy(ybuf.at[buf, pl.ds(r, 1), :], y_hbm.at[pl.ds(row, 1), :], ssem.at[buf]).start()

    def prev_tile_dst(r):
        return jnp.where(i > 0, dst_ref[jnp.maximum(i - 1, 0) * tm + r], spare_row + r)

    @pl.when(i == 0)
    def _():
        ybuf[...] = jnp.zeros_like(ybuf)

        def one(r, c):
            gather(0, r, 0)
            return c
        lax.fori_loop(0, tm, one, 0)

    @pl.when(i <= nu)
    def _():
        pltpu.make_async_copy(x_hbm.at[pl.ds(0, tm), :], xg.at[slot], gsem.at[slot]).wait()

    @pl.when(i < nu)
    def _():
        xb[...] = xg[slot].astype(BF16)
        x = xb[...]
        for c in range(nc):
            c0 = c * fw
            gt = jnp.dot(x, wgu_ref[:, c0:c0 + fw], preferred_element_type=F32)
            up = jnp.dot(x, wgu_ref[:, ffe + c0:ffe + c0 + fw], preferred_element_type=F32)
            act = (gt * _sigmoid(gt) * up).astype(BF16)
            part = jnp.dot(act, wd_ref[c0:c0 + fw, :], preferred_element_type=F32)
            if c == 0:
                acc[...] = part
            else:
                acc[...] += part
            r1 = tm if c == nc - 1 else (c + 1) * per_chunk
            for r in range(c * per_chunk, r1):
                gather(i + 1, r, other)
                scatter(prev_tile_dst(r), r, other)

    @pl.when(i == nu)
    def _():
        def one(r, c):
            scatter(dst_ref[(i - 1) * tm + r], r, other)
            return c
        lax.fori_loop(0, tm, one, 0)

    @pl.when((i >= 1) & (i <= nu + 1))
    def _():
        pltpu.make_async_copy(ybuf.at[slot], y_hbm.at[pl.ds(0, tm), :], ssem.at[slot]).wait()

    @pl.when(i < nu)
    def _():
        ybuf[slot] = acc[...]


def _moe_experts(x1, w_gu, w_down, li, tile_expert, n_used, src, dst, *, tm, fw, out_rows):
    t, d = x1.shape
    ffe = w_down.shape[2]
    nt = src.shape[0] // tm
    return pl.pallas_call(
        functools.partial(_moe_body, tm=tm, fw=fw, spare_row=out_rows - tm),
        grid_spec=pltpu.PrefetchScalarGridSpec(
            num_scalar_prefetch=4,
            grid=(nt,),
            in_specs=[pl.BlockSpec(memory_space=pl.ANY),
                      _resident((None, None, d, 2 * ffe), lambda i, te, nu, s, ds: (li, te[i], 0, 0)),
                      _resident((None, None, ffe, d), lambda i, te, nu, s, ds: (li, te[i], 0, 0))],
            out_specs=pl.BlockSpec(memory_space=pl.ANY),
            scratch_shapes=[pltpu.VMEM((2, tm, d), F32), pltpu.VMEM((tm, d), BF16), pltpu.VMEM((tm, d), F32),
                            pltpu.VMEM((2, tm, d), F32),
                            pltpu.SemaphoreType.DMA((2,)), pltpu.SemaphoreType.DMA((2,))]),
        out_shape=jax.ShapeDtypeStruct((out_rows, d), F32),
        compiler_params=_cparams(("arbitrary",)),
        name="moe_experts",
    )(tile_expert, n_used, src, dst, x1, w_gu, w_down)


def _moe_out_body(x_ref, route_ref, y0_ref, y1_ref, p_ref, wpg_ref, wpp_ref, g_ref, b_ref, o_ref):
    x1 = x_ref[...]
    route = route_ref[...]
    ffn = route[:, 2:3] * y0_ref[...] + route[:, 3:4] * y1_ref[...]
    h = DEEPNORM_ALPHA * x1 + ffn + _ple(x1.astype(BF16), p_ref, wpg_ref, wpp_ref)
    o_ref[...] = _layer_norm(h, g_ref[...], b_ref[...])


def _moe_out(x1, route, p, y, wpg, wpp, g, b, *, tm):
    t, d = x1.shape
    nt = t // tm
    const = lambda i: (0, 0)
    row = lambda i: (i, 0)
    return pl.pallas_call(
        _moe_out_body,
        grid=(nt,),
        in_specs=[pl.BlockSpec((tm, d), row), pl.BlockSpec((tm, LANES), row),
                  pl.BlockSpec((tm, d), row), pl.BlockSpec((tm, d), lambda i: (nt + i, 0)),
                  pl.BlockSpec((tm, p.shape[1]), row),
                  pl.BlockSpec(wpg.shape, const), pl.BlockSpec(wpp.shape, const),
                  pl.BlockSpec((1, d), const), pl.BlockSpec((1, d), const)],
        out_specs=pl.BlockSpec((tm, d), row),
        out_shape=jax.ShapeDtypeStruct((t, d), F32),
        compiler_params=_cparams(("parallel",)),
        name="moe_combine_ln2",
    )(x1, route, y, y, p, wpg, wpp, g, b)


def _slot_table_body(pos_ref, dst_ref, *, t, tm, groups):
    s = pl.program_id(0)
    rc = dst_ref.shape[0] // groups
    tc = t // groups

    @pl.when(s < groups)
    def _():
        base = s * rc

        def fill(k, c):
            dst_ref[base + k] = 2 * t + (base + k) % tm
            return c
        lax.fori_loop(0, rc, fill, 0, unroll=8)

    @pl.when(s >= groups)
    def _():
        base = (s - groups) * tc

        def put(k, c):
            tok = base + k
            dst_ref[pos_ref[2 * tok]] = tok
            dst_ref[pos_ref[2 * tok + 1]] = t + tok
            return c
        lax.fori_loop(0, tc, put, 0, unroll=8)


def _slot_table(pos, *, t, tm, r):
    groups = SLOT_TABLE_GROUPS
    assert r % groups == 0 and t % groups == 0
    smem = pl.BlockSpec(memory_space=pltpu.SMEM)
    return pl.pallas_call(
        functools.partial(_slot_table_body, t=t, tm=tm, groups=groups),
        grid=(2 * groups,),
        in_specs=[smem], out_specs=smem,
        out_shape=jax.ShapeDtypeStruct((r,), jnp.int32),
        compiler_params=_cparams(("arbitrary",)),
        name="moe_slot_table",
    )(pos)


def _routing_tables(route, *, tm, n_experts):
    t = route.shape[0]
    eids = route[:, :2].astype(jnp.int32).reshape(-1)
    onehot = (eids[:, None] == jnp.arange(n_experts, dtype=jnp.int32)[None, :]).astype(jnp.int32)
    csum = jnp.cumsum(onehot, axis=0)
    rank = jnp.sum((csum - onehot) * onehot, axis=1)
    cnt = csum[-1]
    ntile = (cnt + tm - 1) // tm
    tile_end = jnp.cumsum(ntile)
    row_off = (tile_end - ntile) * tm
    pos = jnp.sum(onehot * row_off[None, :], axis=1) + rank
    r = 2 * t + (n_experts + 2) * tm
    dst = _slot_table(pos.astype(jnp.int32), t=t, tm=tm, r=r)
    src = jnp.where(dst < 2 * t, dst % t, 0)
    n_used = tile_end[-1:]
    tiles = jnp.arange(r // tm, dtype=jnp.int32)
    te = jnp.sum((tiles[:, None] >= tile_end[None, :]).astype(jnp.int32), axis=1)
    last = jnp.sum((n_used - 1 >= tile_end).astype(jnp.int32))
    te = jnp.where(tiles < n_used, te, last).astype(jnp.int32)
    out_rows = 2 * t + tm
    return te, n_used.astype(jnp.int32), src.astype(jnp.int32), dst.astype(jnp.int32), out_rows


def kernel(x, p, rel_bias, w_in, attn_sinks, ssm_lambda_re, ssm_lambda_im, ssm_log_dt, ssm_b_re, ssm_b_im,
           ssm_c_re, ssm_c_im, ssm_d, w_glu, b_glu, w_branch_attn, w_branch_ssm, w_out, ln1_g, ln1_b,
           ffn_w_gate_up, ffn_w_down, moe_router, moe_w_gate_up, moe_w_down, ple_w_proj, ple_w_gate,
           ln2_g, ln2_b):
    bsz, seq, d = x.shape
    t = bsz * seq
    depth = w_in.shape[0]

    xt = x.reshape(t, d)
    pt = p.reshape(depth, t, p.shape[-1])

    bias = _attn_bias_table(rel_bias)
    ffn_gu = ffn_w_gate_up.astype(BF16)
    ffn_dn = ffn_w_down.astype(BF16)
    moe_gu = moe_w_gate_up.astype(BF16)
    moe_dn = moe_w_down.astype(BF16)

    aw, kw = ATTN_WIDTH, KV_WIDTH
    for i in range(depth):
        wi = w_in[i]
        u0 = aw + 2 * kw
        w_proj = jnp.concatenate([wi[:, :aw], _per_head_kv_columns(wi[:, aw:aw + kw]),
                                  _per_head_kv_columns(wi[:, aw + kw:u0]), wi[:, u0:u0 + SSM_WIDTH]],
                                 axis=1).astype(BF16)
        w_gates = wi[:, u0 + SSM_WIDTH:].astype(BF16)

        q, kv, u = _inproj(xt, w_proj, tm=TM_PROJ)
        a_out = _attention(q, kv, attn_sinks[i] * LOG2_E, bias, seq=seq, batch=bsz)

        lbr, lbi, bbr, bbi = _ssm_params(ssm_lambda_re[i], ssm_lambda_im[i], ssm_log_dt[i],
                                         ssm_b_re[i], ssm_b_im[i])
        bre = _block_diag_halves(bbr).astype(BF16)
        bim = _block_diag_halves(bbi).astype(BF16)
        cre = _block_diag_halves(jnp.transpose(ssm_c_re[i], (0, 2, 1))).astype(BF16)
        cim = _block_diag_halves(jnp.transpose(ssm_c_im[i], (0, 2, 1))).astype(BF16)
        s_out = _ssm(u, bre, bim, cre, cim, lbr.reshape(1, -1), lbi.reshape(1, -1),
                     ssm_d[i].reshape(1, -1), w_glu[i].astype(BF16), b_glu[i].reshape(1, -1),
                     batch=bsz, steps=SSM_STEPS, lane_chunk=SSM_LANE_CHUNK)

        g1, b1 = ln1_g[i].reshape(1, d), ln1_b[i].reshape(1, d)
        g2, b2 = ln2_g[i].reshape(1, d), ln2_b[i].reshape(1, d)
        wpg = ple_w_gate[i].astype(BF16)
        wpp = ple_w_proj[i].astype(BF16)
        merge_w = (w_gates, w_branch_attn[i].astype(BF16), w_branch_ssm[i].astype(BF16), w_out[i].astype(BF16))
        if i % 2 == 0:
            (x1,) = _merge(xt, a_out, s_out, *merge_w, g1, b1, tm=TM_MERGE)
            xt = _ffn(x1, pt[i], ffn_gu, ffn_dn, i // 2, wpg, wpp, g2, b2, tm=TM_FFN)
        else:
            rt = jnp.pad(moe_router[i // 2], ((0, 0), (0, LANES - N_EXPERTS)))
            rhi = rt.astype(BF16)
            rlo = (rt - rhi.astype(F32)).astype(BF16)
            x1, route = _merge(xt, a_out, s_out, *merge_w, g1, b1, (rhi, rlo), tm=TM_MERGE)
            te, n_used, src, dst, out_rows = _routing_tables(route, tm=TM_MOE, n_experts=N_EXPERTS)
            y = _moe_experts(x1, moe_gu, moe_dn, i // 2, te, n_used, src, dst, tm=TM_MOE, fw=FW_MOE,
                             out_rows=out_rows)
            xt = _moe_out(x1, route, pt[i], y, wpg, wpp, g2, b2, tm=TM_FFN)

    return xt.reshape(bsz, seq, d)
```

```python
import functools
import math
from typing import NamedTuple

import jax
import jax.numpy as jnp
import numpy as np
from jax import lax
from jax.experimental import pallas as pl
from jax.experimental.pallas import tpu as pltpu

F32 = jnp.float32
BF16 = jnp.bfloat16

D_MODEL = 1024
BATCH = 16
SEQ = 2048
DEPTH = 4
HEAD_DIM = 64
N_Q_HEADS = 8
N_KV_HEADS = 2
ATTN_WIDTH = N_Q_HEADS * HEAD_DIM
KV_WIDTH = N_KV_HEADS * HEAD_DIM
ATTN_BLOCK = 128
N_BUCKETS = 32
MAX_DISTANCE = 128
SSM_WIDTH = 512
SSM_GROUP_CH = 16
SSM_GROUPS = SSM_WIDTH // SSM_GROUP_CH
SSM_STATE = 64
D_FF = 2816
N_EXPERTS = 8
D_FF_EXPERT = 3584
PLE_DIM = 256
DEEPNORM_ALPHA = (2 * DEPTH) ** 0.25
LN_EPS = 1e-5
NEG_INF = -1e30
LOG2_E = math.log2(math.e)

LANES = 128
VMEM_LIMIT_BYTES = 56 * 1024 * 1024

TM_PROJ = 1024
ATTN_BLOCKS_PER_STEP = 8
TM_MERGE = 512
TM_FFN = 512
TM_MOE = 512
FW_MOE = 512
SLOT_TABLE_GROUPS = 16
SLOT_TABLE_BATCH = 8
SSM_STEPS = 32
SSM_PERM_STEPS = 16
SSM_LANE_CHUNK = 512
FFN_CHUNKS = ((0, 768), (768, 768), (1536, 768), (2304, 512))


def _cparams(sem):
    return pltpu.CompilerParams(dimension_semantics=sem, vmem_limit_bytes=VMEM_LIMIT_BYTES)


class _Layer(NamedTuple):
    arr: jax.Array
    idx: int


def _layer_spec(p, resident=False):
    tail = (0,) * (p.arr.ndim - 1)
    mode = {"pipeline_mode": pl.Buffered(1)} if resident else {}
    return pl.BlockSpec((None,) + p.arr.shape[1:], lambda *_: (p.idx,) + tail, **mode)


def _layer_norm(h, g, b):
    mu = jnp.mean(h, axis=-1, keepdims=True)
    c = h - mu
    var = jnp.mean(c * c, axis=-1, keepdims=True)
    return c * lax.rsqrt(var + LN_EPS) * g + b


def _sigmoid(v):
    return 1.0 / (1.0 + jnp.exp(-v))


def _gelu_tanh(v):
    return 0.5 * v * (1.0 + jnp.tanh(math.sqrt(2.0 / math.pi) * (v + 0.044715 * (v * v * v))))


def _inproj_body(x_ref, w_ref, q_ref, kv_ref, u_ref):
    xb = x_ref[...].astype(BF16)
    z = jnp.dot(xb, w_ref[...], preferred_element_type=F32)
    aw = q_ref.shape[1]
    kw = kv_ref.shape[1]
    q_ref[...] = (z[:, :aw] * (LOG2_E * HEAD_DIM ** -0.5)).astype(BF16)
    kv_ref[...] = z[:, aw:aw + kw].astype(BF16)
    u_ref[...] = z[:, aw + kw:].astype(BF16)


def _inproj(x, w, *, tm):
    t, d = x.shape
    n = w.arr.shape[2]
    aw, kw = ATTN_WIDTH, 2 * (N_KV_HEADS * 2 * LANES)
    uw = n - aw - kw
    return pl.pallas_call(
        _inproj_body,
        grid=(t // tm,),
        in_specs=[pl.BlockSpec((tm, d), lambda i: (i, 0)), _layer_spec(w)],
        out_specs=[pl.BlockSpec((tm, aw), lambda i: (i, 0)),
                   pl.BlockSpec((tm, kw), lambda i: (i, 0)),
                   pl.BlockSpec((tm, uw), lambda i: (i, 0))],
        out_shape=[jax.ShapeDtypeStruct((t, aw), BF16),
                   jax.ShapeDtypeStruct((t, kw), BF16),
                   jax.ShapeDtypeStruct((t, uw), BF16)],
        compiler_params=_cparams(("parallel",)),
        name="inproj",
    )(x, w.arr)


def _attn_body(sink_ref, q_ref, kvc_ref, kvp_ref, bias0_ref, bias_ref, o_ref, *, blk, nbs, layer):
    v0 = N_KV_HEADS * 2 * LANES
    group = N_Q_HEADS // N_KV_HEADS
    for j in range(nbs):
        rows = slice(j * blk, (j + 1) * blk)
        b_ref = bias0_ref if j == 0 else bias_ref
        for tile in range(N_Q_HEADS // 2):
            qt = q_ref[rows, tile * LANES:(tile + 1) * LANES]
            acc = None
            for par in range(2):
                h = 2 * tile + par
                kt = 2 * (h // group) + par
                ks = slice(kt * LANES, (kt + 1) * LANES)
                vs = slice(v0 + kt * LANES, v0 + (kt + 1) * LANES)
                if j == 0:
                    k = jnp.concatenate([kvp_ref[:, ks], kvc_ref[0:blk, ks]], axis=0)
                    v = jnp.concatenate([kvp_ref[:, vs], kvc_ref[0:blk, vs]], axis=0)
                else:
                    k = kvc_ref[(j - 1) * blk:(j + 1) * blk, ks]
                    v = kvc_ref[(j - 1) * blk:(j + 1) * blk, vs]
                s = lax.dot_general(qt, k, (((1,), (1,)), ((), ())), preferred_element_type=F32)
                s = s + b_ref[h]
                sink = sink_ref[layer, h]
                m = jnp.maximum(jnp.max(s, axis=-1, keepdims=True), sink)
                e = jnp.exp2(s - m)
                denom = jnp.sum(e, axis=-1, keepdims=True) + jnp.exp2(sink - m)
                o = jnp.dot(e.astype(BF16), v, preferred_element_type=F32) / denom
                acc = o if acc is None else acc + o
            o_ref[rows, tile * LANES:(tile + 1) * LANES] = acc.astype(BF16)


def _attention(q, kv, sinks, bias, *, seq, batch):
    t = q.shape[0]
    blk = ATTN_BLOCK
    nbs = ATTN_BLOCKS_PER_STEP
    ns = seq // (blk * nbs)
    aw, kw = q.shape[1], kv.shape[1]
    table = (None, N_Q_HEADS, blk, 2 * blk)
    return pl.pallas_call(
        functools.partial(_attn_body, blk=blk, nbs=nbs, layer=sinks.idx),
        grid=(batch, ns),
        in_specs=[pl.BlockSpec(memory_space=pltpu.SMEM),
                  pl.BlockSpec((blk * nbs, aw), lambda b, n: (b * ns + n, 0)),
                  pl.BlockSpec((blk * nbs, kw), lambda b, n: (b * ns + n, 0)),
                  pl.BlockSpec((blk, kw), lambda b, n: ((b * ns + n) * nbs - jnp.minimum(n, 1), 0)),
                  pl.BlockSpec(table, lambda b, n: (jnp.minimum(n, 1), 0, 0, 0)),
                  pl.BlockSpec(table, lambda b, n: (1, 0, 0, 0))],
        out_specs=pl.BlockSpec((blk * nbs, aw), lambda b, n: (b * ns + n, 0)),
        out_shape=jax.ShapeDtypeStruct((t, aw), BF16),
        compiler_params=_cparams(("parallel", "parallel")),
        name="swa_attention",
    )(sinks.arr, q, kv, kv, bias, bias)


def _per_head_kv_columns(w):
    depth, d, _ = w.shape
    wh = w.reshape(depth, d, N_KV_HEADS, 1, 1, HEAD_DIM)
    half = jnp.arange(2)[None, None, None, None, :, None]
    par = jnp.arange(2)[None, None, None, :, None, None]
    tiles = jnp.where(half == par, wh, 0.0)
    return tiles.reshape(depth, d, N_KV_HEADS * 2 * LANES)


def _attn_bias_table(rel_bias):
    blk = ATTN_BLOCK
    d = np.arange(blk, dtype=np.int32)
    max_exact = N_BUCKETS // 2
    d_f = np.maximum(d, 1).astype(np.float32)
    large = max_exact + (np.log(d_f / np.float32(max_exact)) / np.float32(math.log(MAX_DISTANCE / max_exact))
                         * np.float32(N_BUCKETS - max_exact)).astype(np.int32)
    large = np.minimum(large, N_BUCKETS - 1)
    bucket = np.where(d < max_exact, d, large)
    vals = jnp.transpose(rel_bias.astype(F32)[bucket], (1, 0))
    h = vals.shape[0]
    neg = jnp.full((h, blk), NEG_INF, F32)
    strip = jnp.concatenate([neg, vals[:, ::-1], neg], axis=1)
    rows = [strip[:, blk - 1 - i:blk - 1 - i + 2 * blk] for i in range(blk)]
    table = jnp.stack(rows, axis=1) * LOG2_E
    first = jnp.where(np.arange(2 * blk)[None, None, :] < blk, NEG_INF, table)
    return jnp.stack([first, table], axis=0)


def _ssm_param_body(lre_ref, lim_ref, ldt_ref, bre_ref, bim_ref, lbr_ref, lbi_ref, bbr_ref, bbi_ref):
    lre, lim = lre_ref[...], lim_ref[...]
    dt = jnp.exp(ldt_ref[...])
    mag = jnp.exp(lre * dt)
    lbr = mag * jnp.cos(lim * dt)
    lbi = mag * jnp.sin(lim * dt)
    lbr_ref[...] = lbr
    lbi_ref[...] = lbi
    nr, ni = lbr - 1.0, lbi
    inv = 1.0 / (lre * lre + lim * lim)
    cr = (nr * lre + ni * lim) * inv
    ci = (ni * lre - nr * lim) * inv
    br, bi = bre_ref[...], bim_ref[...]
    bbr_ref[...] = cr * br - ci * bi
    bbi_ref[...] = cr * bi + ci * br


def _ssm_params(lam_re, lam_im, log_dt, b_re, b_im):
    depth, g, p = lam_re.shape
    h = b_re.shape[3]
    n = depth * g
    ldt = jnp.broadcast_to(log_dt.reshape(n, 1, 1), (n, 1, p))
    brt = jnp.transpose(b_re, (0, 1, 3, 2)).reshape(n, h, p)
    bit = jnp.transpose(b_im, (0, 1, 3, 2)).reshape(n, h, p)
    sd = jax.ShapeDtypeStruct
    lbr, lbi, bbr, bbi = pl.pallas_call(
        _ssm_param_body,
        out_shape=[sd((n, 1, p), F32), sd((n, 1, p), F32), sd((n, h, p), F32), sd((n, h, p), F32)],
        name="ssm_params",
    )(lam_re.reshape(n, 1, p), lam_im.reshape(n, 1, p), ldt, brt, bit)
    return (lbr.reshape(depth, 1, g * p), lbi.reshape(depth, 1, g * p),
            bbr.reshape(depth, g, h, p), bbi.reshape(depth, g, h, p))


def _block_diag_halves(w):
    depth, g, a, b = w.shape
    hg = g // 2
    wh = w.reshape(depth, 2, hg, a, b)
    eye = jnp.eye(hg, dtype=bool)[None, None, :, None, :, None]
    full = jnp.where(eye, wh[:, :, :, :, None, :], 0.0)
    return full.reshape(depth, 2, hg * a, hg * b)


def _ssm_body(u_ref, perm_ref, permt_ref, bre_ref, bim_ref, cre_ref, cim_ref, lr_ref, li_ref, d_ref,
              wglu_ref, bglu_ref, o_ref, xr0, xi0, ub0, xr1, xi1, ub1, sr, si, *, steps, batch, lane_chunk):
    s = pl.program_id(0)

    @pl.when(s == 0)
    def _():
        for ref in (xr0, xi0, ub0, xr1, xi1, ub1, sr, si):
            ref[...] = jnp.zeros_like(ref)

    sub = SSM_PERM_STEPS
    rows_g = sub * batch

    def stage(fill, scan):
        xr_f, xi_f, ub_f = fill
        xr, xi, ub = scan
        parts = []
        for g in range(steps // sub):
            bm = jnp.concatenate([u_ref[b, g * sub:(g + 1) * sub, :] for b in range(batch)], axis=0)
            parts.append(jnp.dot(perm_ref[...], bm, preferred_element_type=F32).astype(BF16))
        u_new = jnp.concatenate(parts, axis=0)
        ub_f[...] = u_new
        hw = u_new.shape[1] // 2
        hs = xr.shape[1] // 2
        for hf in range(2):
            uh = u_new[:, hf * hw:(hf + 1) * hw]
            xr_f[:, hf * hs:(hf + 1) * hs] = jnp.dot(uh, bre_ref[hf], preferred_element_type=F32)
            xi_f[:, hf * hs:(hf + 1) * hs] = jnp.dot(uh, bim_ref[hf], preferred_element_type=F32)
        for c in range(xr.shape[1] // lane_chunk):
            cs = slice(c * lane_chunk, (c + 1) * lane_chunk)
            ar = lr_ref[:, cs]
            ai = li_ref[:, cs]
            pr, pi = sr[:, cs], si[:, cs]
            for t in range(steps):
                rows = slice(t * batch, (t + 1) * batch)
                nr = ar * pr - ai * pi + xr[rows, cs]
                ni = ar * pi + ai * pr + xi[rows, cs]
                xr[rows, cs] = nr
                xi[rows, cs] = ni
                pr, pi = nr, ni
            sr[:, cs] = pr
            si[:, cs] = pi
        ys = []
        for hf in range(2):
            xrb = xr[:, hf * hs:(hf + 1) * hs].astype(BF16)
            xib = xi[:, hf * hs:(hf + 1) * hs].astype(BF16)
            ys.append(jnp.dot(xrb, cre_ref[hf], preferred_element_type=F32)
                      - jnp.dot(xib, cim_ref[hf], preferred_element_type=F32))
        y = jnp.concatenate(ys, axis=1) + d_ref[...] * ub[...].astype(F32)
        y = _gelu_tanh(y)
        z = jnp.dot(y.astype(BF16), wglu_ref[...], preferred_element_type=F32) + bglu_ref[...]
        out = (y * _sigmoid(z)).astype(BF16)
        for g in range(steps // sub):
            bm = jnp.dot(permt_ref[...], out[g * rows_g:(g + 1) * rows_g], preferred_element_type=F32).astype(BF16)
            for b in range(batch):
                o_ref[b, g * sub:(g + 1) * sub, :] = bm[b * sub:(b + 1) * sub]

    @pl.when(s % 2 == 0)
    def _():
        stage((xr0, xi0, ub0), (xr1, xi1, ub1))

    @pl.when(s % 2 == 1)
    def _():
        stage((xr1, xi1, ub1), (xr0, xi0, ub0))


def _time_major_permutation(batch, sub):
    n = batch * sub
    r = np.arange(n)
    p = np.zeros((n, n), np.float32)
    p[r, (r % batch) * sub + r // batch] = 1.0
    return p


def _ssm(u, bre, bim, cre, cim, lbr, lbi, d, wglu, bglu, *, batch, steps, lane_chunk):
    t, w = u.shape
    seq = t // batch
    n = seq // steps
    ns = lbr.arr.shape[2]
    rows = steps * batch
    perm = _time_major_permutation(batch, SSM_PERM_STEPS)
    const2 = lambda i: (0, 0)
    params = (bre, bim, cre, cim, lbr, lbi, d, wglu, bglu)
    buf = [pltpu.VMEM((rows, ns), F32), pltpu.VMEM((rows, ns), F32), pltpu.VMEM((rows, w), BF16)]
    out = pl.pallas_call(
        functools.partial(_ssm_body, steps=steps, batch=batch, lane_chunk=lane_chunk),
        grid=(n + 1,),
        in_specs=[pl.BlockSpec((batch, steps, w), lambda i: (0, jnp.minimum(i, n - 1), 0)),
                  pl.BlockSpec(perm.shape, const2), pl.BlockSpec(perm.shape, const2)]
        + [_layer_spec(p) for p in params],
        out_specs=pl.BlockSpec((batch, steps, w), lambda i: (0, jnp.maximum(i - 1, 0), 0)),
        out_shape=jax.ShapeDtypeStruct((batch, seq, w), BF16),
        scratch_shapes=buf + buf + [pltpu.VMEM((batch, ns), F32), pltpu.VMEM((batch, ns), F32)],
        compiler_params=_cparams(("arbitrary",)),
        name="s5_scan",
    )(u.reshape(batch, seq, w), jnp.asarray(perm, BF16), jnp.asarray(perm.T, BF16), *[p.arr for p in params])
    return out.reshape(t, w)


def _split_bf16(v):
    hi = v.astype(BF16)
    lo = (v - hi.astype(F32)).astype(BF16)
    return hi, lo


def _merge_body(x_ref, a_ref, s_ref, wg_ref, wba_ref, wbs_ref, wo_ref, g_ref, b_ref, *rest, with_router):
    if with_router:
        rhi_ref, rlo_ref, x1_ref, route_ref = rest
    else:
        (x1_ref,) = rest
    x = x_ref[...]
    d = x.shape[1]
    gates = jnp.dot(x.astype(BF16), wg_ref[...], preferred_element_type=F32)
    pa = jnp.dot(a_ref[...], wba_ref[...], preferred_element_type=F32)
    ps = jnp.dot(s_ref[...], wbs_ref[...], preferred_element_type=F32)
    merged = _sigmoid(gates[:, :d]) * pa + _sigmoid(gates[:, d:]) * ps
    y = jnp.dot(merged.astype(BF16), wo_ref[...], preferred_element_type=F32)
    x1 = _layer_norm(DEEPNORM_ALPHA * x + y, g_ref[...], b_ref[...])
    x1_ref[...] = x1
    if with_router:
        hi, lo = _split_bf16(x1)
        rhi = rhi_ref[...]
        logits = (jnp.dot(hi, rhi, preferred_element_type=F32)
                  + jnp.dot(lo, rhi, preferred_element_type=F32)
                  + jnp.dot(hi, rlo_ref[...], preferred_element_type=F32))
        lane = lax.broadcasted_iota(jnp.int32, logits.shape, 1).astype(F32)
        big = float(LANES)
        l1 = jnp.where(lane < N_EXPERTS, logits, -jnp.inf)
        m1 = jnp.max(l1, axis=-1, keepdims=True)
        i1 = jnp.min(jnp.where(l1 == m1, lane, big), axis=-1, keepdims=True)
        l2 = jnp.where(lane == i1, -jnp.inf, l1)
        m2 = jnp.max(l2, axis=-1, keepdims=True)
        i2 = jnp.min(jnp.where(l2 == m2, lane, big), axis=-1, keepdims=True)
        ed = jnp.exp(m2 - m1)
        g1 = 1.0 / (1.0 + ed)
        g2 = ed / (1.0 + ed)
        route = jnp.where(lane == 0.0, i1,
                          jnp.where(lane == 1.0, i2,
                                    jnp.where(lane == 2.0, g1, jnp.where(lane == 3.0, g2, 0.0))))
        route_ref[...] = route


def _merge(x, a, s, wg, wba, wbs, wo, g, b, router=None, *, tm):
    t, d = x.shape
    row = lambda i: (i, 0)
    params = [wg, wba, wbs, wo, g, b] + list(router or ())
    in_specs = [pl.BlockSpec((tm, d), row), pl.BlockSpec((tm, a.shape[1]), row), pl.BlockSpec((tm, s.shape[1]), row)]
    in_specs += [_layer_spec(p) for p in params]
    args = [x, a, s] + [p.arr for p in params]
    out_specs = [pl.BlockSpec((tm, d), row)]
    out_shape = [jax.ShapeDtypeStruct((t, d), F32)]
    if router is not None:
        out_specs.append(pl.BlockSpec((tm, LANES), row))
        out_shape.append(jax.ShapeDtypeStruct((t, LANES), F32))
    return pl.pallas_call(
        functools.partial(_merge_body, with_router=router is not None),
        grid=(t // tm,),
        in_specs=in_specs, out_specs=out_specs, out_shape=out_shape,
        compiler_params=_cparams(("parallel",)),
        name="merge_ln1",
    )(*args)


def _ple(xb, p_ref, wpg_ref, wpp_ref):
    gate = _sigmoid(jnp.dot(xb, wpg_ref[...], preferred_element_type=F32))
    return gate * jnp.dot(p_ref[...].astype(BF16), wpp_ref[...], preferred_element_type=F32)


def _ffn_body(x_ref, p_ref, wg_ref, wu_ref, wd_ref, wpg_ref, wpp_ref, g_ref, b_ref, o_ref, acc):
    x1 = x_ref[...]
    xb = x1.astype(BF16)
    for c, (c0, cw) in enumerate(FFN_CHUNKS):
        gt = jnp.dot(xb, wg_ref[:, c0:c0 + cw], preferred_element_type=F32)
        up = jnp.dot(xb, wu_ref[:, c0:c0 + cw], preferred_element_type=F32)
        act = (gt * _sigmoid(gt) * up).astype(BF16)
        contrib = jnp.dot(act, wd_ref[c0:c0 + cw, :], preferred_element_type=F32)
        if c == 0:
            acc[...] = contrib
        else:
            acc[...] += contrib
    h = DEEPNORM_ALPHA * x1 + acc[...] + _ple(xb, p_ref, wpg_ref, wpp_ref)
    o_ref[...] = _layer_norm(h, g_ref[...], b_ref[...])


def _resident(shape, index_map):
    return pl.BlockSpec(shape, index_map, pipeline_mode=pl.Buffered(1))


def _ple_rows_spec(p, tm):
    return pl.BlockSpec((None, tm, p.arr.shape[2]), lambda i: (p.idx, i, 0))


def _ffn(x1, p, w_gu, w_down, li, wpg, wpp, g, b, *, tm):
    t, d = x1.shape
    ff = w_down.shape[1]
    row = lambda i: (i, 0)
    return pl.pallas_call(
        _ffn_body,
        grid=(t // tm,),
        in_specs=[pl.BlockSpec((tm, d), row), _ple_rows_spec(p, tm),
                  _resident((None, d, ff), lambda i: (li, 0, 0)),
                  _resident((None, d, ff), lambda i: (li, 0, 1)),
                  _resident((None, ff, d), lambda i: (li, 0, 0)),
                  _layer_spec(wpg, resident=True), _layer_spec(wpp, resident=True),
                  _layer_spec(g), _layer_spec(b)],
        out_specs=pl.BlockSpec((tm, d), row),
        out_shape=jax.ShapeDtypeStruct((t, d), F32),
        scratch_shapes=[pltpu.VMEM((tm, d), F32)],
        compiler_params=_cparams(("parallel",)),
        name="ffn_ln2",
    )(x1, p.arr, w_gu, w_gu, w_down, wpg.arr, wpp.arr, g.arr, b.arr)


def _moe_body(te_ref, nu_ref, src_ref, dst_ref, x_hbm, wgu_ref, wd_ref, y_hbm,
              xg, xb, acc, ybuf, gsem, ssem, *, tm, fw, spare_row):
    i = pl.program_id(0)
    nu = nu_ref[0]
    slot = i % 2
    other = 1 - slot
    ffe = wd_ref.shape[0]
    nc = ffe // fw
    per_chunk = tm // nc

    def gather(tile, r, buf):
        row = src_ref[tile * tm + r]
        pltpu.make_async_copy(x_hbm.at[pl.ds(row, 1), :], xg.at[buf, pl.ds(r, 1), :], gsem.at[buf]).start()

    def scatter(row, r, buf):
        pltpu.make_async_copy(ybuf.at[buf, pl.ds(r, 1), :], y_hbm.at[pl.ds(row, 1), :], ssem.at[buf]).start()

    def prev_tile_dst(r):
        return jnp.where(i > 0, dst_ref[jnp.maximum(i - 1, 0) * tm + r], spare_row + r)

    @pl.when(i == 0)
    def _():
        ybuf[...] = jnp.zeros_like(ybuf)

        def one(r, c):
            gather(0, r, 0)
            return c
        lax.fori_loop(0, tm, one, 0)

    @pl.when(i <= nu)
    def _():
        pltpu.make_async_copy(x_hbm.at[pl.ds(0, tm), :], xg.at[slot], gsem.at[slot]).wait()

    @pl.when(i < nu)
    def _():
        xb[...] = xg[slot].astype(BF16)
        x = xb[...]
        for c in range(nc):
            c0 = c * fw
            gt = jnp.dot(x, wgu_ref[:, c0:c0 + fw], preferred_element_type=F32)
            up = jnp.dot(x, wgu_ref[:, ffe + c0:ffe + c0 + fw], preferred_element_type=F32)
            act = (gt * _sigmoid(gt) * up).astype(BF16)
            part = jnp.dot(act, wd_ref[c0:c0 + fw, :], preferred_element_type=F32)
            if c == 0:
                acc[...] = part
            else:
                acc[...] += part
            r1 = tm if c == nc - 1 else (c + 1) * per_chunk
            for r in range(c * per_chunk, r1):
                gather(i + 1, r, other)
                scatter(prev_tile_dst(r), r, other)

    @pl.when(i == nu)
    def _():
        def one(r, c):
            scatter(dst_ref[(i - 1) * tm + r], r, other)
            return c
        lax.fori_loop(0, tm, one, 0)

    @pl.when((i >= 1) & (i <= nu + 1))
    def _():
        pltpu.make_async_copy(ybuf.at[slot], y_hbm.at[pl.ds(0, tm), :], ssem.at[slot]).wait()

    @pl.when(i < nu)
    def _():
        ybuf[slot] = acc[...]


def _moe_experts(x1, w_gu, w_down, li, tile_expert, n_used, src, dst, *, tm, fw, out_rows):
    t, d = x1.shape
    ffe = w_down.shape[2]
    nt = src.shape[0] // tm
    return pl.pallas_call(
        functools.partial(_moe_body, tm=tm, fw=fw, spare_row=out_rows - tm),
        grid_spec=pltpu.PrefetchScalarGridSpec(
            num_scalar_prefetch=4,
            grid=(nt,),
            in_specs=[pl.BlockSpec(memory_space=pl.ANY),
                      _resident((None, None, d, 2 * ffe), lambda i, te, nu, s, ds: (li, te[i], 0, 0)),
                      _resident((None, None, ffe, d), lambda i, te, nu, s, ds: (li, te[i], 0, 0))],
            out_specs=pl.BlockSpec(memory_space=pl.ANY),
            scratch_shapes=[pltpu.VMEM((2, tm, d), F32), pltpu.VMEM((tm, d), BF16), pltpu.VMEM((tm, d), F32),
                            pltpu.VMEM((2, tm, d), F32),
                            pltpu.SemaphoreType.DMA((2,)), pltpu.SemaphoreType.DMA((2,))]),
        out_shape=jax.ShapeDtypeStruct((out_rows, d), F32),
        compiler_params=_cparams(("arbitrary",)),
        name="moe_experts",
    )(tile_expert, n_used, src, dst, x1, w_gu, w_down)


def _moe_out_body(x_ref, route_ref, y0_ref, y1_ref, p_ref, wpg_ref, wpp_ref, g_ref, b_ref, o_ref):
    x1 = x_ref[...]
    route = route_ref[...]
    ffn = route[:, 2:3] * y0_ref[...] + route[:, 3:4] * y1_ref[...]
    h = DEEPNORM_ALPHA * x1 + ffn + _ple(x1.astype(BF16), p_ref, wpg_ref, wpp_ref)
    o_ref[...] = _layer_norm(h, g_ref[...], b_ref[...])


def _moe_out(x1, route, p, y, wpg, wpp, g, b, *, tm):
    t, d = x1.shape
    nt = t // tm
    row = lambda i: (i, 0)
    return pl.pallas_call(
        _moe_out_body,
        grid=(nt,),
        in_specs=[pl.BlockSpec((tm, d), row), pl.BlockSpec((tm, LANES), row),
                  pl.BlockSpec((tm, d), row), pl.BlockSpec((tm, d), lambda i: (nt + i, 0)),
                  _ple_rows_spec(p, tm),
                  _layer_spec(wpg), _layer_spec(wpp), _layer_spec(g), _layer_spec(b)],
        out_specs=pl.BlockSpec((tm, d), row),
        out_shape=jax.ShapeDtypeStruct((t, d), F32),
        compiler_params=_cparams(("parallel",)),
        name="moe_combine_ln2",
    )(x1, route, y, y, p.arr, wpg.arr, wpp.arr, g.arr, b.arr)


def _slot_table_body(pos_ref, init_hbm, dst_ref, sem, *, t, groups):
    @pl.when(pl.program_id(0) == 0)
    def _():
        copy = pltpu.make_async_copy(init_hbm, dst_ref, sem)
        copy.start()
        copy.wait()

    batch = SLOT_TABLE_BATCH
    base = pl.program_id(0) * (t // groups)

    def put(kb, c):
        tok0 = base + kb * batch
        ps = [pos_ref[2 * tok0 + j] for j in range(2 * batch)]
        for j, p in enumerate(ps):
            dst_ref[p] = (j % 2) * t + tok0 + j // 2
        return c
    lax.fori_loop(0, t // groups // batch, put, 0)


def _slot_table(pos, *, t, tm, r):
    groups = SLOT_TABLE_GROUPS
    assert t % (groups * SLOT_TABLE_BATCH) == 0
    init = 2 * t + jnp.arange(r, dtype=jnp.int32) % tm
    smem = pl.BlockSpec(memory_space=pltpu.SMEM)
    return pl.pallas_call(
        functools.partial(_slot_table_body, t=t, groups=groups),
        grid=(groups,),
        in_specs=[smem, pl.BlockSpec(memory_space=pl.ANY)], out_specs=smem,
        out_shape=jax.ShapeDtypeStruct((r,), jnp.int32),
        scratch_shapes=[pltpu.SemaphoreType.DMA],
        compiler_params=_cparams(("arbitrary",)),
        name="moe_slot_table",
    )(pos, init)


def _routing_tables(route, *, tm, n_experts):
    t = route.shape[0]
    eids = route[:, :2].astype(jnp.int32).reshape(-1)
    onehot = (eids[:, None] == jnp.arange(n_experts, dtype=jnp.int32)[None, :]).astype(jnp.int32)
    csum = jnp.cumsum(onehot, axis=0)
    rank = jnp.sum((csum - onehot) * onehot, axis=1)
    cnt = csum[-1]
    ntile = (cnt + tm - 1) // tm
    tile_end = jnp.cumsum(ntile)
    row_off = (tile_end - ntile) * tm
    pos = jnp.sum(onehot * row_off[None, :], axis=1) + rank
    r = 2 * t + (n_experts + 2) * tm
    dst = _slot_table(pos.astype(jnp.int32), t=t, tm=tm, r=r)
    src = jnp.where(dst < 2 * t, dst % t, 0)
    n_used = tile_end[-1:]
    tiles = jnp.arange(r // tm, dtype=jnp.int32)
    te = jnp.sum((tiles[:, None] >= tile_end[None, :]).astype(jnp.int32), axis=1)
    last = jnp.sum((n_used - 1 >= tile_end).astype(jnp.int32))
    te = jnp.where(tiles < n_used, te, last).astype(jnp.int32)
    out_rows = 2 * t + tm
    return te, n_used.astype(jnp.int32), src.astype(jnp.int32), dst.astype(jnp.int32), out_rows


def kernel(x, p, rel_bias, w_in, attn_sinks, ssm_lambda_re, ssm_lambda_im, ssm_log_dt, ssm_b_re, ssm_b_im,
           ssm_c_re, ssm_c_im, ssm_d, w_glu, b_glu, w_branch_attn, w_branch_ssm, w_out, ln1_g, ln1_b,
           ffn_w_gate_up, ffn_w_down, moe_router, moe_w_gate_up, moe_w_down, ple_w_proj, ple_w_gate,
           ln2_g, ln2_b):
    bsz, seq, d = x.shape
    t = bsz * seq
    depth = w_in.shape[0]

    xt = x.reshape(t, d)
    pt = p.reshape(depth, t, p.shape[-1])

    bias = _attn_bias_table(rel_bias)
    sinks = attn_sinks * LOG2_E
    ffn_gu = ffn_w_gate_up.astype(BF16)
    ffn_dn = ffn_w_down.astype(BF16)
    moe_gu = moe_w_gate_up.astype(BF16)
    moe_dn = moe_w_down.astype(BF16)

    aw, kw = ATTN_WIDTH, KV_WIDTH
    u0 = aw + 2 * kw
    w_proj = jnp.concatenate([w_in[:, :, :aw], _per_head_kv_columns(w_in[:, :, aw:aw + kw]),
                              _per_head_kv_columns(w_in[:, :, aw + kw:u0]), w_in[:, :, u0:u0 + SSM_WIDTH]],
                             axis=2).astype(BF16)
    w_gates = w_in[:, :, u0 + SSM_WIDTH:].astype(BF16)
    wba, wbs, wo = w_branch_attn.astype(BF16), w_branch_ssm.astype(BF16), w_out.astype(BF16)
    wpg, wpp = ple_w_gate.astype(BF16), ple_w_proj.astype(BF16)
    g1, b1 = ln1_g.reshape(depth, 1, d), ln1_b.reshape(depth, 1, d)
    g2, b2 = ln2_g.reshape(depth, 1, d), ln2_b.reshape(depth, 1, d)

    lbr, lbi, bbr, bbi = _ssm_params(ssm_lambda_re, ssm_lambda_im, ssm_log_dt, ssm_b_re, ssm_b_im)
    bre = _block_diag_halves(bbr).astype(BF16)
    bim = _block_diag_halves(bbi).astype(BF16)
    cre = _block_diag_halves(jnp.transpose(ssm_c_re, (0, 1, 3, 2))).astype(BF16)
    cim = _block_diag_halves(jnp.transpose(ssm_c_im, (0, 1, 3, 2))).astype(BF16)
    ssm_skip = ssm_d.reshape(depth, 1, -1)
    wglu, bglu = w_glu.astype(BF16), b_glu.reshape(depth, 1, -1)

    rt = jnp.pad(moe_router, ((0, 0), (0, 0), (0, LANES - N_EXPERTS)))
    rhi = rt.astype(BF16)
    rlo = (rt - rhi.astype(F32)).astype(BF16)

    for i in range(depth):
        at = lambda arr, idx=i: _Layer(arr, idx)
        q, kv, u = _inproj(xt, at(w_proj), tm=TM_PROJ)
        a_out = _attention(q, kv, at(sinks), bias, seq=seq, batch=bsz)
        s_out = _ssm(u, at(bre), at(bim), at(cre), at(cim), at(lbr), at(lbi), at(ssm_skip), at(wglu), at(bglu),
                     batch=bsz, steps=SSM_STEPS, lane_chunk=SSM_LANE_CHUNK)
        merge_w = (at(w_gates), at(wba), at(wbs), at(wo), at(g1), at(b1))
        if i % 2 == 0:
            (x1,) = _merge(xt, a_out, s_out, *merge_w, tm=TM_MERGE)
            xt = _ffn(x1, at(pt), ffn_gu, ffn_dn, i // 2, at(wpg), at(wpp), at(g2), at(b2), tm=TM_FFN)
        else:
            router = (_Layer(rhi, i // 2), _Layer(rlo, i // 2))
            x1, route = _merge(xt, a_out, s_out, *merge_w, router, tm=TM_MERGE)
            te, n_used, src, dst, out_rows = _routing_tables(route, tm=TM_MOE, n_experts=N_EXPERTS)
            y = _moe_experts(x1, moe_gu, moe_dn, i // 2, te, n_used, src, dst, tm=TM_MOE, fw=FW_MOE,
                             out_rows=out_rows)
            xt = _moe_out(x1, route, at(pt), y, at(wpg), at(wpp), at(g2), at(b2), tm=TM_FFN)

    return xt.reshape(bsz, seq, d)
```

```python
import functools
import math
from typing import NamedTuple

import jax
import jax.numpy as jnp
import numpy as np
from jax import lax
from jax.experimental import pallas as pl
from jax.experimental.pallas import tpu as pltpu

F32 = jnp.float32
BF16 = jnp.bfloat16

D_MODEL = 1024
BATCH = 16
SEQ = 2048
DEPTH = 4
HEAD_DIM = 64
N_Q_HEADS = 8
N_KV_HEADS = 2
ATTN_WIDTH = N_Q_HEADS * HEAD_DIM
KV_WIDTH = N_KV_HEADS * HEAD_DIM
ATTN_BLOCK = 128
N_BUCKETS = 32
MAX_DISTANCE = 128
SSM_WIDTH = 512
SSM_GROUP_CH = 16
SSM_GROUPS = SSM_WIDTH // SSM_GROUP_CH
SSM_STATE = 64
D_FF = 2816
N_EXPERTS = 8
D_FF_EXPERT = 3584
PLE_DIM = 256
DEEPNORM_ALPHA = (2 * DEPTH) ** 0.25
LN_EPS = 1e-5
NEG_INF = -1e30
LOG2_E = math.log2(math.e)

LANES = 128
VMEM_LIMIT_BYTES = 56 * 1024 * 1024

TM_PROJ = 1024
ATTN_BLOCKS_PER_STEP = 8
TM_MERGE = 512
MERGE_ROW_GROUPS = 2
TM_FFN = 512
TM_MOE = 512
FW_MOE = 512
SLOT_TABLE_GROUPS = 16
SLOT_TABLE_BATCH = 8
SSM_STEPS = 32
SSM_PERM_STEPS = 16
SSM_LANE_CHUNK = 512
FFN_CHUNKS = ((0, 768), (768, 768), (1536, 768), (2304, 512))


def _cparams(sem):
    return pltpu.CompilerParams(dimension_semantics=sem, vmem_limit_bytes=VMEM_LIMIT_BYTES)


class _Layer(NamedTuple):
    arr: jax.Array
    idx: int


def _layer_spec(p, resident=False):
    tail = (0,) * (p.arr.ndim - 1)
    mode = {"pipeline_mode": pl.Buffered(1)} if resident else {}
    return pl.BlockSpec((None,) + p.arr.shape[1:], lambda *_: (p.idx,) + tail, **mode)


def _layer_norm(h, g, b):
    mu = jnp.mean(h, axis=-1, keepdims=True)
    c = h - mu
    var = jnp.mean(c * c, axis=-1, keepdims=True)
    return c * lax.rsqrt(var + LN_EPS) * g + b


def _sigmoid(v):
    return 1.0 / (1.0 + jnp.exp(-v))


def _pack_bf16_pairs(v):
    c = v.shape[1] // 2
    lo = lax.bitcast_convert_type(v[:, :c].astype(BF16).astype(F32), jnp.uint32)
    hi = lax.bitcast_convert_type(v[:, c:].astype(BF16).astype(F32), jnp.uint32)
    return (lo >> 16) | (hi & jnp.uint32(0xFFFF0000))


def _unpack_bf16_pairs(w):
    lo = lax.bitcast_convert_type(w << 16, F32)
    hi = lax.bitcast_convert_type(w & jnp.uint32(0xFFFF0000), F32)
    return jnp.concatenate([lo, hi], axis=1)


def _gelu_tanh(v):
    return 0.5 * v * (1.0 + jnp.tanh(math.sqrt(2.0 / math.pi) * (v + 0.044715 * (v * v * v))))


def _inproj_body(x_ref, w_ref, q_ref, kv_ref, u_ref):
    xb = x_ref[...].astype(BF16)
    z = jnp.dot(xb, w_ref[...], preferred_element_type=F32)
    aw = q_ref.shape[1]
    kw = kv_ref.shape[1]
    q_ref[...] = (z[:, :aw] * (LOG2_E * HEAD_DIM ** -0.5)).astype(BF16)
    kv_ref[...] = z[:, aw:aw + kw].astype(BF16)
    u_ref[...] = z[:, aw + kw:].astype(BF16)


def _inproj(x, w, *, tm):
    t, d = x.shape
    n = w.arr.shape[2]
    aw, kw = ATTN_WIDTH, 2 * (N_KV_HEADS * 2 * LANES)
    uw = n - aw - kw
    return pl.pallas_call(
        _inproj_body,
        grid=(t // tm,),
        in_specs=[pl.BlockSpec((tm, d), lambda i: (i, 0)), _layer_spec(w)],
        out_specs=[pl.BlockSpec((tm, aw), lambda i: (i, 0)),
                   pl.BlockSpec((tm, kw), lambda i: (i, 0)),
                   pl.BlockSpec((tm, uw), lambda i: (i, 0))],
        out_shape=[jax.ShapeDtypeStruct((t, aw), BF16),
                   jax.ShapeDtypeStruct((t, kw), BF16),
                   jax.ShapeDtypeStruct((t, uw), BF16)],
        compiler_params=_cparams(("parallel",)),
        name="inproj",
    )(x, w.arr)


def _attn_body(sink_ref, q_ref, kvc_ref, kvp_ref, bias0_ref, bias_ref, o_ref, *, blk, nbs, layer):
    v0 = N_KV_HEADS * 2 * LANES
    group = N_Q_HEADS // N_KV_HEADS
    for j in range(nbs):
        rows = slice(j * blk, (j + 1) * blk)
        b_ref = bias0_ref if j == 0 else bias_ref
        for tile in range(N_Q_HEADS // 2):
            qt = q_ref[rows, tile * LANES:(tile + 1) * LANES]
            acc = None
            for par in range(2):
                h = 2 * tile + par
                kt = 2 * (h // group) + par
                ks = slice(kt * LANES, (kt + 1) * LANES)
                vs = slice(v0 + kt * LANES, v0 + (kt + 1) * LANES)
                if j == 0:
                    k = jnp.concatenate([kvp_ref[:, ks], kvc_ref[0:blk, ks]], axis=0)
                    v = jnp.concatenate([kvp_ref[:, vs], kvc_ref[0:blk, vs]], axis=0)
                else:
                    k = kvc_ref[(j - 1) * blk:(j + 1) * blk, ks]
                    v = kvc_ref[(j - 1) * blk:(j + 1) * blk, vs]
                s = lax.dot_general(qt, k, (((1,), (1,)), ((), ())), preferred_element_type=F32)
                s = s + b_ref[h]
                sink = sink_ref[layer, h]
                m = jnp.maximum(jnp.max(s, axis=-1, keepdims=True), sink)
                e = jnp.exp2(s - m)
                denom = jnp.sum(e, axis=-1, keepdims=True) + jnp.exp2(sink - m)
                o = jnp.dot(e.astype(BF16), v, preferred_element_type=F32) / denom
                acc = o if acc is None else acc + o
            o_ref[rows, tile * LANES:(tile + 1) * LANES] = acc.astype(BF16)


def _attention(q, kv, sinks, bias, *, seq, batch):
    t = q.shape[0]
    blk = ATTN_BLOCK
    nbs = ATTN_BLOCKS_PER_STEP
    ns = seq // (blk * nbs)
    aw, kw = q.shape[1], kv.shape[1]
    table = (None, N_Q_HEADS, blk, 2 * blk)
    return pl.pallas_call(
        functools.partial(_attn_body, blk=blk, nbs=nbs, layer=sinks.idx),
        grid=(batch, ns),
        in_specs=[pl.BlockSpec(memory_space=pltpu.SMEM),
                  pl.BlockSpec((blk * nbs, aw), lambda b, n: (b * ns + n, 0)),
                  pl.BlockSpec((blk * nbs, kw), lambda b, n: (b * ns + n, 0)),
                  pl.BlockSpec((blk, kw), lambda b, n: ((b * ns + n) * nbs - jnp.minimum(n, 1), 0)),
                  pl.BlockSpec(table, lambda b, n: (jnp.minimum(n, 1), 0, 0, 0)),
                  pl.BlockSpec(table, lambda b, n: (1, 0, 0, 0))],
        out_specs=pl.BlockSpec((blk * nbs, aw), lambda b, n: (b * ns + n, 0)),
        out_shape=jax.ShapeDtypeStruct((t, aw), BF16),
        compiler_params=_cparams(("parallel", "parallel")),
        name="swa_attention",
    )(sinks.arr, q, kv, kv, bias, bias)


def _per_head_kv_columns(w):
    depth, d, _ = w.shape
    wh = w.reshape(depth, d, N_KV_HEADS, 1, 1, HEAD_DIM)
    half = jnp.arange(2)[None, None, None, None, :, None]
    par = jnp.arange(2)[None, None, None, :, None, None]
    tiles = jnp.where(half == par, wh, 0.0)
    return tiles.reshape(depth, d, N_KV_HEADS * 2 * LANES)


def _attn_bias_table(rel_bias):
    blk = ATTN_BLOCK
    d = np.arange(blk, dtype=np.int32)
    max_exact = N_BUCKETS // 2
    d_f = np.maximum(d, 1).astype(np.float32)
    large = max_exact + (np.log(d_f / np.float32(max_exact)) / np.float32(math.log(MAX_DISTANCE / max_exact))
                         * np.float32(N_BUCKETS - max_exact)).astype(np.int32)
    large = np.minimum(large, N_BUCKETS - 1)
    bucket = np.where(d < max_exact, d, large)
    vals = jnp.transpose(rel_bias.astype(F32)[bucket], (1, 0))
    h = vals.shape[0]
    neg = jnp.full((h, blk), NEG_INF, F32)
    strip = jnp.concatenate([neg, vals[:, ::-1], neg], axis=1)
    rows = [strip[:, blk - 1 - i:blk - 1 - i + 2 * blk] for i in range(blk)]
    table = jnp.stack(rows, axis=1) * LOG2_E
    first = jnp.where(np.arange(2 * blk)[None, None, :] < blk, NEG_INF, table)
    return jnp.stack([first, table], axis=0)


def _ssm_param_body(lre_ref, lim_ref, ldt_ref, bre_ref, bim_ref, lbr_ref, lbi_ref, bbr_ref, bbi_ref):
    lre, lim = lre_ref[...], lim_ref[...]
    dt = jnp.exp(ldt_ref[...])
    mag = jnp.exp(lre * dt)
    lbr = mag * jnp.cos(lim * dt)
    lbi = mag * jnp.sin(lim * dt)
    lbr_ref[...] = lbr
    lbi_ref[...] = lbi
    nr, ni = lbr - 1.0, lbi
    inv = 1.0 / (lre * lre + lim * lim)
    cr = (nr * lre + ni * lim) * inv
    ci = (ni * lre - nr * lim) * inv
    br, bi = bre_ref[...], bim_ref[...]
    bbr_ref[...] = cr * br - ci * bi
    bbi_ref[...] = cr * bi + ci * br


def _ssm_params(lam_re, lam_im, log_dt, b_re, b_im):
    depth, g, p = lam_re.shape
    h = b_re.shape[3]
    n = depth * g
    ldt = jnp.broadcast_to(log_dt.reshape(n, 1, 1), (n, 1, p))
    brt = jnp.transpose(b_re, (0, 1, 3, 2)).reshape(n, h, p)
    bit = jnp.transpose(b_im, (0, 1, 3, 2)).reshape(n, h, p)
    sd = jax.ShapeDtypeStruct
    lbr, lbi, bbr, bbi = pl.pallas_call(
        _ssm_param_body,
        out_shape=[sd((n, 1, p), F32), sd((n, 1, p), F32), sd((n, h, p), F32), sd((n, h, p), F32)],
        name="ssm_params",
    )(lam_re.reshape(n, 1, p), lam_im.reshape(n, 1, p), ldt, brt, bit)
    return (lbr.reshape(depth, 1, g * p), lbi.reshape(depth, 1, g * p),
            bbr.reshape(depth, g, h, p), bbi.reshape(depth, g, h, p))


def _block_diag_halves(w):
    depth, g, a, b = w.shape
    hg = g // 2
    wh = w.reshape(depth, 2, hg, a, b)
    eye = jnp.eye(hg, dtype=bool)[None, None, :, None, :, None]
    full = jnp.where(eye, wh[:, :, :, :, None, :], 0.0)
    return full.reshape(depth, 2, hg * a, hg * b)


def _ssm_body(u_ref, perm_ref, permt_ref, bre_ref, bim_ref, cre_ref, cim_ref, lr_ref, li_ref, d_ref,
              wglu_ref, bglu_ref, o_ref, xr0, xi0, ub0, xr1, xi1, ub1, sr, si, *, steps, batch, lane_chunk):
    s = pl.program_id(0)

    @pl.when(s == 0)
    def _():
        for ref in (xr0, xi0, ub0, xr1, xi1, ub1, sr, si):
            ref[...] = jnp.zeros_like(ref)

    sub = SSM_PERM_STEPS
    rows_g = sub * batch

    def stage(fill, scan):
        xr_f, xi_f, ub_f = fill
        xr, xi, ub = scan
        parts = []
        for g in range(steps // sub):
            bm = jnp.concatenate([u_ref[b, g * sub:(g + 1) * sub, :] for b in range(batch)], axis=0)
            parts.append(jnp.dot(perm_ref[...], bm, preferred_element_type=F32).astype(BF16))
        u_new = jnp.concatenate(parts, axis=0)
        ub_f[...] = u_new
        hw = u_new.shape[1] // 2
        hs = xr.shape[1] // 2
        for hf in range(2):
            uh = u_new[:, hf * hw:(hf + 1) * hw]
            xr_f[:, hf * hs:(hf + 1) * hs] = jnp.dot(uh, bre_ref[hf], preferred_element_type=F32)
            xi_f[:, hf * hs:(hf + 1) * hs] = jnp.dot(uh, bim_ref[hf], preferred_element_type=F32)
        for c in range(xr.shape[1] // lane_chunk):
            cs = slice(c * lane_chunk, (c + 1) * lane_chunk)
            ar = lr_ref[:, cs]
            ai = li_ref[:, cs]
            pr, pi = sr[:, cs], si[:, cs]
            for t in range(steps):
                rows = slice(t * batch, (t + 1) * batch)
                nr = ar * pr - ai * pi + xr[rows, cs]
                ni = ar * pi + ai * pr + xi[rows, cs]
                xr[rows, cs] = nr
                xi[rows, cs] = ni
                pr, pi = nr, ni
            sr[:, cs] = pr
            si[:, cs] = pi
        ys = []
        for hf in range(2):
            xrb = xr[:, hf * hs:(hf + 1) * hs].astype(BF16)
            xib = xi[:, hf * hs:(hf + 1) * hs].astype(BF16)
            ys.append(jnp.dot(xrb, cre_ref[hf], preferred_element_type=F32)
                      - jnp.dot(xib, cim_ref[hf], preferred_element_type=F32))
        y = jnp.concatenate(ys, axis=1) + d_ref[...] * ub[...].astype(F32)
        y = _gelu_tanh(y)
        z = jnp.dot(y.astype(BF16), wglu_ref[...], preferred_element_type=F32) + bglu_ref[...]
        out = (y * _sigmoid(z)).astype(BF16)
        for g in range(steps // sub):
            bm = jnp.dot(permt_ref[...], out[g * rows_g:(g + 1) * rows_g], preferred_element_type=F32).astype(BF16)
            for b in range(batch):
                o_ref[b, g * sub:(g + 1) * sub, :] = bm[b * sub:(b + 1) * sub]

    @pl.when(s % 2 == 0)
    def _():
        stage((xr0, xi0, ub0), (xr1, xi1, ub1))

    @pl.when(s % 2 == 1)
    def _():
        stage((xr1, xi1, ub1), (xr0, xi0, ub0))


def _time_major_permutation(batch, sub):
    n = batch * sub
    r = np.arange(n)
    p = np.zeros((n, n), np.float32)
    p[r, (r % batch) * sub + r // batch] = 1.0
    return p


def _ssm(u, bre, bim, cre, cim, lbr, lbi, d, wglu, bglu, *, batch, steps, lane_chunk):
    t, w = u.shape
    seq = t // batch
    n = seq // steps
    ns = lbr.arr.shape[2]
    rows = steps * batch
    perm = _time_major_permutation(batch, SSM_PERM_STEPS)
    const2 = lambda i: (0, 0)
    params = (bre, bim, cre, cim, lbr, lbi, d, wglu, bglu)
    buf = [pltpu.VMEM((rows, ns), F32), pltpu.VMEM((rows, ns), F32), pltpu.VMEM((rows, w), BF16)]
    out = pl.pallas_call(
        functools.partial(_ssm_body, steps=steps, batch=batch, lane_chunk=lane_chunk),
        grid=(n + 1,),
        in_specs=[pl.BlockSpec((batch, steps, w), lambda i: (0, jnp.minimum(i, n - 1), 0)),
                  pl.BlockSpec(perm.shape, const2), pl.BlockSpec(perm.shape, const2)]
        + [_layer_spec(p) for p in params],
        out_specs=pl.BlockSpec((batch, steps, w), lambda i: (0, jnp.maximum(i - 1, 0), 0)),
        out_shape=jax.ShapeDtypeStruct((batch, seq, w), BF16),
        scratch_shapes=buf + buf + [pltpu.VMEM((batch, ns), F32), pltpu.VMEM((batch, ns), F32)],
        compiler_params=_cparams(("arbitrary",)),
        name="s5_scan",
    )(u.reshape(batch, seq, w), jnp.asarray(perm, BF16), jnp.asarray(perm.T, BF16), *[p.arr for p in params])
    return out.reshape(t, w)


def _split_bf16(v):
    hi = v.astype(BF16)
    lo = (v - hi.astype(F32)).astype(BF16)
    return hi, lo


def _merge_body(x_ref, a_ref, s_ref, wg_ref, wba_ref, wbs_ref, wo_ref, g_ref, b_ref, *rest, with_router):
    if with_router:
        rcat_ref, x1_ref, route_ref = rest
    else:
        (x1_ref,) = rest
    tm, d = x_ref.shape
    rg = tm // MERGE_ROW_GROUPS
    for grp in range(MERGE_ROW_GROUPS):
        rows = slice(grp * rg, (grp + 1) * rg)
        x = x_ref[rows, :]
        gates = jnp.dot(x.astype(BF16), wg_ref[...], preferred_element_type=F32)
        pa = jnp.dot(a_ref[rows, :], wba_ref[...], preferred_element_type=F32)
        ps = jnp.dot(s_ref[rows, :], wbs_ref[...], preferred_element_type=F32)
        merged = _sigmoid(gates[:, :d]) * pa + _sigmoid(gates[:, d:]) * ps
        y = jnp.dot(merged.astype(BF16), wo_ref[...], preferred_element_type=F32)
        x1 = _layer_norm(DEEPNORM_ALPHA * x + y, g_ref[...], b_ref[...])
        x1_ref[rows, :] = x1
        if with_router:
            hi, lo = _split_bf16(x1)
            both = jnp.dot(hi, rcat_ref[...], preferred_element_type=F32)
            logits = (both[:, :LANES] + jnp.dot(lo, rcat_ref[:, :LANES], preferred_element_type=F32)
                      + both[:, LANES:])
            lane = lax.broadcasted_iota(jnp.int32, logits.shape, 1).astype(F32)
            big = float(LANES)
            l1 = jnp.where(lane < N_EXPERTS, logits, -jnp.inf)
            m1 = jnp.max(l1, axis=-1, keepdims=True)
            i1 = jnp.min(jnp.where(l1 == m1, lane, big), axis=-1, keepdims=True)
            l2 = jnp.where(lane == i1, -jnp.inf, l1)
            m2 = jnp.max(l2, axis=-1, keepdims=True)
            i2 = jnp.min(jnp.where(l2 == m2, lane, big), axis=-1, keepdims=True)
            ed = jnp.exp(m2 - m1)
            g1 = 1.0 / (1.0 + ed)
            g2 = ed / (1.0 + ed)
            route = jnp.where(lane == 0.0, i1,
                              jnp.where(lane == 1.0, i2,
                                        jnp.where(lane == 2.0, g1, jnp.where(lane == 3.0, g2, 0.0))))
            route_ref[rows, :] = route


def _merge(x, a, s, wg, wba, wbs, wo, g, b, router=None, *, tm):
    t, d = x.shape
    row = lambda i: (i, 0)
    params = [wg, wba, wbs, wo, g, b] + list(router or ())
    in_specs = [pl.BlockSpec((tm, d), row), pl.BlockSpec((tm, a.shape[1]), row), pl.BlockSpec((tm, s.shape[1]), row)]
    in_specs += [_layer_spec(p) for p in params]
    args = [x, a, s] + [p.arr for p in params]
    out_specs = [pl.BlockSpec((tm, d), row)]
    out_shape = [jax.ShapeDtypeStruct((t, d), F32)]
    if router is not None:
        out_specs.append(pl.BlockSpec((tm, LANES), row))
        out_shape.append(jax.ShapeDtypeStruct((t, LANES), F32))
    return pl.pallas_call(
        functools.partial(_merge_body, with_router=router is not None),
        grid=(t // tm,),
        in_specs=in_specs, out_specs=out_specs, out_shape=out_shape,
        compiler_params=_cparams(("parallel",)),
        name="merge_ln1",
    )(*args)


def _ple(xb, p_ref, wpg_ref, wpp_ref):
    gate = _sigmoid(jnp.dot(xb, wpg_ref[...], preferred_element_type=F32))
    return gate * jnp.dot(p_ref[...].astype(BF16), wpp_ref[...], preferred_element_type=F32)


def _ffn_body(x_ref, p_ref, wg_ref, wu_ref, wd_ref, wpg_ref, wpp_ref, g_ref, b_ref, o_ref, acc):
    x1 = x_ref[...]
    xb = x1.astype(BF16)
    for c, (c0, cw) in enumerate(FFN_CHUNKS):
        gt = jnp.dot(xb, wg_ref[:, c0:c0 + cw], preferred_element_type=F32)
        up = jnp.dot(xb, wu_ref[:, c0:c0 + cw], preferred_element_type=F32)
        act = (gt * _sigmoid(gt) * up).astype(BF16)
        contrib = jnp.dot(act, wd_ref[c0:c0 + cw, :], preferred_element_type=F32)
        if c == 0:
            acc[...] = contrib
        else:
            acc[...] += contrib
    h = DEEPNORM_ALPHA * x1 + acc[...] + _ple(xb, p_ref, wpg_ref, wpp_ref)
    o_ref[...] = _layer_norm(h, g_ref[...], b_ref[...])


def _resident(shape, index_map):
    return pl.BlockSpec(shape, index_map, pipeline_mode=pl.Buffered(1))


def _ple_rows_spec(p, tm):
    return pl.BlockSpec((None, tm, p.arr.shape[2]), lambda i: (p.idx, i, 0))


def _ffn(x1, p, w_gu, w_down, li, wpg, wpp, g, b, *, tm):
    t, d = x1.shape
    ff = w_down.shape[1]
    row = lambda i: (i, 0)
    return pl.pallas_call(
        _ffn_body,
        grid=(t // tm,),
        in_specs=[pl.BlockSpec((tm, d), row), _ple_rows_spec(p, tm),
                  _resident((None, d, ff), lambda i: (li, 0, 0)),
                  _resident((None, d, ff), lambda i: (li, 0, 1)),
                  _resident((None, ff, d), lambda i: (li, 0, 0)),
                  _layer_spec(wpg, resident=True), _layer_spec(wpp, resident=True),
                  _layer_spec(g), _layer_spec(b)],
        out_specs=pl.BlockSpec((tm, d), row),
        out_shape=jax.ShapeDtypeStruct((t, d), F32),
        scratch_shapes=[pltpu.VMEM((tm, d), F32)],
        compiler_params=_cparams(("parallel",)),
        name="ffn_ln2",
    )(x1, p.arr, w_gu, w_gu, w_down, wpg.arr, wpp.arr, g.arr, b.arr)


def _moe_body(te_ref, nu_ref, src_ref, dst_ref, x_hbm, wgu_ref, wd_ref, y_hbm,
              xg, xb, acc, ybuf, gsem, ssem, *, tm, fw, spare_row):
    i = pl.program_id(0)
    nu = nu_ref[0]
    slot = i % 2
    other = 1 - slot
    ffe = wd_ref.shape[0]
    nc = ffe // fw
    per_chunk = tm // nc

    def gather(tile, r, buf):
        row = src_ref[tile * tm + r]
        pltpu.make_async_copy(x_hbm.at[pl.ds(row, 1), :], xg.at[buf, pl.ds(r, 1), :], gsem.at[buf]).start()

    def scatter(row, r, buf):
        pltpu.make_async_copy(ybuf.at[buf, pl.ds(r, 1), :], y_hbm.at[pl.ds(row, 1), :], ssem.at[buf]).start()

    def prev_tile_dst(r):
        return jnp.where(i > 0, dst_ref[jnp.maximum(i - 1, 0) * tm + r], spare_row + r)

    @pl.when(i == 0)
    def _():
        ybuf[...] = jnp.zeros_like(ybuf)

        def one(r, c):
            gather(0, r, 0)
            return c
        lax.fori_loop(0, tm, one, 0)

    @pl.when(i <= nu)
    def _():
        pltpu.make_async_copy(x_hbm.at[pl.ds(0, tm), :], xg.at[slot], gsem.at[slot]).wait()

    @pl.when(i < nu)
    def _():
        xb[...] = xg[slot].astype(BF16)
        x = xb[...]
        for c in range(nc):
            c0 = c * fw
            gt = jnp.dot(x, wgu_ref[:, c0:c0 + fw], preferred_element_type=F32)
            up = jnp.dot(x, wgu_ref[:, ffe + c0:ffe + c0 + fw], preferred_element_type=F32)
            act = (gt * _sigmoid(gt) * up).astype(BF16)
            part = jnp.dot(act, wd_ref[c0:c0 + fw, :], preferred_element_type=F32)
            if c == 0:
                acc[...] = part
            else:
                acc[...] += part
            r1 = tm if c == nc - 1 else (c + 1) * per_chunk
            for r in range(c * per_chunk, r1):
                gather(i + 1, r, other)
                scatter(prev_tile_dst(r), r, other)

    @pl.when(i == nu)
    def _():
        def one(r, c):
            scatter(dst_ref[(i - 1) * tm + r], r, other)
            return c
        lax.fori_loop(0, tm, one, 0)

    @pl.when((i >= 1) & (i <= nu + 1))
    def _():
        pltpu.make_async_copy(ybuf.at[slot], y_hbm.at[pl.ds(0, tm), :], ssem.at[slot]).wait()

    @pl.when(i < nu)
    def _():
        ybuf[slot] = _pack_bf16_pairs(acc[...])


def _moe_experts(x1, w_gu, w_down, li, tile_expert, n_used, src, dst, *, tm, fw, out_rows):
    t, d = x1.shape
    ffe = w_down.shape[2]
    nt = src.shape[0] // tm
    return pl.pallas_call(
        functools.partial(_moe_body, tm=tm, fw=fw, spare_row=out_rows - tm),
        grid_spec=pltpu.PrefetchScalarGridSpec(
            num_scalar_prefetch=4,
            grid=(nt,),
            in_specs=[pl.BlockSpec(memory_space=pl.ANY),
                      _resident((None, None, d, 2 * ffe), lambda i, te, nu, s, ds: (li, te[i], 0, 0)),
                      _resident((None, None, ffe, d), lambda i, te, nu, s, ds: (li, te[i], 0, 0))],
            out_specs=pl.BlockSpec(memory_space=pl.ANY),
            scratch_shapes=[pltpu.VMEM((2, tm, d), F32), pltpu.VMEM((tm, d), BF16), pltpu.VMEM((tm, d), F32),
                            pltpu.VMEM((2, tm, d // 2), jnp.uint32),
                            pltpu.SemaphoreType.DMA((2,)), pltpu.SemaphoreType.DMA((2,))]),
        out_shape=jax.ShapeDtypeStruct((out_rows, d // 2), jnp.uint32),
        compiler_params=_cparams(("arbitrary",)),
        name="moe_experts",
    )(tile_expert, n_used, src, dst, x1, w_gu, w_down)


def _moe_out_body(x_ref, route_ref, y0_ref, y1_ref, p_ref, wpg_ref, wpp_ref, g_ref, b_ref, o_ref):
    x1 = x_ref[...]
    route = route_ref[...]
    ffn = route[:, 2:3] * _unpack_bf16_pairs(y0_ref[...]) + route[:, 3:4] * _unpack_bf16_pairs(y1_ref[...])
    h = DEEPNORM_ALPHA * x1 + ffn + _ple(x1.astype(BF16), p_ref, wpg_ref, wpp_ref)
    o_ref[...] = _layer_norm(h, g_ref[...], b_ref[...])


def _moe_out(x1, route, p, y, wpg, wpp, g, b, *, tm):
    t, d = x1.shape
    nt = t // tm
    row = lambda i: (i, 0)
    return pl.pallas_call(
        _moe_out_body,
        grid=(nt,),
        in_specs=[pl.BlockSpec((tm, d), row), pl.BlockSpec((tm, LANES), row),
                  pl.BlockSpec((tm, d // 2), row), pl.BlockSpec((tm, d // 2), lambda i: (nt + i, 0)),
                  _ple_rows_spec(p, tm),
                  _layer_spec(wpg), _layer_spec(wpp), _layer_spec(g), _layer_spec(b)],
        out_specs=pl.BlockSpec((tm, d), row),
        out_shape=jax.ShapeDtypeStruct((t, d), F32),
        compiler_params=_cparams(("parallel",)),
        name="moe_combine_ln2",
    )(x1, route, y, y, p.arr, wpg.arr, wpp.arr, g.arr, b.arr)


def _slot_table_body(pos_ref, init_hbm, dst_ref, sem, *, t, groups):
    @pl.when(pl.program_id(0) == 0)
    def _():
        copy = pltpu.make_async_copy(init_hbm, dst_ref, sem)
        copy.start()
        copy.wait()

    batch = SLOT_TABLE_BATCH
    base = pl.program_id(0) * (t // groups)

    def put(kb, c):
        tok0 = base + kb * batch
        ps = [pos_ref[2 * tok0 + j] for j in range(2 * batch)]
        for j, p in enumerate(ps):
            dst_ref[p] = (j % 2) * t + tok0 + j // 2
        return c
    lax.fori_loop(0, t // groups // batch, put, 0)


def _slot_table(pos, *, t, tm, r):
    groups = SLOT_TABLE_GROUPS
    assert t % (groups * SLOT_TABLE_BATCH) == 0
    init = 2 * t + jnp.arange(r, dtype=jnp.int32) % tm
    smem = pl.BlockSpec(memory_space=pltpu.SMEM)
    return pl.pallas_call(
        functools.partial(_slot_table_body, t=t, groups=groups),
        grid=(groups,),
        in_specs=[smem, pl.BlockSpec(memory_space=pl.ANY)], out_specs=smem,
        out_shape=jax.ShapeDtypeStruct((r,), jnp.int32),
        scratch_shapes=[pltpu.SemaphoreType.DMA],
        compiler_params=_cparams(("arbitrary",)),
        name="moe_slot_table",
    )(pos, init)


def _routing_tables(route, *, tm, n_experts):
    t = route.shape[0]
    eids = route[:, :2].astype(jnp.int32).reshape(-1)
    onehot = (eids[:, None] == jnp.arange(n_experts, dtype=jnp.int32)[None, :]).astype(jnp.int32)
    csum = jnp.cumsum(onehot, axis=0)
    rank = jnp.sum((csum - onehot) * onehot, axis=1)
    cnt = csum[-1]
    ntile = (cnt + tm - 1) // tm
    tile_end = jnp.cumsum(ntile)
    row_off = (tile_end - ntile) * tm
    pos = jnp.sum(onehot * row_off[None, :], axis=1) + rank
    r = 2 * t + (n_experts + 2) * tm
    dst = _slot_table(pos.astype(jnp.int32), t=t, tm=tm, r=r)
    src = jnp.where(dst < 2 * t, dst % t, 0)
    n_used = tile_end[-1:]
    tiles = jnp.arange(r // tm, dtype=jnp.int32)
    te = jnp.sum((tiles[:, None] >= tile_end[None, :]).astype(jnp.int32), axis=1)
    last = jnp.sum((n_used - 1 >= tile_end).astype(jnp.int32))
    te = jnp.where(tiles < n_used, te, last).astype(jnp.int32)
    out_rows = 2 * t + tm
    return te, n_used.astype(jnp.int32), src.astype(jnp.int32), dst.astype(jnp.int32), out_rows


def kernel(x, p, rel_bias, w_in, attn_sinks, ssm_lambda_re, ssm_lambda_im, ssm_log_dt, ssm_b_re, ssm_b_im,
           ssm_c_re, ssm_c_im, ssm_d, w_glu, b_glu, w_branch_attn, w_branch_ssm, w_out, ln1_g, ln1_b,
           ffn_w_gate_up, ffn_w_down, moe_router, moe_w_gate_up, moe_w_down, ple_w_proj, ple_w_gate,
           ln2_g, ln2_b):
    bsz, seq, d = x.shape
    t = bsz * seq
    depth = w_in.shape[0]

    xt = x.reshape(t, d)
    pt = p.reshape(depth, t, p.shape[-1])

    bias = _attn_bias_table(rel_bias)
    sinks = attn_sinks * LOG2_E
    ffn_gu = ffn_w_gate_up.astype(BF16)
    ffn_dn = ffn_w_down.astype(BF16)
    moe_gu = moe_w_gate_up.astype(BF16)
    moe_dn = moe_w_down.astype(BF16)

    aw, kw = ATTN_WIDTH, KV_WIDTH
    u0 = aw + 2 * kw
    w_proj = jnp.concatenate([w_in[:, :, :aw], _per_head_kv_columns(w_in[:, :, aw:aw + kw]),
                              _per_head_kv_columns(w_in[:, :, aw + kw:u0]), w_in[:, :, u0:u0 + SSM_WIDTH]],
                             axis=2).astype(BF16)
    w_gates = w_in[:, :, u0 + SSM_WIDTH:].astype(BF16)
    wba, wbs, wo = w_branch_attn.astype(BF16), w_branch_ssm.astype(BF16), w_out.astype(BF16)
    wpg, wpp = ple_w_gate.astype(BF16), ple_w_proj.astype(BF16)
    g1, b1 = ln1_g.reshape(depth, 1, d), ln1_b.reshape(depth, 1, d)
    g2, b2 = ln2_g.reshape(depth, 1, d), ln2_b.reshape(depth, 1, d)

    lbr, lbi, bbr, bbi = _ssm_params(ssm_lambda_re, ssm_lambda_im, ssm_log_dt, ssm_b_re, ssm_b_im)
    bre = _block_diag_halves(bbr).astype(BF16)
    bim = _block_diag_halves(bbi).astype(BF16)
    cre = _block_diag_halves(jnp.transpose(ssm_c_re, (0, 1, 3, 2))).astype(BF16)
    cim = _block_diag_halves(jnp.transpose(ssm_c_im, (0, 1, 3, 2))).astype(BF16)
    ssm_skip = ssm_d.reshape(depth, 1, -1)
    wglu, bglu = w_glu.astype(BF16), b_glu.reshape(depth, 1, -1)

    rt = jnp.pad(moe_router, ((0, 0), (0, 0), (0, LANES - N_EXPERTS)))
    rhi = rt.astype(BF16)
    rcat = jnp.concatenate([rhi, (rt - rhi.astype(F32)).astype(BF16)], axis=2)

    for i in range(depth):
        at = lambda arr, idx=i: _Layer(arr, idx)
        q, kv, u = _inproj(xt, at(w_proj), tm=TM_PROJ)
        a_out = _attention(q, kv, at(sinks), bias, seq=seq, batch=bsz)
        s_out = _ssm(u, at(bre), at(bim), at(cre), at(cim), at(lbr), at(lbi), at(ssm_skip), at(wglu), at(bglu),
                     batch=bsz, steps=SSM_STEPS, lane_chunk=SSM_LANE_CHUNK)
        merge_w = (at(w_gates), at(wba), at(wbs), at(wo), at(g1), at(b1))
        if i % 2 == 0:
            (x1,) = _merge(xt, a_out, s_out, *merge_w, tm=TM_MERGE)
            xt = _ffn(x1, at(pt), ffn_gu, ffn_dn, i // 2, at(wpg), at(wpp), at(g2), at(b2), tm=TM_FFN)
        else:
            router = (_Layer(rcat, i // 2),)
            x1, route = _merge(xt, a_out, s_out, *merge_w, router, tm=TM_MERGE)
            te, n_used, src, dst, out_rows = _routing_tables(route, tm=TM_MOE, n_experts=N_EXPERTS)
            y = _moe_experts(x1, moe_gu, moe_dn, i // 2, te, n_used, src, dst, tm=TM_MOE, fw=FW_MOE,
                             out_rows=out_rows)
            xt = _moe_out(x1, route, at(pt), y, at(wpg), at(wpp), at(g2), at(b2), tm=TM_FFN)

    return xt.reshape(bsz, seq, d)
```

```python
import functools
import math
from typing import NamedTuple

import jax
import jax.numpy as jnp
import numpy as np
from jax import lax
from jax.experimental import pallas as pl
from jax.experimental.pallas import tpu as pltpu

F32 = jnp.float32
BF16 = jnp.bfloat16

D_MODEL = 1024
BATCH = 16
SEQ = 2048
DEPTH = 4
HEAD_DIM = 64
N_Q_HEADS = 8
N_KV_HEADS = 2
ATTN_WIDTH = N_Q_HEADS * HEAD_DIM
KV_WIDTH = N_KV_HEADS * HEAD_DIM
ATTN_BLOCK = 128
N_BUCKETS = 32
MAX_DISTANCE = 128
SSM_WIDTH = 512
SSM_GROUP_CH = 16
SSM_GROUPS = SSM_WIDTH // SSM_GROUP_CH
SSM_STATE = 64
D_FF = 2816
N_EXPERTS = 8
D_FF_EXPERT = 3584
PLE_DIM = 256
DEEPNORM_ALPHA = (2 * DEPTH) ** 0.25
LN_EPS = 1e-5
NEG_INF = -1e30
LOG2_E = math.log2(math.e)

LANES = 128
VMEM_LIMIT_BYTES = 56 * 1024 * 1024

TM_PROJ = 1024
ATTN_BLOCKS_PER_STEP = 8
TM_MERGE = 512
MERGE_ROW_GROUPS = 2
TM_FFN = 512
TM_MOE = 512
FW_MOE = 512
SLOT_TABLE_GROUPS = 16
SLOT_TABLE_BATCH = 8
SSM_STEPS = 32
SSM_PERM_STEPS = 16
SSM_LANE_CHUNK = 512
FFN_CHUNKS = ((0, 768), (768, 768), (1536, 768), (2304, 512))


def _cparams(sem):
    return pltpu.CompilerParams(dimension_semantics=sem, vmem_limit_bytes=VMEM_LIMIT_BYTES)


class _Layer(NamedTuple):
    arr: jax.Array
    idx: int


def _layer_spec(p, resident=False):
    tail = (0,) * (p.arr.ndim - 1)
    mode = {"pipeline_mode": pl.Buffered(1)} if resident else {}
    return pl.BlockSpec((None,) + p.arr.shape[1:], lambda *_: (p.idx,) + tail, **mode)


def _layer_norm(h, g, b):
    mu = jnp.mean(h, axis=-1, keepdims=True)
    c = h - mu
    var = jnp.mean(c * c, axis=-1, keepdims=True)
    return c * lax.rsqrt(var + LN_EPS) * g + b


def _sigmoid(v):
    return 1.0 / (1.0 + jnp.exp(-v))


def _pack_bf16_pairs(v):
    c = v.shape[1] // 2
    lo = lax.bitcast_convert_type(v[:, :c].astype(BF16).astype(F32), jnp.uint32)
    hi = lax.bitcast_convert_type(v[:, c:].astype(BF16).astype(F32), jnp.uint32)
    return (lo >> 16) | (hi & jnp.uint32(0xFFFF0000))


def _unpack_bf16_pairs(w):
    lo = lax.bitcast_convert_type(w << 16, F32)
    hi = lax.bitcast_convert_type(w & jnp.uint32(0xFFFF0000), F32)
    return jnp.concatenate([lo, hi], axis=1)


def _gelu_tanh(v):
    return 0.5 * v * (1.0 + jnp.tanh(math.sqrt(2.0 / math.pi) * (v + 0.044715 * (v * v * v))))


def _inproj_body(x_ref, w_ref, q_ref, kv_ref, u_ref):
    xb = x_ref[...].astype(BF16)
    z = jnp.dot(xb, w_ref[...], preferred_element_type=F32)
    aw = q_ref.shape[1]
    kw = kv_ref.shape[1]
    q_ref[...] = (z[:, :aw] * (LOG2_E * HEAD_DIM ** -0.5)).astype(BF16)
    kv_ref[...] = z[:, aw:aw + kw].astype(BF16)
    u_ref[...] = z[:, aw + kw:].astype(BF16)


def _inproj(x, w, *, tm):
    t, d = x.shape
    n = w.arr.shape[2]
    aw, kw = N_Q_HEADS * LANES, 2 * KV_WIDTH
    uw = n - aw - kw
    return pl.pallas_call(
        _inproj_body,
        grid=(t // tm,),
        in_specs=[pl.BlockSpec((tm, d), lambda i: (i, 0)), _layer_spec(w)],
        out_specs=[pl.BlockSpec((tm, aw), lambda i: (i, 0)),
                   pl.BlockSpec((tm, kw), lambda i: (i, 0)),
                   pl.BlockSpec((tm, uw), lambda i: (i, 0))],
        out_shape=[jax.ShapeDtypeStruct((t, aw), BF16),
                   jax.ShapeDtypeStruct((t, kw), BF16),
                   jax.ShapeDtypeStruct((t, uw), BF16)],
        compiler_params=_cparams(("parallel",)),
        name="inproj",
    )(x, w.arr)


def _attn_body(sink_ref, q_ref, kvc_ref, kvp_ref, bias0_ref, bias_ref, o_ref, *, blk, nbs, layer):
    group = N_Q_HEADS // N_KV_HEADS
    ks = slice(0, LANES)
    vs = slice(LANES, 2 * LANES)
    low = lax.broadcasted_iota(jnp.int32, (blk, LANES), 1) < HEAD_DIM
    for j in range(nbs):
        rows = slice(j * blk, (j + 1) * blk)
        b_ref = bias0_ref if j == 0 else bias_ref
        if j == 0:
            k = jnp.concatenate([kvp_ref[:, ks], kvc_ref[0:blk, ks]], axis=0)
            v = jnp.concatenate([kvp_ref[:, vs], kvc_ref[0:blk, vs]], axis=0)
        else:
            k = kvc_ref[(j - 1) * blk:(j + 1) * blk, ks]
            v = kvc_ref[(j - 1) * blk:(j + 1) * blk, vs]
        for tile in range(group):
            halves = []
            for g in range(N_KV_HEADS):
                h = tile + group * g
                qt = q_ref[rows, h * LANES:(h + 1) * LANES]
                s = lax.dot_general(qt, k, (((1,), (1,)), ((), ())), preferred_element_type=F32)
                s = s + b_ref[h]
                sink = sink_ref[layer, h]
                m = jnp.maximum(jnp.max(s, axis=-1, keepdims=True), sink)
                e = jnp.exp2(s - m)
                denom = jnp.sum(e, axis=-1, keepdims=True) + jnp.exp2(sink - m)
                halves.append(jnp.dot(e.astype(BF16), v, preferred_element_type=F32) / denom)
            o_ref[rows, tile * LANES:(tile + 1) * LANES] = jnp.where(low, halves[0], halves[1]).astype(BF16)


def _attention(q, kv, sinks, bias, *, seq, batch):
    assert KV_WIDTH == LANES
    t = q.shape[0]
    blk = ATTN_BLOCK
    nbs = ATTN_BLOCKS_PER_STEP
    ns = seq // (blk * nbs)
    qw, kw, aw = q.shape[1], kv.shape[1], ATTN_WIDTH
    table = (None, N_Q_HEADS, blk, 2 * blk)
    return pl.pallas_call(
        functools.partial(_attn_body, blk=blk, nbs=nbs, layer=sinks.idx),
        grid=(batch, ns),
        in_specs=[pl.BlockSpec(memory_space=pltpu.SMEM),
                  pl.BlockSpec((blk * nbs, qw), lambda b, n: (b * ns + n, 0)),
                  pl.BlockSpec((blk * nbs, kw), lambda b, n: (b * ns + n, 0)),
                  pl.BlockSpec((blk, kw), lambda b, n: ((b * ns + n) * nbs - jnp.minimum(n, 1), 0)),
                  pl.BlockSpec(table, lambda b, n: (jnp.minimum(n, 1), 0, 0, 0)),
                  pl.BlockSpec(table, lambda b, n: (1, 0, 0, 0))],
        out_specs=pl.BlockSpec((blk * nbs, aw), lambda b, n: (b * ns + n, 0)),
        out_shape=jax.ShapeDtypeStruct((t, aw), BF16),
        compiler_params=_cparams(("parallel", "parallel")),
        name="swa_attention",
    )(sinks.arr, q, kv, kv, bias, bias)


def _q_head_tiles(w):
    depth, d, _ = w.shape
    group = N_Q_HEADS // N_KV_HEADS
    wh = w.reshape(depth, d, N_Q_HEADS, 1, HEAD_DIM)
    half = jnp.arange(N_KV_HEADS)[None, None, None, :, None]
    kv_head = (jnp.arange(N_Q_HEADS) // group)[None, None, :, None, None]
    tiles = jnp.where(half == kv_head, wh, 0.0)
    return tiles.reshape(depth, d, N_Q_HEADS * LANES)


def _attn_out_head_order():
    group = N_Q_HEADS // N_KV_HEADS
    return [j + group * half for j in range(group) for half in range(N_KV_HEADS)]


def _attn_bias_table(rel_bias):
    blk = ATTN_BLOCK
    d = np.arange(blk, dtype=np.int32)
    max_exact = N_BUCKETS // 2
    d_f = np.maximum(d, 1).astype(np.float32)
    large = max_exact + (np.log(d_f / np.float32(max_exact)) / np.float32(math.log(MAX_DISTANCE / max_exact))
                         * np.float32(N_BUCKETS - max_exact)).astype(np.int32)
    large = np.minimum(large, N_BUCKETS - 1)
    bucket = np.where(d < max_exact, d, large)
    vals = jnp.transpose(rel_bias.astype(F32)[bucket], (1, 0))
    h = vals.shape[0]
    neg = jnp.full((h, blk), NEG_INF, F32)
    strip = jnp.concatenate([neg, vals[:, ::-1], neg], axis=1)
    rows = [strip[:, blk - 1 - i:blk - 1 - i + 2 * blk] for i in range(blk)]
    table = jnp.stack(rows, axis=1) * LOG2_E
    first = jnp.where(np.arange(2 * blk)[None, None, :] < blk, NEG_INF, table)
    return jnp.stack([first, table], axis=0)


def _ssm_param_body(lre_ref, lim_ref, ldt_ref, bre_ref, bim_ref, lbr_ref, lbi_ref, bbr_ref, bbi_ref):
    lre, lim = lre_ref[...], lim_ref[...]
    dt = jnp.exp(ldt_ref[...])
    mag = jnp.exp(lre * dt)
    lbr = mag * jnp.cos(lim * dt)
    lbi = mag * jnp.sin(lim * dt)
    lbr_ref[...] = lbr
    lbi_ref[...] = lbi
    nr, ni = lbr - 1.0, lbi
    inv = 1.0 / (lre * lre + lim * lim)
    cr = (nr * lre + ni * lim) * inv
    ci = (ni * lre - nr * lim) * inv
    br, bi = bre_ref[...], bim_ref[...]
    bbr_ref[...] = cr * br - ci * bi
    bbi_ref[...] = cr * bi + ci * br


def _ssm_params(lam_re, lam_im, log_dt, b_re, b_im):
    depth, g, p = lam_re.shape
    h = b_re.shape[3]
    n = depth * g
    ldt = jnp.broadcast_to(log_dt.reshape(n, 1, 1), (n, 1, p))
    brt = jnp.transpose(b_re, (0, 1, 3, 2)).reshape(n, h, p)
    bit = jnp.transpose(b_im, (0, 1, 3, 2)).reshape(n, h, p)
    sd = jax.ShapeDtypeStruct
    lbr, lbi, bbr, bbi = pl.pallas_call(
        _ssm_param_body,
        out_shape=[sd((n, 1, p), F32), sd((n, 1, p), F32), sd((n, h, p), F32), sd((n, h, p), F32)],
        name="ssm_params",
    )(lam_re.reshape(n, 1, p), lam_im.reshape(n, 1, p), ldt, brt, bit)
    return (lbr.reshape(depth, 1, g * p), lbi.reshape(depth, 1, g * p),
            bbr.reshape(depth, g, h, p), bbi.reshape(depth, g, h, p))


def _block_diag_halves(w):
    depth, g, a, b = w.shape
    hg = g // 2
    wh = w.reshape(depth, 2, hg, a, b)
    eye = jnp.eye(hg, dtype=bool)[None, None, :, None, :, None]
    full = jnp.where(eye, wh[:, :, :, :, None, :], 0.0)
    return full.reshape(depth, 2, hg * a, hg * b)


def _ssm_body(u_ref, perm_ref, permt_ref, bre_ref, bim_ref, cre_ref, cim_ref, lr_ref, li_ref, d_ref,
              wglu_ref, bglu_ref, o_ref, xr0, xi0, ub0, xr1, xi1, ub1, sr, si, *, steps, batch, lane_chunk):
    s = pl.program_id(0)

    @pl.when(s == 0)
    def _():
        for ref in (xr0, xi0, ub0, xr1, xi1, ub1, sr, si):
            ref[...] = jnp.zeros_like(ref)

    sub = SSM_PERM_STEPS
    rows_g = sub * batch

    def stage(fill, scan):
        xr_f, xi_f, ub_f = fill
        xr, xi, ub = scan
        parts = []
        for g in range(steps // sub):
            bm = jnp.concatenate([u_ref[b, g * sub:(g + 1) * sub, :] for b in range(batch)], axis=0)
            parts.append(jnp.dot(perm_ref[...], bm, preferred_element_type=F32).astype(BF16))
        u_new = jnp.concatenate(parts, axis=0)
        ub_f[...] = u_new
        hw = u_new.shape[1] // 2
        hs = xr.shape[1] // 2
        for hf in range(2):
            uh = u_new[:, hf * hw:(hf + 1) * hw]
            xr_f[:, hf * hs:(hf + 1) * hs] = jnp.dot(uh, bre_ref[hf], preferred_element_type=F32)
            xi_f[:, hf * hs:(hf + 1) * hs] = jnp.dot(uh, bim_ref[hf], preferred_element_type=F32)
        for c in range(xr.shape[1] // lane_chunk):
            cs = slice(c * lane_chunk, (c + 1) * lane_chunk)
            ar = lr_ref[:, cs]
            ai = li_ref[:, cs]
            pr, pi = sr[:, cs], si[:, cs]
            for t in range(steps):
                rows = slice(t * batch, (t + 1) * batch)
                nr = ar * pr - ai * pi + xr[rows, cs]
                ni = ar * pi + ai * pr + xi[rows, cs]
                xr[rows, cs] = nr
                xi[rows, cs] = ni
                pr, pi = nr, ni
            sr[:, cs] = pr
            si[:, cs] = pi
        ys = []
        for hf in range(2):
            xrb = xr[:, hf * hs:(hf + 1) * hs].astype(BF16)
            xib = xi[:, hf * hs:(hf + 1) * hs].astype(BF16)
            ys.append(jnp.dot(xrb, cre_ref[hf], preferred_element_type=F32)
                      - jnp.dot(xib, cim_ref[hf], preferred_element_type=F32))
        y = jnp.concatenate(ys, axis=1) + d_ref[...] * ub[...].astype(F32)
        y = _gelu_tanh(y)
        z = jnp.dot(y.astype(BF16), wglu_ref[...], preferred_element_type=F32) + bglu_ref[...]
        out = (y * _sigmoid(z)).astype(BF16)
        for g in range(steps // sub):
            bm = jnp.dot(permt_ref[...], out[g * rows_g:(g + 1) * rows_g], preferred_element_type=F32).astype(BF16)
            for b in range(batch):
                o_ref[b, g * sub:(g + 1) * sub, :] = bm[b * sub:(b + 1) * sub]

    @pl.when(s % 2 == 0)
    def _():
        stage((xr0, xi0, ub0), (xr1, xi1, ub1))

    @pl.when(s % 2 == 1)
    def _():
        stage((xr1, xi1, ub1), (xr0, xi0, ub0))


def _time_major_permutation(batch, sub):
    n = batch * sub
    r = np.arange(n)
    p = np.zeros((n, n), np.float32)
    p[r, (r % batch) * sub + r // batch] = 1.0
    return p


def _ssm(u, bre, bim, cre, cim, lbr, lbi, d, wglu, bglu, *, batch, steps, lane_chunk):
    t, w = u.shape
    seq = t // batch
    n = seq // steps
    ns = lbr.arr.shape[2]
    rows = steps * batch
    perm = _time_major_permutation(batch, SSM_PERM_STEPS)
    const2 = lambda i: (0, 0)
    params = (bre, bim, cre, cim, lbr, lbi, d, wglu, bglu)
    buf = [pltpu.VMEM((rows, ns), F32), pltpu.VMEM((rows, ns), F32), pltpu.VMEM((rows, w), BF16)]
    out = pl.pallas_call(
        functools.partial(_ssm_body, steps=steps, batch=batch, lane_chunk=lane_chunk),
        grid=(n + 1,),
        in_specs=[pl.BlockSpec((batch, steps, w), lambda i: (0, jnp.minimum(i, n - 1), 0)),
                  pl.BlockSpec(perm.shape, const2), pl.BlockSpec(perm.shape, const2)]
        + [_layer_spec(p) for p in params],
        out_specs=pl.BlockSpec((batch, steps, w), lambda i: (0, jnp.maximum(i - 1, 0), 0)),
        out_shape=jax.ShapeDtypeStruct((batch, seq, w), BF16),
        scratch_shapes=buf + buf + [pltpu.VMEM((batch, ns), F32), pltpu.VMEM((batch, ns), F32)],
        compiler_params=_cparams(("arbitrary",)),
        name="s5_scan",
    )(u.reshape(batch, seq, w), jnp.asarray(perm, BF16), jnp.asarray(perm.T, BF16), *[p.arr for p in params])
    return out.reshape(t, w)


def _split_bf16(v):
    hi = v.astype(BF16)
    lo = (v - hi.astype(F32)).astype(BF16)
    return hi, lo


def _merge_body(x_ref, a_ref, s_ref, wg_ref, wba_ref, wbs_ref, wo_ref, g_ref, b_ref, *rest, with_router):
    if with_router:
        rcat_ref, tri_ref, x1_ref, route_ref, count_ref, run = rest

        @pl.when(pl.program_id(0) == 0)
        def _():
            run[...] = jnp.zeros_like(run)
    else:
        (x1_ref,) = rest
    tm, d = x_ref.shape
    rg = tm // MERGE_ROW_GROUPS
    for grp in range(MERGE_ROW_GROUPS):
        rows = slice(grp * rg, (grp + 1) * rg)
        x = x_ref[rows, :]
        gates = jnp.dot(x.astype(BF16), wg_ref[...], preferred_element_type=F32)
        pa = jnp.dot(a_ref[rows, :], wba_ref[...], preferred_element_type=F32)
        ps = jnp.dot(s_ref[rows, :], wbs_ref[...], preferred_element_type=F32)
        merged = _sigmoid(gates[:, :d]) * pa + _sigmoid(gates[:, d:]) * ps
        y = jnp.dot(merged.astype(BF16), wo_ref[...], preferred_element_type=F32)
        x1 = _layer_norm(DEEPNORM_ALPHA * x + y, g_ref[...], b_ref[...])
        x1_ref[rows, :] = x1
        if with_router:
            hi, lo = _split_bf16(x1)
            both = jnp.dot(hi, rcat_ref[...], preferred_element_type=F32)
            logits = (both[:, :LANES] + jnp.dot(lo, rcat_ref[:, :LANES], preferred_element_type=F32)
                      + both[:, LANES:])
            lane = lax.broadcasted_iota(jnp.int32, logits.shape, 1).astype(F32)
            big = float(LANES)
            l1 = jnp.where(lane < N_EXPERTS, logits, -jnp.inf)
            m1 = jnp.max(l1, axis=-1, keepdims=True)
            i1 = jnp.min(jnp.where(l1 == m1, lane, big), axis=-1, keepdims=True)
            l2 = jnp.where(lane == i1, -jnp.inf, l1)
            m2 = jnp.max(l2, axis=-1, keepdims=True)
            i2 = jnp.min(jnp.where(l2 == m2, lane, big), axis=-1, keepdims=True)
            ed = jnp.exp(m2 - m1)
            g1 = 1.0 / (1.0 + ed)
            g2 = ed / (1.0 + ed)
            oh1 = lane == i1
            oh2 = lane == i2
            chosen = jnp.where(oh1, 1.0, 0.0) + jnp.where(oh2, 1.0, 0.0)
            before = jnp.dot(tri_ref[...], chosen.astype(BF16), preferred_element_type=F32) + run[...]
            r1 = jnp.sum(jnp.where(oh1, before, 0.0), axis=-1, keepdims=True)
            r2 = jnp.sum(jnp.where(oh2, before, 0.0), axis=-1, keepdims=True)
            run[...] = before[rg - 1:rg, :] + chosen[rg - 1:rg, :]
            route = jnp.where(lane == 0.0, i1, jnp.where(lane == 1.0, i2, jnp.where(lane == 2.0, g1, jnp.where(
                lane == 3.0, g2, jnp.where(lane == 4.0, r1, jnp.where(lane == 5.0, r2, 0.0))))))
            route_ref[rows, :] = route
    if with_router:
        count_ref[...] = jnp.broadcast_to(run[...], count_ref.shape)


def _merge(x, a, s, wg, wba, wbs, wo, g, b, router=None, *, tm):
    t, d = x.shape
    row = lambda i: (i, 0)
    params = [wg, wba, wbs, wo, g, b] + list(router or ())
    in_specs = [pl.BlockSpec((tm, d), row), pl.BlockSpec((tm, a.shape[1]), row), pl.BlockSpec((tm, s.shape[1]), row)]
    in_specs += [_layer_spec(p) for p in params]
    args = [x, a, s] + [p.arr for p in params]
    out_specs = [pl.BlockSpec((tm, d), row)]
    out_shape = [jax.ShapeDtypeStruct((t, d), F32)]
    scratch = []
    if router is not None:
        rg = tm // MERGE_ROW_GROUPS
        tri = np.tril(np.ones((rg, rg), np.float32), -1)
        in_specs.append(pl.BlockSpec((rg, rg), lambda i: (0, 0)))
        args.append(jnp.asarray(tri, BF16))
        out_specs += [pl.BlockSpec((tm, LANES), row), pl.BlockSpec((8, LANES), lambda i: (0, 0))]
        out_shape += [jax.ShapeDtypeStruct((t, LANES), F32), jax.ShapeDtypeStruct((8, LANES), F32)]
        scratch = [pltpu.VMEM((1, LANES), F32)]
    return pl.pallas_call(
        functools.partial(_merge_body, with_router=router is not None),
        grid=(t // tm,),
        in_specs=in_specs, out_specs=out_specs, out_shape=out_shape, scratch_shapes=scratch,
        compiler_params=_cparams(("parallel",) if router is None else ("arbitrary",)),
        name="merge_ln1",
    )(*args)


def _ple(xb, p_ref, wpg_ref, wpp_ref):
    gate = _sigmoid(jnp.dot(xb, wpg_ref[...], preferred_element_type=F32))
    return gate * jnp.dot(p_ref[...].astype(BF16), wpp_ref[...], preferred_element_type=F32)


def _ffn_body(x_ref, p_ref, wg_ref, wu_ref, wd_ref, wpg_ref, wpp_ref, g_ref, b_ref, o_ref, acc):
    x1 = x_ref[...]
    xb = x1.astype(BF16)
    for c, (c0, cw) in enumerate(FFN_CHUNKS):
        gt = jnp.dot(xb, wg_ref[:, c0:c0 + cw], preferred_element_type=F32)
        up = jnp.dot(xb, wu_ref[:, c0:c0 + cw], preferred_element_type=F32)
        act = (gt * _sigmoid(gt) * up).astype(BF16)
        contrib = jnp.dot(act, wd_ref[c0:c0 + cw, :], preferred_element_type=F32)
        if c == 0:
            acc[...] = contrib
        else:
            acc[...] += contrib
    h = DEEPNORM_ALPHA * x1 + acc[...] + _ple(xb, p_ref, wpg_ref, wpp_ref)
    o_ref[...] = _layer_norm(h, g_ref[...], b_ref[...])


def _resident(shape, index_map):
    return pl.BlockSpec(shape, index_map, pipeline_mode=pl.Buffered(1))


def _ple_rows_spec(p, tm):
    return pl.BlockSpec((None, tm, p.arr.shape[2]), lambda i: (p.idx, i, 0))


def _ffn(x1, p, w_gu, w_down, li, wpg, wpp, g, b, *, tm):
    t, d = x1.shape
    ff = w_down.shape[1]
    row = lambda i: (i, 0)
    return pl.pallas_call(
        _ffn_body,
        grid=(t // tm,),
        in_specs=[pl.BlockSpec((tm, d), row), _ple_rows_spec(p, tm),
                  _resident((None, d, ff), lambda i: (li, 0, 0)),
                  _resident((None, d, ff), lambda i: (li, 0, 1)),
                  _resident((None, ff, d), lambda i: (li, 0, 0)),
                  _layer_spec(wpg, resident=True), _layer_spec(wpp, resident=True),
                  _layer_spec(g), _layer_spec(b)],
        out_specs=pl.BlockSpec((tm, d), row),
        out_shape=jax.ShapeDtypeStruct((t, d), F32),
        scratch_shapes=[pltpu.VMEM((tm, d), F32)],
        compiler_params=_cparams(("parallel",)),
        name="ffn_ln2",
    )(x1, p.arr, w_gu, w_gu, w_down, wpg.arr, wpp.arr, g.arr, b.arr)


def _moe_body(te_ref, nu_ref, src_ref, dst_ref, x_hbm, wgu_ref, wd_ref, y_hbm,
              xg, xb, acc, ybuf, gsem, ssem, *, tm, fw, spare_row):
    i = pl.program_id(0)
    nu = nu_ref[0]
    slot = i % 2
    other = 1 - slot
    ffe = wd_ref.shape[0]
    nc = ffe // fw
    per_chunk = tm // nc

    def gather(tile, r, buf):
        row = src_ref[tile * tm + r]
        pltpu.make_async_copy(x_hbm.at[pl.ds(row, 1), :], xg.at[buf, pl.ds(r, 1), :], gsem.at[buf]).start()

    def scatter(row, r, buf):
        pltpu.make_async_copy(ybuf.at[buf, pl.ds(r, 1), :], y_hbm.at[pl.ds(row, 1), :], ssem.at[buf]).start()

    def prev_tile_dst(r):
        return jnp.where(i > 0, dst_ref[jnp.maximum(i - 1, 0) * tm + r], spare_row + r)

    @pl.when(i == 0)
    def _():
        ybuf[...] = jnp.zeros_like(ybuf)

        def one(r, c):
            gather(0, r, 0)
            return c
        lax.fori_loop(0, tm, one, 0)

    @pl.when(i <= nu)
    def _():
        pltpu.make_async_copy(x_hbm.at[pl.ds(0, tm), :], xg.at[slot], gsem.at[slot]).wait()

    @pl.when(i < nu)
    def _():
        xb[...] = xg[slot].astype(BF16)
        x = xb[...]
        for c in range(nc):
            c0 = c * fw
            gt = jnp.dot(x, wgu_ref[:, c0:c0 + fw], preferred_element_type=F32)
            up = jnp.dot(x, wgu_ref[:, ffe + c0:ffe + c0 + fw], preferred_element_type=F32)
            act = (gt * _sigmoid(gt) * up).astype(BF16)
            part = jnp.dot(act, wd_ref[c0:c0 + fw, :], preferred_element_type=F32)
            if c == 0:
                acc[...] = part
            else:
                acc[...] += part
            r1 = tm if c == nc - 1 else (c + 1) * per_chunk
            for r in range(c * per_chunk, r1):
                gather(i + 1, r, other)
                scatter(prev_tile_dst(r), r, other)

    @pl.when(i == nu)
    def _():
        def one(r, c):
            scatter(dst_ref[(i - 1) * tm + r], r, other)
            return c
        lax.fori_loop(0, tm, one, 0)

    @pl.when((i >= 1) & (i <= nu + 1))
    def _():
        pltpu.make_async_copy(ybuf.at[slot], y_hbm.at[pl.ds(0, tm), :], ssem.at[slot]).wait()

    @pl.when(i < nu)
    def _():
        ybuf[slot] = _pack_bf16_pairs(acc[...])


def _moe_experts(x1, w_gu, w_down, li, tile_expert, n_used, src, dst, *, tm, fw, out_rows):
    t, d = x1.shape
    ffe = w_down.shape[2]
    nt = src.shape[0] // tm
    return pl.pallas_call(
        functools.partial(_moe_body, tm=tm, fw=fw, spare_row=out_rows - tm),
        grid_spec=pltpu.PrefetchScalarGridSpec(
            num_scalar_prefetch=4,
            grid=(nt,),
            in_specs=[pl.BlockSpec(memory_space=pl.ANY),
                      _resident((None, None, d, 2 * ffe), lambda i, te, nu, s, ds: (li, te[i], 0, 0)),
                      _resident((None, None, ffe, d), lambda i, te, nu, s, ds: (li, te[i], 0, 0))],
            out_specs=pl.BlockSpec(memory_space=pl.ANY),
            scratch_shapes=[pltpu.VMEM((2, tm, d), F32), pltpu.VMEM((tm, d), BF16), pltpu.VMEM((tm, d), F32),
                            pltpu.VMEM((2, tm, d // 2), jnp.uint32),
                            pltpu.SemaphoreType.DMA((2,)), pltpu.SemaphoreType.DMA((2,))]),
        out_shape=jax.ShapeDtypeStruct((out_rows, d // 2), jnp.uint32),
        compiler_params=_cparams(("arbitrary",)),
        name="moe_experts",
    )(tile_expert, n_used, src, dst, x1, w_gu, w_down)


def _moe_out_body(x_ref, route_ref, y0_ref, y1_ref, p_ref, wpg_ref, wpp_ref, g_ref, b_ref, o_ref):
    x1 = x_ref[...]
    route = route_ref[...]
    ffn = route[:, 2:3] * _unpack_bf16_pairs(y0_ref[...]) + route[:, 3:4] * _unpack_bf16_pairs(y1_ref[...])
    h = DEEPNORM_ALPHA * x1 + ffn + _ple(x1.astype(BF16), p_ref, wpg_ref, wpp_ref)
    o_ref[...] = _layer_norm(h, g_ref[...], b_ref[...])


def _moe_out(x1, route, p, y, wpg, wpp, g, b, *, tm):
    t, d = x1.shape
    nt = t // tm
    row = lambda i: (i, 0)
    return pl.pallas_call(
        _moe_out_body,
        grid=(nt,),
        in_specs=[pl.BlockSpec((tm, d), row), pl.BlockSpec((tm, LANES), row),
                  pl.BlockSpec((tm, d // 2), row), pl.BlockSpec((tm, d // 2), lambda i: (nt + i, 0)),
                  _ple_rows_spec(p, tm),
                  _layer_spec(wpg), _layer_spec(wpp), _layer_spec(g), _layer_spec(b)],
        out_specs=pl.BlockSpec((tm, d), row),
        out_shape=jax.ShapeDtypeStruct((t, d), F32),
        compiler_params=_cparams(("parallel",)),
        name="moe_combine_ln2",
    )(x1, route, y, y, p.arr, wpg.arr, wpp.arr, g.arr, b.arr)


def _slot_table_body(pos_ref, init_hbm, dst_ref, sem, *, t, groups):
    @pl.when(pl.program_id(0) == 0)
    def _():
        copy = pltpu.make_async_copy(init_hbm, dst_ref, sem)
        copy.start()
        copy.wait()

    batch = SLOT_TABLE_BATCH
    base = pl.program_id(0) * (t // groups)

    def put(kb, c):
        tok0 = base + kb * batch
        ps = [pos_ref[2 * tok0 + j] for j in range(2 * batch)]
        for j, p in enumerate(ps):
            dst_ref[p] = (j % 2) * t + tok0 + j // 2
        return c
    lax.fori_loop(0, t // groups // batch, put, 0)


def _slot_table(pos, *, t, tm, r):
    groups = SLOT_TABLE_GROUPS
    assert t % (groups * SLOT_TABLE_BATCH) == 0
    init = 2 * t + jnp.arange(r, dtype=jnp.int32) % tm
    smem = pl.BlockSpec(memory_space=pltpu.SMEM)
    return pl.pallas_call(
        functools.partial(_slot_table_body, t=t, groups=groups),
        grid=(groups,),
        in_specs=[smem, pl.BlockSpec(memory_space=pl.ANY)], out_specs=smem,
        out_shape=jax.ShapeDtypeStruct((r,), jnp.int32),
        scratch_shapes=[pltpu.SemaphoreType.DMA],
        compiler_params=_cparams(("arbitrary",)),
        name="moe_slot_table",
    )(pos, init)


def _routing_tables(route, counts, *, tm, n_experts):
    t = route.shape[0]
    eids = route[:, :2].astype(jnp.int32).reshape(-1)
    rank = route[:, 4:6].astype(jnp.int32).reshape(-1)
    onehot = (eids[:, None] == jnp.arange(n_experts, dtype=jnp.int32)[None, :]).astype(jnp.int32)
    cnt = counts[0, :n_experts].astype(jnp.int32)
    ntile = (cnt + tm - 1) // tm
    tile_end = jnp.cumsum(ntile)
    row_off = (tile_end - ntile) * tm
    pos = jnp.sum(onehot * row_off[None, :], axis=1) + rank
    r = 2 * t + (n_experts + 2) * tm
    dst = _slot_table(pos.astype(jnp.int32), t=t, tm=tm, r=r)
    src = jnp.where(dst < 2 * t, dst % t, 0)
    n_used = tile_end[-1:]
    tiles = jnp.arange(r // tm, dtype=jnp.int32)
    te = jnp.sum((tiles[:, None] >= tile_end[None, :]).astype(jnp.int32), axis=1)
    last = jnp.sum((n_used - 1 >= tile_end).astype(jnp.int32))
    te = jnp.where(tiles < n_used, te, last).astype(jnp.int32)
    out_rows = 2 * t + tm
    return te, n_used.astype(jnp.int32), src.astype(jnp.int32), dst.astype(jnp.int32), out_rows


def kernel(x, p, rel_bias, w_in, attn_sinks, ssm_lambda_re, ssm_lambda_im, ssm_log_dt, ssm_b_re, ssm_b_im,
           ssm_c_re, ssm_c_im, ssm_d, w_glu, b_glu, w_branch_attn, w_branch_ssm, w_out, ln1_g, ln1_b,
           ffn_w_gate_up, ffn_w_down, moe_router, moe_w_gate_up, moe_w_down, ple_w_proj, ple_w_gate,
           ln2_g, ln2_b):
    bsz, seq, d = x.shape
    t = bsz * seq
    depth = w_in.shape[0]

    xt = x.reshape(t, d)
    pt = p.reshape(depth, t, p.shape[-1])

    bias = _attn_bias_table(rel_bias)
    sinks = attn_sinks * LOG2_E
    ffn_gu = ffn_w_gate_up.astype(BF16)
    ffn_dn = ffn_w_down.astype(BF16)
    moe_gu = moe_w_gate_up.astype(BF16)
    moe_dn = moe_w_down.astype(BF16)

    aw, kw = ATTN_WIDTH, KV_WIDTH
    u0 = aw + 2 * kw
    w_proj = jnp.concatenate([_q_head_tiles(w_in[:, :, :aw]), w_in[:, :, aw:u0 + SSM_WIDTH]], axis=2).astype(BF16)
    w_gates = w_in[:, :, u0 + SSM_WIDTH:].astype(BF16)
    wba = w_branch_attn.reshape(depth, N_Q_HEADS, HEAD_DIM, d)[:, np.array(_attn_out_head_order())]
    wba = wba.reshape(depth, aw, d).astype(BF16)
    wbs, wo = w_branch_ssm.astype(BF16), w_out.astype(BF16)
    wpg, wpp = ple_w_gate.astype(BF16), ple_w_proj.astype(BF16)
    g1, b1 = ln1_g.reshape(depth, 1, d), ln1_b.reshape(depth, 1, d)
    g2, b2 = ln2_g.reshape(depth, 1, d), ln2_b.reshape(depth, 1, d)

    lbr, lbi, bbr, bbi = _ssm_params(ssm_lambda_re, ssm_lambda_im, ssm_log_dt, ssm_b_re, ssm_b_im)
    bre = _block_diag_halves(bbr).astype(BF16)
    bim = _block_diag_halves(bbi).astype(BF16)
    cre = _block_diag_halves(jnp.transpose(ssm_c_re, (0, 1, 3, 2))).astype(BF16)
    cim = _block_diag_halves(jnp.transpose(ssm_c_im, (0, 1, 3, 2))).astype(BF16)
    ssm_skip = ssm_d.reshape(depth, 1, -1)
    wglu, bglu = w_glu.astype(BF16), b_glu.reshape(depth, 1, -1)

    rt = jnp.pad(moe_router, ((0, 0), (0, 0), (0, LANES - N_EXPERTS)))
    rhi = rt.astype(BF16)
    rcat = jnp.concatenate([rhi, (rt - rhi.astype(F32)).astype(BF16)], axis=2)

    for i in range(depth):
        at = lambda arr, idx=i: _Layer(arr, idx)
        q, kv, u = _inproj(xt, at(w_proj), tm=TM_PROJ)
        a_out = _attention(q, kv, at(sinks), bias, seq=seq, batch=bsz)
        s_out = _ssm(u, at(bre), at(bim), at(cre), at(cim), at(lbr), at(lbi), at(ssm_skip), at(wglu), at(bglu),
                     batch=bsz, steps=SSM_STEPS, lane_chunk=SSM_LANE_CHUNK)
        merge_w = (at(w_gates), at(wba), at(wbs), at(wo), at(g1), at(b1))
        if i % 2 == 0:
            (x1,) = _merge(xt, a_out, s_out, *merge_w, tm=TM_MERGE)
            xt = _ffn(x1, at(pt), ffn_gu, ffn_dn, i // 2, at(wpg), at(wpp), at(g2), at(b2), tm=TM_FFN)
        else:
            router = (_Layer(rcat, i // 2),)
            x1, route, counts = _merge(xt, a_out, s_out, *merge_w, router, tm=TM_MERGE)
            te, n_used, src, dst, out_rows = _routing_tables(route, counts, tm=TM_MOE, n_experts=N_EXPERTS)
            y = _moe_experts(x1, moe_gu, moe_dn, i // 2, te, n_used, src, dst, tm=TM_MOE, fw=FW_MOE,
                             out_rows=out_rows)
            xt = _moe_out(x1, route, at(pt), y, at(wpg), at(wpp), at(g2), at(b2), tm=TM_FFN)

    return xt.reshape(bsz, seq, d)
```

```python
import functools
import math
from typing import NamedTuple

import jax
import jax.numpy as jnp
import numpy as np
from jax import lax
from jax.experimental import pallas as pl
from jax.experimental.pallas import tpu as pltpu

F32 = jnp.float32
BF16 = jnp.bfloat16

D_MODEL = 1024
BATCH = 16
SEQ = 2048
DEPTH = 4
HEAD_DIM = 64
N_Q_HEADS = 8
N_KV_HEADS = 2
ATTN_WIDTH = N_Q_HEADS * HEAD_DIM
KV_WIDTH = N_KV_HEADS * HEAD_DIM
ATTN_BLOCK = 128
N_BUCKETS = 32
MAX_DISTANCE = 128
SSM_WIDTH = 512
SSM_GROUP_CH = 16
SSM_GROUPS = SSM_WIDTH // SSM_GROUP_CH
SSM_STATE = 64
D_FF = 2816
N_EXPERTS = 8
D_FF_EXPERT = 3584
PLE_DIM = 256
DEEPNORM_ALPHA = (2 * DEPTH) ** 0.25
LN_EPS = 1e-5
NEG_INF = -1e30
LOG2_E = math.log2(math.e)

LANES = 128
VMEM_LIMIT_BYTES = 56 * 1024 * 1024

TM_PROJ = 1024
ATTN_BLOCKS_PER_STEP = 8
TM_MERGE = 512
MERGE_ROW_GROUPS = 2
TM_FFN = 512
TM_MOE = 512
FW_MOE = 512
SLOT_TABLE_GROUPS = 16
SLOT_TABLE_BATCH = 8
SSM_STEPS = 32
SSM_PERM_STEPS = 16
SSM_LANE_CHUNK = 512
FFN_CHUNKS = ((0, 768), (768, 768), (1536, 768), (2304, 512))


def _cparams(sem):
    return pltpu.CompilerParams(dimension_semantics=sem, vmem_limit_bytes=VMEM_LIMIT_BYTES)


class _Layer(NamedTuple):
    arr: jax.Array
    idx: int


def _layer_spec(p, resident=False):
    tail = (0,) * (p.arr.ndim - 1)
    mode = {"pipeline_mode": pl.Buffered(1)} if resident else {}
    return pl.BlockSpec((None,) + p.arr.shape[1:], lambda *_: (p.idx,) + tail, **mode)


def _layer_norm(h, g, b):
    mu = jnp.mean(h, axis=-1, keepdims=True)
    c = h - mu
    var = jnp.mean(c * c, axis=-1, keepdims=True)
    return c * lax.rsqrt(var + LN_EPS) * g + b


def _sigmoid(v):
    return 1.0 / (1.0 + jnp.exp(-v))


def _pack_bf16_pairs(v):
    c = v.shape[1] // 2
    lo = lax.bitcast_convert_type(v[:, :c].astype(BF16).astype(F32), jnp.uint32)
    hi = lax.bitcast_convert_type(v[:, c:].astype(BF16).astype(F32), jnp.uint32)
    return (lo >> 16) | (hi & jnp.uint32(0xFFFF0000))


def _unpack_bf16_pairs(w):
    lo = lax.bitcast_convert_type(w << 16, F32)
    hi = lax.bitcast_convert_type(w & jnp.uint32(0xFFFF0000), F32)
    return jnp.concatenate([lo, hi], axis=1)


def _gelu_tanh(v):
    return 0.5 * v * (1.0 + jnp.tanh(math.sqrt(2.0 / math.pi) * (v + 0.044715 * (v * v * v))))


def _inproj_body(x_ref, w_ref, q_ref, kv_ref, u_ref):
    xb = x_ref[...].astype(BF16)
    z = jnp.dot(xb, w_ref[...], preferred_element_type=F32)
    aw = q_ref.shape[1]
    kw = kv_ref.shape[1]
    q_ref[...] = (z[:, :aw] * (LOG2_E * HEAD_DIM ** -0.5)).astype(BF16)
    kv_ref[...] = z[:, aw:aw + kw].astype(BF16)
    u_ref[...] = z[:, aw + kw:].astype(BF16)


def _inproj(x, w, *, tm):
    t, d = x.shape
    n = w.arr.shape[2]
    aw, kw = N_Q_HEADS * LANES, 2 * KV_WIDTH
    uw = n - aw - kw
    return pl.pallas_call(
        _inproj_body,
        grid=(t // tm,),
        in_specs=[pl.BlockSpec((tm, d), lambda i: (i, 0)), _layer_spec(w)],
        out_specs=[pl.BlockSpec((tm, aw), lambda i: (i, 0)),
                   pl.BlockSpec((tm, kw), lambda i: (i, 0)),
                   pl.BlockSpec((tm, uw), lambda i: (i, 0))],
        out_shape=[jax.ShapeDtypeStruct((t, aw), BF16),
                   jax.ShapeDtypeStruct((t, kw), BF16),
                   jax.ShapeDtypeStruct((t, uw), BF16)],
        compiler_params=_cparams(("parallel",)),
        name="inproj",
    )(x, w.arr)


def _attn_body(sink_ref, q_ref, kvc_ref, kvp_ref, bias0_ref, bias_ref, o_ref, *, blk, nbs, layer):
    group = N_Q_HEADS // N_KV_HEADS
    ks = slice(0, LANES)
    vs = slice(LANES, 2 * LANES)
    low = lax.broadcasted_iota(jnp.int32, (blk, LANES), 1) < HEAD_DIM
    for j in range(nbs):
        rows = slice(j * blk, (j + 1) * blk)
        b_ref = bias0_ref if j == 0 else bias_ref
        if j == 0:
            k = jnp.concatenate([kvp_ref[:, ks], kvc_ref[0:blk, ks]], axis=0)
            v = jnp.concatenate([kvp_ref[:, vs], kvc_ref[0:blk, vs]], axis=0)
        else:
            k = kvc_ref[(j - 1) * blk:(j + 1) * blk, ks]
            v = kvc_ref[(j - 1) * blk:(j + 1) * blk, vs]
        for tile in range(group):
            halves = []
            for g in range(N_KV_HEADS):
                h = tile + group * g
                qt = q_ref[rows, h * LANES:(h + 1) * LANES]
                s = lax.dot_general(qt, k, (((1,), (1,)), ((), ())), preferred_element_type=F32)
                s = s + b_ref[h]
                sink = sink_ref[layer, h]
                m = jnp.maximum(jnp.max(s, axis=-1, keepdims=True), sink)
                e = jnp.exp2(s - m)
                denom = jnp.sum(e, axis=-1, keepdims=True) + jnp.exp2(sink - m)
                halves.append(jnp.dot(e.astype(BF16), v, preferred_element_type=F32) / denom)
            o_ref[rows, tile * LANES:(tile + 1) * LANES] = jnp.where(low, halves[0], halves[1]).astype(BF16)


def _attention(q, kv, sinks, bias, *, seq, batch):
    assert KV_WIDTH == LANES
    t = q.shape[0]
    blk = ATTN_BLOCK
    nbs = ATTN_BLOCKS_PER_STEP
    ns = seq // (blk * nbs)
    qw, kw, aw = q.shape[1], kv.shape[1], ATTN_WIDTH
    table = (None, N_Q_HEADS, blk, 2 * blk)
    return pl.pallas_call(
        functools.partial(_attn_body, blk=blk, nbs=nbs, layer=sinks.idx),
        grid=(batch, ns),
        in_specs=[pl.BlockSpec(memory_space=pltpu.SMEM),
                  pl.BlockSpec((blk * nbs, qw), lambda b, n: (b * ns + n, 0)),
                  pl.BlockSpec((blk * nbs, kw), lambda b, n: (b * ns + n, 0)),
                  pl.BlockSpec((blk, kw), lambda b, n: ((b * ns + n) * nbs - jnp.minimum(n, 1), 0)),
                  pl.BlockSpec(table, lambda b, n: (jnp.minimum(n, 1), 0, 0, 0)),
                  pl.BlockSpec(table, lambda b, n: (1, 0, 0, 0))],
        out_specs=pl.BlockSpec((blk * nbs, aw), lambda b, n: (b * ns + n, 0)),
        out_shape=jax.ShapeDtypeStruct((t, aw), BF16),
        compiler_params=_cparams(("parallel", "parallel")),
        name="swa_attention",
    )(sinks.arr, q, kv, kv, bias, bias)


def _q_head_tiles(w):
    depth, d, _ = w.shape
    group = N_Q_HEADS // N_KV_HEADS
    wh = w.reshape(depth, d, N_Q_HEADS, 1, HEAD_DIM)
    half = jnp.arange(N_KV_HEADS)[None, None, None, :, None]
    kv_head = (jnp.arange(N_Q_HEADS) // group)[None, None, :, None, None]
    tiles = jnp.where(half == kv_head, wh, 0.0)
    return tiles.reshape(depth, d, N_Q_HEADS * LANES)


def _attn_out_head_order():
    group = N_Q_HEADS // N_KV_HEADS
    return [j + group * half for j in range(group) for half in range(N_KV_HEADS)]


def _attn_bias_table(rel_bias):
    blk = ATTN_BLOCK
    d = np.arange(blk, dtype=np.int32)
    max_exact = N_BUCKETS // 2
    d_f = np.maximum(d, 1).astype(np.float32)
    large = max_exact + (np.log(d_f / np.float32(max_exact)) / np.float32(math.log(MAX_DISTANCE / max_exact))
                         * np.float32(N_BUCKETS - max_exact)).astype(np.int32)
    large = np.minimum(large, N_BUCKETS - 1)
    bucket = np.where(d < max_exact, d, large)
    vals = jnp.transpose(rel_bias.astype(F32)[bucket], (1, 0))
    h = vals.shape[0]
    neg = jnp.full((h, blk), NEG_INF, F32)
    strip = jnp.concatenate([neg, vals[:, ::-1], neg], axis=1)
    rows = [strip[:, blk - 1 - i:blk - 1 - i + 2 * blk] for i in range(blk)]
    table = jnp.stack(rows, axis=1) * LOG2_E
    first = jnp.where(np.arange(2 * blk)[None, None, :] < blk, NEG_INF, table)
    return jnp.stack([first, table], axis=0)


def _ssm_param_body(lre_ref, lim_ref, ldt_ref, bre_ref, bim_ref, lbr_ref, lbi_ref, bbr_ref, bbi_ref):
    lre, lim = lre_ref[...], lim_ref[...]
    dt = jnp.exp(ldt_ref[...])
    mag = jnp.exp(lre * dt)
    lbr = mag * jnp.cos(lim * dt)
    lbi = mag * jnp.sin(lim * dt)
    lbr_ref[...] = lbr
    lbi_ref[...] = lbi
    nr, ni = lbr - 1.0, lbi
    inv = 1.0 / (lre * lre + lim * lim)
    cr = (nr * lre + ni * lim) * inv
    ci = (ni * lre - nr * lim) * inv
    br, bi = bre_ref[...], bim_ref[...]
    bbr_ref[...] = cr * br - ci * bi
    bbi_ref[...] = cr * bi + ci * br


def _ssm_params(lam_re, lam_im, log_dt, b_re, b_im):
    depth, g, p = lam_re.shape
    h = b_re.shape[3]
    n = depth * g
    ldt = jnp.broadcast_to(log_dt.reshape(n, 1, 1), (n, 1, p))
    brt = jnp.transpose(b_re, (0, 1, 3, 2)).reshape(n, h, p)
    bit = jnp.transpose(b_im, (0, 1, 3, 2)).reshape(n, h, p)
    sd = jax.ShapeDtypeStruct
    lbr, lbi, bbr, bbi = pl.pallas_call(
        _ssm_param_body,
        out_shape=[sd((n, 1, p), F32), sd((n, 1, p), F32), sd((n, h, p), F32), sd((n, h, p), F32)],
        name="ssm_params",
    )(lam_re.reshape(n, 1, p), lam_im.reshape(n, 1, p), ldt, brt, bit)
    return (lbr.reshape(depth, 1, g * p), lbi.reshape(depth, 1, g * p),
            bbr.reshape(depth, g, h, p), bbi.reshape(depth, g, h, p))


def _block_diag_halves(w):
    depth, g, a, b = w.shape
    hg = g // 2
    wh = w.reshape(depth, 2, hg, a, b)
    eye = jnp.eye(hg, dtype=bool)[None, None, :, None, :, None]
    full = jnp.where(eye, wh[:, :, :, :, None, :], 0.0)
    return full.reshape(depth, 2, hg * a, hg * b)


def _ssm_body(u_ref, perm_ref, permt_ref, bre_ref, bim_ref, cre_ref, cim_ref, lr_ref, li_ref, d_ref,
              wglu_ref, bglu_ref, o_ref, xr0, xi0, ub0, xr1, xi1, ub1, sr, si, *, steps, batch, lane_chunk):
    s = pl.program_id(0)

    @pl.when(s == 0)
    def _():
        for ref in (xr0, xi0, ub0, xr1, xi1, ub1, sr, si):
            ref[...] = jnp.zeros_like(ref)

    sub = SSM_PERM_STEPS
    rows_g = sub * batch

    def stage(fill, scan):
        xr_f, xi_f, ub_f = fill
        xr, xi, ub = scan
        parts = []
        for g in range(steps // sub):
            bm = jnp.concatenate([u_ref[b, g * sub:(g + 1) * sub, :] for b in range(batch)], axis=0)
            parts.append(jnp.dot(perm_ref[...], bm, preferred_element_type=F32).astype(BF16))
        u_new = jnp.concatenate(parts, axis=0)
        ub_f[...] = u_new
        hw = u_new.shape[1] // 2
        hs = xr.shape[1] // 2
        for hf in range(2):
            uh = u_new[:, hf * hw:(hf + 1) * hw]
            xr_f[:, hf * hs:(hf + 1) * hs] = jnp.dot(uh, bre_ref[hf], preferred_element_type=F32)
            xi_f[:, hf * hs:(hf + 1) * hs] = jnp.dot(uh, bim_ref[hf], preferred_element_type=F32)
        for c in range(xr.shape[1] // lane_chunk):
            cs = slice(c * lane_chunk, (c + 1) * lane_chunk)
            ar = lr_ref[:, cs]
            ai = li_ref[:, cs]
            pr, pi = sr[:, cs], si[:, cs]
            for t in range(steps):
                rows = slice(t * batch, (t + 1) * batch)
                nr = ar * pr - ai * pi + xr[rows, cs]
                ni = ar * pi + ai * pr + xi[rows, cs]
                xr[rows, cs] = nr
                xi[rows, cs] = ni
                pr, pi = nr, ni
            sr[:, cs] = pr
            si[:, cs] = pi
        ys = []
        for hf in range(2):
            xrb = xr[:, hf * hs:(hf + 1) * hs].astype(BF16)
            xib = xi[:, hf * hs:(hf + 1) * hs].astype(BF16)
            ys.append(jnp.dot(xrb, cre_ref[hf], preferred_element_type=F32)
                      - jnp.dot(xib, cim_ref[hf], preferred_element_type=F32))
        y = jnp.concatenate(ys, axis=1) + d_ref[...] * ub[...].astype(F32)
        y = _gelu_tanh(y)
        z = jnp.dot(y.astype(BF16), wglu_ref[...], preferred_element_type=F32) + bglu_ref[...]
        out = (y * _sigmoid(z)).astype(BF16)
        for g in range(steps // sub):
            bm = jnp.dot(permt_ref[...], out[g * rows_g:(g + 1) * rows_g], preferred_element_type=F32).astype(BF16)
            for b in range(batch):
                o_ref[b, g * sub:(g + 1) * sub, :] = bm[b * sub:(b + 1) * sub]

    @pl.when(s % 2 == 0)
    def _():
        stage((xr0, xi0, ub0), (xr1, xi1, ub1))

    @pl.when(s % 2 == 1)
    def _():
        stage((xr1, xi1, ub1), (xr0, xi0, ub0))


def _time_major_permutation(batch, sub):
    n = batch * sub
    r = np.arange(n)
    p = np.zeros((n, n), np.float32)
    p[r, (r % batch) * sub + r // batch] = 1.0
    return p


def _ssm(u, bre, bim, cre, cim, lbr, lbi, d, wglu, bglu, *, batch, steps, lane_chunk):
    t, w = u.shape
    seq = t // batch
    n = seq // steps
    ns = lbr.arr.shape[2]
    rows = steps * batch
    perm = _time_major_permutation(batch, SSM_PERM_STEPS)
    const2 = lambda i: (0, 0)
    params = (bre, bim, cre, cim, lbr, lbi, d, wglu, bglu)
    buf = [pltpu.VMEM((rows, ns), F32), pltpu.VMEM((rows, ns), F32), pltpu.VMEM((rows, w), BF16)]
    out = pl.pallas_call(
        functools.partial(_ssm_body, steps=steps, batch=batch, lane_chunk=lane_chunk),
        grid=(n + 1,),
        in_specs=[pl.BlockSpec((batch, steps, w), lambda i: (0, jnp.minimum(i, n - 1), 0)),
                  pl.BlockSpec(perm.shape, const2), pl.BlockSpec(perm.shape, const2)]
        + [_layer_spec(p) for p in params],
        out_specs=pl.BlockSpec((batch, steps, w), lambda i: (0, jnp.maximum(i - 1, 0), 0)),
        out_shape=jax.ShapeDtypeStruct((batch, seq, w), BF16),
        scratch_shapes=buf + buf + [pltpu.VMEM((batch, ns), F32), pltpu.VMEM((batch, ns), F32)],
        compiler_params=_cparams(("arbitrary",)),
        name="s5_scan",
    )(u.reshape(batch, seq, w), jnp.asarray(perm, BF16), jnp.asarray(perm.T, BF16), *[p.arr for p in params])
    return out.reshape(t, w)


def _split_bf16(v):
    hi = v.astype(BF16)
    lo = (v - hi.astype(F32)).astype(BF16)
    return hi, lo


def _merge_body(x_ref, a_ref, s_ref, wg_ref, wba_ref, wbs_ref, wo_ref, g_ref, b_ref, *rest, with_router):
    if with_router:
        rcat_ref, x1_ref, route_ref = rest
    else:
        (x1_ref,) = rest
    tm, d = x_ref.shape
    rg = tm // MERGE_ROW_GROUPS
    for grp in range(MERGE_ROW_GROUPS):
        rows = slice(grp * rg, (grp + 1) * rg)
        x = x_ref[rows, :]
        gates = jnp.dot(x.astype(BF16), wg_ref[...], preferred_element_type=F32)
        pa = jnp.dot(a_ref[rows, :], wba_ref[...], preferred_element_type=F32)
        ps = jnp.dot(s_ref[rows, :], wbs_ref[...], preferred_element_type=F32)
        merged = _sigmoid(gates[:, :d]) * pa + _sigmoid(gates[:, d:]) * ps
        y = jnp.dot(merged.astype(BF16), wo_ref[...], preferred_element_type=F32)
        x1 = _layer_norm(DEEPNORM_ALPHA * x + y, g_ref[...], b_ref[...])
        x1_ref[rows, :] = x1
        if with_router:
            hi, lo = _split_bf16(x1)
            both = jnp.dot(hi, rcat_ref[...], preferred_element_type=F32)
            logits = (both[:, :LANES] + jnp.dot(lo, rcat_ref[:, :LANES], preferred_element_type=F32)
                      + both[:, LANES:])
            lane = lax.broadcasted_iota(jnp.int32, logits.shape, 1).astype(F32)
            big = float(LANES)
            l1 = jnp.where(lane < N_EXPERTS, logits, -jnp.inf)
            m1 = jnp.max(l1, axis=-1, keepdims=True)
            i1 = jnp.min(jnp.where(l1 == m1, lane, big), axis=-1, keepdims=True)
            l2 = jnp.where(lane == i1, -jnp.inf, l1)
            m2 = jnp.max(l2, axis=-1, keepdims=True)
            i2 = jnp.min(jnp.where(l2 == m2, lane, big), axis=-1, keepdims=True)
            ed = jnp.exp(m2 - m1)
            g1 = 1.0 / (1.0 + ed)
            g2 = ed / (1.0 + ed)
            route = jnp.where(lane == 0.0, i1,
                              jnp.where(lane == 1.0, i2,
                                        jnp.where(lane == 2.0, g1, jnp.where(lane == 3.0, g2, 0.0))))
            route_ref[rows, :] = route


def _merge(x, a, s, wg, wba, wbs, wo, g, b, router=None, *, tm):
    t, d = x.shape
    row = lambda i: (i, 0)
    params = [wg, wba, wbs, wo, g, b] + list(router or ())
    in_specs = [pl.BlockSpec((tm, d), row), pl.BlockSpec((tm, a.shape[1]), row), pl.BlockSpec((tm, s.shape[1]), row)]
    in_specs += [_layer_spec(p) for p in params]
    args = [x, a, s] + [p.arr for p in params]
    out_specs = [pl.BlockSpec((tm, d), row)]
    out_shape = [jax.ShapeDtypeStruct((t, d), F32)]
    if router is not None:
        out_specs.append(pl.BlockSpec((tm, LANES), row))
        out_shape.append(jax.ShapeDtypeStruct((t, LANES), F32))
    return pl.pallas_call(
        functools.partial(_merge_body, with_router=router is not None),
        grid=(t // tm,),
        in_specs=in_specs, out_specs=out_specs, out_shape=out_shape,
        compiler_params=_cparams(("parallel",)),
        name="merge_ln1",
    )(*args)


def _ple(xb, p_ref, wpg_ref, wpp_ref):
    gate = _sigmoid(jnp.dot(xb, wpg_ref[...], preferred_element_type=F32))
    return gate * jnp.dot(p_ref[...].astype(BF16), wpp_ref[...], preferred_element_type=F32)


def _ffn_body(x_ref, p_ref, wg_ref, wu_ref, wd_ref, wpg_ref, wpp_ref, g_ref, b_ref, o_ref, acc):
    x1 = x_ref[...]
    xb = x1.astype(BF16)
    for c, (c0, cw) in enumerate(FFN_CHUNKS):
        gt = jnp.dot(xb, wg_ref[:, c0:c0 + cw], preferred_element_type=F32)
        up = jnp.dot(xb, wu_ref[:, c0:c0 + cw], preferred_element_type=F32)
        act = (gt * _sigmoid(gt) * up).astype(BF16)
        contrib = jnp.dot(act, wd_ref[c0:c0 + cw, :], preferred_element_type=F32)
        if c == 0:
            acc[...] = contrib
        else:
            acc[...] += contrib
    h = DEEPNORM_ALPHA * x1 + acc[...] + _ple(xb, p_ref, wpg_ref, wpp_ref)
    o_ref[...] = _layer_norm(h, g_ref[...], b_ref[...])


def _resident(shape, index_map):
    return pl.BlockSpec(shape, index_map, pipeline_mode=pl.Buffered(1))


def _ple_rows_spec(p, tm):
    return pl.BlockSpec((None, tm, p.arr.shape[2]), lambda i: (p.idx, i, 0))


def _ffn(x1, p, w_gu, w_down, li, wpg, wpp, g, b, *, tm):
    t, d = x1.shape
    ff = w_down.shape[1]
    row = lambda i: (i, 0)
    return pl.pallas_call(
        _ffn_body,
        grid=(t // tm,),
        in_specs=[pl.BlockSpec((tm, d), row), _ple_rows_spec(p, tm),
                  _resident((None, d, ff), lambda i: (li, 0, 0)),
                  _resident((None, d, ff), lambda i: (li, 0, 1)),
                  _resident((None, ff, d), lambda i: (li, 0, 0)),
                  _layer_spec(wpg, resident=True), _layer_spec(wpp, resident=True),
                  _layer_spec(g), _layer_spec(b)],
        out_specs=pl.BlockSpec((tm, d), row),
        out_shape=jax.ShapeDtypeStruct((t, d), F32),
        scratch_shapes=[pltpu.VMEM((tm, d), F32)],
        compiler_params=_cparams(("parallel",)),
        name="ffn_ln2",
    )(x1, p.arr, w_gu, w_gu, w_down, wpg.arr, wpp.arr, g.arr, b.arr)


def _moe_body(te_ref, nu_ref, src_ref, dst_ref, x_hbm, wgu_ref, wd_ref, y_hbm,
              xg, xb, acc, ybuf, gsem, ssem, *, tm, fw, spare_row):
    i = pl.program_id(0)
    nu = nu_ref[0]
    slot = i % 2
    other = 1 - slot
    ffe = wd_ref.shape[0]
    nc = ffe // fw
    per_chunk = tm // nc

    def gather(tile, r, buf):
        row = src_ref[tile * tm + r]
        pltpu.make_async_copy(x_hbm.at[pl.ds(row, 1), :], xg.at[buf, pl.ds(r, 1), :], gsem.at[buf]).start()

    def scatter(row, r, buf):
        pltpu.make_async_copy(ybuf.at[buf, pl.ds(r, 1), :], y_hbm.at[pl.ds(row, 1), :], ssem.at[buf]).start()

    def prev_tile_dst(r):
        return jnp.where(i > 0, dst_ref[jnp.maximum(i - 1, 0) * tm + r], spare_row + r)

    @pl.when(i == 0)
    def _():
        ybuf[...] = jnp.zeros_like(ybuf)

        def one(r, c):
            gather(0, r, 0)
            return c
        lax.fori_loop(0, tm, one, 0)

    @pl.when(i <= nu)
    def _():
        pltpu.make_async_copy(x_hbm.at[pl.ds(0, tm), :], xg.at[slot], gsem.at[slot]).wait()

    @pl.when(i < nu)
    def _():
        xb[...] = xg[slot].astype(BF16)
        x = xb[...]
        for c in range(nc):
            c0 = c * fw
            gt = jnp.dot(x, wgu_ref[:, c0:c0 + fw], preferred_element_type=F32)
            up = jnp.dot(x, wgu_ref[:, ffe + c0:ffe + c0 + fw], preferred_element_type=F32)
            act = (gt * _sigmoid(gt) * up).astype(BF16)
            part = jnp.dot(act, wd_ref[c0:c0 + fw, :], preferred_element_type=F32)
            if c == 0:
                acc[...] = part
            else:
                acc[...] += part
            r1 = tm if c == nc - 1 else (c + 1) * per_chunk
            for r in range(c * per_chunk, r1):
                gather(i + 1, r, other)
                scatter(prev_tile_dst(r), r, other)

    @pl.when(i == nu)
    def _():
        def one(r, c):
            scatter(dst_ref[(i - 1) * tm + r], r, other)
            return c
        lax.fori_loop(0, tm, one, 0)

    @pl.when((i >= 1) & (i <= nu + 1))
    def _():
        pltpu.make_async_copy(ybuf.at[slot], y_hbm.at[pl.ds(0, tm), :], ssem.at[slot]).wait()

    @pl.when(i < nu)
    def _():
        ybuf[slot] = _pack_bf16_pairs(acc[...])


def _moe_experts(x1, w_gu, w_down, li, tile_expert, n_used, src, dst, *, tm, fw, out_rows):
    t, d = x1.shape
    ffe = w_down.shape[2]
    nt = src.shape[0] // tm
    return pl.pallas_call(
        functools.partial(_moe_body, tm=tm, fw=fw, spare_row=out_rows - tm),
        grid_spec=pltpu.PrefetchScalarGridSpec(
            num_scalar_prefetch=4,
            grid=(nt,),
            in_specs=[pl.BlockSpec(memory_space=pl.ANY),
                      _resident((None, None, d, 2 * ffe), lambda i, te, nu, s, ds: (li, te[i], 0, 0)),
                      _resident((None, None, ffe, d), lambda i, te, nu, s, ds: (li, te[i], 0, 0))],
            out_specs=pl.BlockSpec(memory_space=pl.ANY),
            scratch_shapes=[pltpu.VMEM((2, tm, d), F32), pltpu.VMEM((tm, d), BF16), pltpu.VMEM((tm, d), F32),
                            pltpu.VMEM((2, tm, d // 2), jnp.uint32),
                            pltpu.SemaphoreType.DMA((2,)), pltpu.SemaphoreType.DMA((2,))]),
        out_shape=jax.ShapeDtypeStruct((out_rows, d // 2), jnp.uint32),
        compiler_params=_cparams(("arbitrary",)),
        name="moe_experts",
    )(tile_expert, n_used, src, dst, x1, w_gu, w_down)


def _moe_out_body(x_ref, route_ref, y0_ref, y1_ref, p_ref, wpg_ref, wpp_ref, g_ref, b_ref, o_ref):
    x1 = x_ref[...]
    route = route_ref[...]
    ffn = route[:, 2:3] * _unpack_bf16_pairs(y0_ref[...]) + route[:, 3:4] * _unpack_bf16_pairs(y1_ref[...])
    h = DEEPNORM_ALPHA * x1 + ffn + _ple(x1.astype(BF16), p_ref, wpg_ref, wpp_ref)
    o_ref[...] = _layer_norm(h, g_ref[...], b_ref[...])


def _moe_out(x1, route, p, y, wpg, wpp, g, b, *, tm):
    t, d = x1.shape
    nt = t // tm
    row = lambda i: (i, 0)
    return pl.pallas_call(
        _moe_out_body,
        grid=(nt,),
        in_specs=[pl.BlockSpec((tm, d), row), pl.BlockSpec((tm, LANES), row),
                  pl.BlockSpec((tm, d // 2), row), pl.BlockSpec((tm, d // 2), lambda i: (nt + i, 0)),
                  _ple_rows_spec(p, tm),
                  _layer_spec(wpg), _layer_spec(wpp), _layer_spec(g), _layer_spec(b)],
        out_specs=pl.BlockSpec((tm, d), row),
        out_shape=jax.ShapeDtypeStruct((t, d), F32),
        compiler_params=_cparams(("parallel",)),
        name="moe_combine_ln2",
    )(x1, route, y, y, p.arr, wpg.arr, wpp.arr, g.arr, b.arr)


def _slot_table_body(pos_ref, init_hbm, dst_ref, sem, *, t, groups):
    @pl.when(pl.program_id(0) == 0)
    def _():
        copy = pltpu.make_async_copy(init_hbm, dst_ref, sem)
        copy.start()
        copy.wait()

    batch = SLOT_TABLE_BATCH
    base = pl.program_id(0) * (t // groups)

    def put(kb, c):
        tok0 = base + kb * batch
        ps = [pos_ref[2 * tok0 + j] for j in range(2 * batch)]
        for j, p in enumerate(ps):
            dst_ref[p] = (j % 2) * t + tok0 + j // 2
        return c
    lax.fori_loop(0, t // groups // batch, put, 0)


def _slot_table(pos, *, t, tm, r):
    groups = SLOT_TABLE_GROUPS
    assert t % (groups * SLOT_TABLE_BATCH) == 0
    init = 2 * t + jnp.arange(r, dtype=jnp.int32) % tm
    smem = pl.BlockSpec(memory_space=pltpu.SMEM)
    return pl.pallas_call(
        functools.partial(_slot_table_body, t=t, groups=groups),
        grid=(groups,),
        in_specs=[smem, pl.BlockSpec(memory_space=pl.ANY)], out_specs=smem,
        out_shape=jax.ShapeDtypeStruct((r,), jnp.int32),
        scratch_shapes=[pltpu.SemaphoreType.DMA],
        compiler_params=_cparams(("arbitrary",)),
        name="moe_slot_table",
    )(pos, init)


def _routing_tables(route, *, tm, n_experts):
    t = route.shape[0]
    eids = route[:, :2].astype(jnp.int32).reshape(-1)
    onehot = (eids[:, None] == jnp.arange(n_experts, dtype=jnp.int32)[None, :]).astype(jnp.int32)
    csum = jnp.cumsum(onehot, axis=0)
    rank = jnp.sum((csum - onehot) * onehot, axis=1)
    cnt = csum[-1]
    ntile = (cnt + tm - 1) // tm
    tile_end = jnp.cumsum(ntile)
    row_off = (tile_end - ntile) * tm
    pos = jnp.sum(onehot * row_off[None, :], axis=1) + rank
    r = 2 * t + (n_experts + 2) * tm
    dst = _slot_table(pos.astype(jnp.int32), t=t, tm=tm, r=r)
    src = jnp.where(dst < 2 * t, dst % t, 0)
    n_used = tile_end[-1:]
    tiles = jnp.arange(r // tm, dtype=jnp.int32)
    te = jnp.sum((tiles[:, None] >= tile_end[None, :]).astype(jnp.int32), axis=1)
    last = jnp.sum((n_used - 1 >= tile_end).astype(jnp.int32))
    te = jnp.where(tiles < n_used, te, last).astype(jnp.int32)
    out_rows = 2 * t + tm
    return te, n_used.astype(jnp.int32), src.astype(jnp.int32), dst.astype(jnp.int32), out_rows


def kernel(x, p, rel_bias, w_in, attn_sinks, ssm_lambda_re, ssm_lambda_im, ssm_log_dt, ssm_b_re, ssm_b_im,
           ssm_c_re, ssm_c_im, ssm_d, w_glu, b_glu, w_branch_attn, w_branch_ssm, w_out, ln1_g, ln1_b,
           ffn_w_gate_up, ffn_w_down, moe_router, moe_w_gate_up, moe_w_down, ple_w_proj, ple_w_gate,
           ln2_g, ln2_b):
    bsz, seq, d = x.shape
    t = bsz * seq
    depth = w_in.shape[0]

    xt = x.reshape(t, d)
    pt = p.reshape(depth, t, p.shape[-1])

    bias = _attn_bias_table(rel_bias)
    sinks = attn_sinks * LOG2_E
    ffn_gu = ffn_w_gate_up.astype(BF16)
    ffn_dn = ffn_w_down.astype(BF16)
    moe_gu = moe_w_gate_up.astype(BF16)
    moe_dn = moe_w_down.astype(BF16)

    aw, kw = ATTN_WIDTH, KV_WIDTH
    u0 = aw + 2 * kw
    w_proj = jnp.concatenate([_q_head_tiles(w_in[:, :, :aw]), w_in[:, :, aw:u0 + SSM_WIDTH]], axis=2).astype(BF16)
    w_gates = w_in[:, :, u0 + SSM_WIDTH:].astype(BF16)
    wba = w_branch_attn.reshape(depth, N_Q_HEADS, HEAD_DIM, d)[:, np.array(_attn_out_head_order())]
    wba = wba.reshape(depth, aw, d).astype(BF16)
    wbs, wo = w_branch_ssm.astype(BF16), w_out.astype(BF16)
    wpg, wpp = ple_w_gate.astype(BF16), ple_w_proj.astype(BF16)
    g1, b1 = ln1_g.reshape(depth, 1, d), ln1_b.reshape(depth, 1, d)
    g2, b2 = ln2_g.reshape(depth, 1, d), ln2_b.reshape(depth, 1, d)

    lbr, lbi, bbr, bbi = _ssm_params(ssm_lambda_re, ssm_lambda_im, ssm_log_dt, ssm_b_re, ssm_b_im)
    bre = _block_diag_halves(bbr).astype(BF16)
    bim = _block_diag_halves(bbi).astype(BF16)
    cre = _block_diag_halves(jnp.transpose(ssm_c_re, (0, 1, 3, 2))).astype(BF16)
    cim = _block_diag_halves(jnp.transpose(ssm_c_im, (0, 1, 3, 2))).astype(BF16)
    ssm_skip = ssm_d.reshape(depth, 1, -1)
    wglu, bglu = w_glu.astype(BF16), b_glu.reshape(depth, 1, -1)

    rt = jnp.pad(moe_router, ((0, 0), (0, 0), (0, LANES - N_EXPERTS)))
    rhi = rt.astype(BF16)
    rcat = jnp.concatenate([rhi, (rt - rhi.astype(F32)).astype(BF16)], axis=2)

    for i in range(depth):
        at = lambda arr, idx=i: _Layer(arr, idx)
        q, kv, u = _inproj(xt, at(w_proj), tm=TM_PROJ)
        a_out = _attention(q, kv, at(sinks), bias, seq=seq, batch=bsz)
        s_out = _ssm(u, at(bre), at(bim), at(cre), at(cim), at(lbr), at(lbi), at(ssm_skip), at(wglu), at(bglu),
                     batch=bsz, steps=SSM_STEPS, lane_chunk=SSM_LANE_CHUNK)
        merge_w = (at(w_gates), at(wba), at(wbs), at(wo), at(g1), at(b1))
        if i % 2 == 0:
            (x1,) = _merge(xt, a_out, s_out, *merge_w, tm=TM_MERGE)
            xt = _ffn(x1, at(pt), ffn_gu, ffn_dn, i // 2, at(wpg), at(wpp), at(g2), at(b2), tm=TM_FFN)
        else:
            router = (_Layer(rcat, i // 2),)
            x1, route = _merge(xt, a_out, s_out, *merge_w, router, tm=TM_MERGE)
            te, n_used, src, dst, out_rows = _routing_tables(route, tm=TM_MOE, n_experts=N_EXPERTS)
            y = _moe_experts(x1, moe_gu, moe_dn, i // 2, te, n_used, src, dst, tm=TM_MOE, fw=FW_MOE,
                             out_rows=out_rows)
            xt = _moe_out(x1, route, at(pt), y, at(wpg), at(wpp), at(g2), at(b2), tm=TM_FFN)

    return xt.reshape(bsz, seq, d)
```

```python
import functools
import math
from typing import NamedTuple

import jax
import jax.numpy as jnp
import numpy as np
from jax import lax
from jax.experimental import pallas as pl
from jax.experimental.pallas import tpu as pltpu

F32 = jnp.float32
BF16 = jnp.bfloat16

D_MODEL = 1024
BATCH = 16
SEQ = 2048
DEPTH = 4
HEAD_DIM = 64
N_Q_HEADS = 8
N_KV_HEADS = 2
ATTN_WIDTH = N_Q_HEADS * HEAD_DIM
KV_WIDTH = N_KV_HEADS * HEAD_DIM
ATTN_BLOCK = 128
N_BUCKETS = 32
MAX_DISTANCE = 128
SSM_WIDTH = 512
SSM_GROUP_CH = 16
SSM_GROUPS = SSM_WIDTH // SSM_GROUP_CH
SSM_STATE = 64
D_FF = 2816
N_EXPERTS = 8
D_FF_EXPERT = 3584
PLE_DIM = 256
DEEPNORM_ALPHA = (2 * DEPTH) ** 0.25
LN_EPS = 1e-5
NEG_INF = -1e30
LOG2_E = math.log2(math.e)

LANES = 128
VMEM_LIMIT_BYTES = 56 * 1024 * 1024

TM_PROJ = 1024
ATTN_BLOCKS_PER_STEP = 8
TM_MERGE = 512
MERGE_ROW_GROUPS = 2
TM_FFN = 512
TM_MOE = 512
FW_MOE = 512
SLOT_TABLE_GROUPS = 16
SLOT_TABLE_BATCH = 8
SSM_STEPS = 32
SSM_PERM_STEPS = 16
SSM_LANE_CHUNK = 512
FFN_CHUNKS = ((0, 768), (768, 768), (1536, 768), (2304, 512))


def _cparams(sem):
    return pltpu.CompilerParams(dimension_semantics=sem, vmem_limit_bytes=VMEM_LIMIT_BYTES)


class _Layer(NamedTuple):
    arr: jax.Array
    idx: int


def _layer_spec(p, resident=False):
    tail = (0,) * (p.arr.ndim - 1)
    mode = {"pipeline_mode": pl.Buffered(1)} if resident else {}
    return pl.BlockSpec((None,) + p.arr.shape[1:], lambda *_: (p.idx,) + tail, **mode)


def _layer_norm(h, g, b):
    mu = jnp.mean(h, axis=-1, keepdims=True)
    c = h - mu
    var = jnp.mean(c * c, axis=-1, keepdims=True)
    return c * lax.rsqrt(var + LN_EPS) * g + b


def _sigmoid(v):
    return 1.0 / (1.0 + jnp.exp(-v))


def _pack_bf16_pairs(v):
    c = v.shape[1] // 2
    lo = lax.bitcast_convert_type(v[:, :c].astype(BF16).astype(F32), jnp.uint32)
    hi = lax.bitcast_convert_type(v[:, c:].astype(BF16).astype(F32), jnp.uint32)
    return (lo >> 16) | (hi & jnp.uint32(0xFFFF0000))


def _unpack_bf16_pairs(w):
    lo = lax.bitcast_convert_type(w << 16, F32)
    hi = lax.bitcast_convert_type(w & jnp.uint32(0xFFFF0000), F32)
    return jnp.concatenate([lo, hi], axis=1)


def _gelu_tanh(v):
    return 0.5 * v * (1.0 + jnp.tanh(math.sqrt(2.0 / math.pi) * (v + 0.044715 * (v * v * v))))


def _inproj_body(x_ref, w_ref, q_ref, kv_ref, u_ref):
    _project(x_ref[...], w_ref, q_ref, kv_ref, u_ref)


def _project(x, w_ref, q_ref, kv_ref, u_ref):
    z = jnp.dot(x.astype(BF16), w_ref[...], preferred_element_type=F32)
    aw = q_ref.shape[1]
    kw = kv_ref.shape[1]
    q_ref[...] = (z[:, :aw] * (LOG2_E * HEAD_DIM ** -0.5)).astype(BF16)
    kv_ref[...] = z[:, aw:aw + kw].astype(BF16)
    u_ref[...] = z[:, aw + kw:].astype(BF16)


def _proj_outputs(t, tm, n):
    aw, kw = N_Q_HEADS * LANES, 2 * KV_WIDTH
    widths = (aw, kw, n - aw - kw)
    specs = [pl.BlockSpec((tm, w), lambda i: (i, 0)) for w in widths]
    shapes = [jax.ShapeDtypeStruct((t, w), BF16) for w in widths]
    return specs, shapes


def _inproj(x, w, *, tm):
    t, d = x.shape
    n = w.arr.shape[2]
    aw, kw = N_Q_HEADS * LANES, 2 * KV_WIDTH
    uw = n - aw - kw
    return pl.pallas_call(
        _inproj_body,
        grid=(t // tm,),
        in_specs=[pl.BlockSpec((tm, d), lambda i: (i, 0)), _layer_spec(w)],
        out_specs=[pl.BlockSpec((tm, aw), lambda i: (i, 0)),
                   pl.BlockSpec((tm, kw), lambda i: (i, 0)),
                   pl.BlockSpec((tm, uw), lambda i: (i, 0))],
        out_shape=[jax.ShapeDtypeStruct((t, aw), BF16),
                   jax.ShapeDtypeStruct((t, kw), BF16),
                   jax.ShapeDtypeStruct((t, uw), BF16)],
        compiler_params=_cparams(("parallel",)),
        name="inproj",
    )(x, w.arr)


def _attn_body(sink_ref, q_ref, kvc_ref, kvp_ref, bias0_ref, bias_ref, o_ref, *, blk, nbs, layer):
    group = N_Q_HEADS // N_KV_HEADS
    ks = slice(0, LANES)
    vs = slice(LANES, 2 * LANES)
    low = lax.broadcasted_iota(jnp.int32, (blk, LANES), 1) < HEAD_DIM
    for j in range(nbs):
        rows = slice(j * blk, (j + 1) * blk)
        b_ref = bias0_ref if j == 0 else bias_ref
        if j == 0:
            k = jnp.concatenate([kvp_ref[:, ks], kvc_ref[0:blk, ks]], axis=0)
            v = jnp.concatenate([kvp_ref[:, vs], kvc_ref[0:blk, vs]], axis=0)
        else:
            k = kvc_ref[(j - 1) * blk:(j + 1) * blk, ks]
            v = kvc_ref[(j - 1) * blk:(j + 1) * blk, vs]
        for tile in range(group):
            halves = []
            for g in range(N_KV_HEADS):
                h = tile + group * g
                qt = q_ref[rows, h * LANES:(h + 1) * LANES]
                s = lax.dot_general(qt, k, (((1,), (1,)), ((), ())), preferred_element_type=F32)
                s = s + b_ref[h]
                sink = sink_ref[layer, h]
                m = jnp.maximum(jnp.max(s, axis=-1, keepdims=True), sink)
                e = jnp.exp2(s - m)
                denom = jnp.sum(e, axis=-1, keepdims=True) + jnp.exp2(sink - m)
                halves.append(jnp.dot(e.astype(BF16), v, preferred_element_type=F32) / denom)
            o_ref[rows, tile * LANES:(tile + 1) * LANES] = jnp.where(low, halves[0], halves[1]).astype(BF16)


def _attention(q, kv, sinks, bias, *, seq, batch):
    assert KV_WIDTH == LANES
    t = q.shape[0]
    blk = ATTN_BLOCK
    nbs = ATTN_BLOCKS_PER_STEP
    ns = seq // (blk * nbs)
    qw, kw, aw = q.shape[1], kv.shape[1], ATTN_WIDTH
    table = (None, N_Q_HEADS, blk, 2 * blk)
    return pl.pallas_call(
        functools.partial(_attn_body, blk=blk, nbs=nbs, layer=sinks.idx),
        grid=(batch, ns),
        in_specs=[pl.BlockSpec(memory_space=pltpu.SMEM),
                  pl.BlockSpec((blk * nbs, qw), lambda b, n: (b * ns + n, 0)),
                  pl.BlockSpec((blk * nbs, kw), lambda b, n: (b * ns + n, 0)),
                  pl.BlockSpec((blk, kw), lambda b, n: ((b * ns + n) * nbs - jnp.minimum(n, 1), 0)),
                  pl.BlockSpec(table, lambda b, n: (jnp.minimum(n, 1), 0, 0, 0)),
                  pl.BlockSpec(table, lambda b, n: (1, 0, 0, 0))],
        out_specs=pl.BlockSpec((blk * nbs, aw), lambda b, n: (b * ns + n, 0)),
        out_shape=jax.ShapeDtypeStruct((t, aw), BF16),
        compiler_params=_cparams(("parallel", "parallel")),
        name="swa_attention",
    )(sinks.arr, q, kv, kv, bias, bias)


def _q_head_tiles(w):
    depth, d, _ = w.shape
    group = N_Q_HEADS // N_KV_HEADS
    wh = w.reshape(depth, d, N_Q_HEADS, 1, HEAD_DIM)
    half = jnp.arange(N_KV_HEADS)[None, None, None, :, None]
    kv_head = (jnp.arange(N_Q_HEADS) // group)[None, None, :, None, None]
    tiles = jnp.where(half == kv_head, wh, 0.0)
    return tiles.reshape(depth, d, N_Q_HEADS * LANES)


def _attn_out_head_order():
    group = N_Q_HEADS // N_KV_HEADS
    return [j + group * half for j in range(group) for half in range(N_KV_HEADS)]


def _attn_bias_table(rel_bias):
    blk = ATTN_BLOCK
    d = np.arange(blk, dtype=np.int32)
    max_exact = N_BUCKETS // 2
    d_f = np.maximum(d, 1).astype(np.float32)
    large = max_exact + (np.log(d_f / np.float32(max_exact)) / np.float32(math.log(MAX_DISTANCE / max_exact))
                         * np.float32(N_BUCKETS - max_exact)).astype(np.int32)
    large = np.minimum(large, N_BUCKETS - 1)
    bucket = np.where(d < max_exact, d, large)
    vals = jnp.transpose(rel_bias.astype(F32)[bucket], (1, 0))
    h = vals.shape[0]
    neg = jnp.full((h, blk), NEG_INF, F32)
    strip = jnp.concatenate([neg, vals[:, ::-1], neg], axis=1)
    rows = [strip[:, blk - 1 - i:blk - 1 - i + 2 * blk] for i in range(blk)]
    table = jnp.stack(rows, axis=1) * LOG2_E
    first = jnp.where(np.arange(2 * blk)[None, None, :] < blk, NEG_INF, table)
    return jnp.stack([first, table], axis=0)


def _ssm_param_body(lre_ref, lim_ref, ldt_ref, bre_ref, bim_ref, lbr_ref, lbi_ref, bbr_ref, bbi_ref):
    lre, lim = lre_ref[...], lim_ref[...]
    dt = jnp.exp(ldt_ref[...])
    mag = jnp.exp(lre * dt)
    lbr = mag * jnp.cos(lim * dt)
    lbi = mag * jnp.sin(lim * dt)
    lbr_ref[...] = lbr
    lbi_ref[...] = lbi
    nr, ni = lbr - 1.0, lbi
    inv = 1.0 / (lre * lre + lim * lim)
    cr = (nr * lre + ni * lim) * inv
    ci = (ni * lre - nr * lim) * inv
    br, bi = bre_ref[...], bim_ref[...]
    bbr_ref[...] = cr * br - ci * bi
    bbi_ref[...] = cr * bi + ci * br


def _ssm_params(lam_re, lam_im, log_dt, b_re, b_im):
    depth, g, p = lam_re.shape
    h = b_re.shape[3]
    n = depth * g
    ldt = jnp.broadcast_to(log_dt.reshape(n, 1, 1), (n, 1, p))
    brt = jnp.transpose(b_re, (0, 1, 3, 2)).reshape(n, h, p)
    bit = jnp.transpose(b_im, (0, 1, 3, 2)).reshape(n, h, p)
    sd = jax.ShapeDtypeStruct
    lbr, lbi, bbr, bbi = pl.pallas_call(
        _ssm_param_body,
        out_shape=[sd((n, 1, p), F32), sd((n, 1, p), F32), sd((n, h, p), F32), sd((n, h, p), F32)],
        name="ssm_params",
    )(lam_re.reshape(n, 1, p), lam_im.reshape(n, 1, p), ldt, brt, bit)
    return (lbr.reshape(depth, 1, g * p), lbi.reshape(depth, 1, g * p),
            bbr.reshape(depth, g, h, p), bbi.reshape(depth, g, h, p))


def _block_diag_halves(w):
    depth, g, a, b = w.shape
    hg = g // 2
    wh = w.reshape(depth, 2, hg, a, b)
    eye = jnp.eye(hg, dtype=bool)[None, None, :, None, :, None]
    full = jnp.where(eye, wh[:, :, :, :, None, :], 0.0)
    return full.reshape(depth, 2, hg * a, hg * b)


def _ssm_body(u_ref, perm_ref, permt_ref, bre_ref, bim_ref, cre_ref, cim_ref, lr_ref, li_ref, d_ref,
              wglu_ref, bglu_ref, o_ref, xr0, xi0, ub0, xr1, xi1, ub1, sr, si, *, steps, batch, lane_chunk):
    s = pl.program_id(0)

    @pl.when(s == 0)
    def _():
        for ref in (xr0, xi0, ub0, xr1, xi1, ub1, sr, si):
            ref[...] = jnp.zeros_like(ref)

    sub = SSM_PERM_STEPS
    rows_g = sub * batch

    def stage(fill, scan):
        xr_f, xi_f, ub_f = fill
        xr, xi, ub = scan
        parts = []
        for g in range(steps // sub):
            bm = jnp.concatenate([u_ref[b, g * sub:(g + 1) * sub, :] for b in range(batch)], axis=0)
            parts.append(jnp.dot(perm_ref[...], bm, preferred_element_type=F32).astype(BF16))
        u_new = jnp.concatenate(parts, axis=0)
        ub_f[...] = u_new
        hw = u_new.shape[1] // 2
        hs = xr.shape[1] // 2
        for hf in range(2):
            uh = u_new[:, hf * hw:(hf + 1) * hw]
            xr_f[:, hf * hs:(hf + 1) * hs] = jnp.dot(uh, bre_ref[hf], preferred_element_type=F32)
            xi_f[:, hf * hs:(hf + 1) * hs] = jnp.dot(uh, bim_ref[hf], preferred_element_type=F32)
        for c in range(xr.shape[1] // lane_chunk):
            cs = slice(c * lane_chunk, (c + 1) * lane_chunk)
            ar = lr_ref[:, cs]
            ai = li_ref[:, cs]
            pr, pi = sr[:, cs], si[:, cs]
            for t in range(steps):
                rows = slice(t * batch, (t + 1) * batch)
                nr = ar * pr - ai * pi + xr[rows, cs]
                ni = ar * pi + ai * pr + xi[rows, cs]
                xr[rows, cs] = nr
                xi[rows, cs] = ni
                pr, pi = nr, ni
            sr[:, cs] = pr
            si[:, cs] = pi
        ys = []
        for hf in range(2):
            xrb = xr[:, hf * hs:(hf + 1) * hs].astype(BF16)
            xib = xi[:, hf * hs:(hf + 1) * hs].astype(BF16)
            ys.append(jnp.dot(xrb, cre_ref[hf], preferred_element_type=F32)
                      - jnp.dot(xib, cim_ref[hf], preferred_element_type=F32))
        y = jnp.concatenate(ys, axis=1) + d_ref[...] * ub[...].astype(F32)
        y = _gelu_tanh(y)
        z = jnp.dot(y.astype(BF16), wglu_ref[...], preferred_element_type=F32) + bglu_ref[...]
        out = (y * _sigmoid(z)).astype(BF16)
        for g in range(steps // sub):
            bm = jnp.dot(permt_ref[...], out[g * rows_g:(g + 1) * rows_g], preferred_element_type=F32).astype(BF16)
            for b in range(batch):
                o_ref[b, g * sub:(g + 1) * sub, :] = bm[b * sub:(b + 1) * sub]

    @pl.when(s % 2 == 0)
    def _():
        stage((xr0, xi0, ub0), (xr1, xi1, ub1))

    @pl.when(s % 2 == 1)
    def _():
        stage((xr1, xi1, ub1), (xr0, xi0, ub0))


def _time_major_permutation(batch, sub):
    n = batch * sub
    r = np.arange(n)
    p = np.zeros((n, n), np.float32)
    p[r, (r % batch) * sub + r // batch] = 1.0
    return p


def _ssm(u, bre, bim, cre, cim, lbr, lbi, d, wglu, bglu, *, batch, steps, lane_chunk):
    t, w = u.shape
    seq = t // batch
    n = seq // steps
    ns = lbr.arr.shape[2]
    rows = steps * batch
    perm = _time_major_permutation(batch, SSM_PERM_STEPS)
    const2 = lambda i: (0, 0)
    params = (bre, bim, cre, cim, lbr, lbi, d, wglu, bglu)
    buf = [pltpu.VMEM((rows, ns), F32), pltpu.VMEM((rows, ns), F32), pltpu.VMEM((rows, w), BF16)]
    out = pl.pallas_call(
        functools.partial(_ssm_body, steps=steps, batch=batch, lane_chunk=lane_chunk),
        grid=(n + 1,),
        in_specs=[pl.BlockSpec((batch, steps, w), lambda i: (0, jnp.minimum(i, n - 1), 0)),
                  pl.BlockSpec(perm.shape, const2), pl.BlockSpec(perm.shape, const2)]
        + [_layer_spec(p) for p in params],
        out_specs=pl.BlockSpec((batch, steps, w), lambda i: (0, jnp.maximum(i - 1, 0), 0)),
        out_shape=jax.ShapeDtypeStruct((batch, seq, w), BF16),
        scratch_shapes=buf + buf + [pltpu.VMEM((batch, ns), F32), pltpu.VMEM((batch, ns), F32)],
        compiler_params=_cparams(("arbitrary",)),
        name="s5_scan",
    )(u.reshape(batch, seq, w), jnp.asarray(perm, BF16), jnp.asarray(perm.T, BF16), *[p.arr for p in params])
    return out.reshape(t, w)


def _split_bf16(v):
    hi = v.astype(BF16)
    lo = (v - hi.astype(F32)).astype(BF16)
    return hi, lo


def _merge_body(x_ref, a_ref, s_ref, wg_ref, wba_ref, wbs_ref, wo_ref, g_ref, b_ref, *rest, with_router):
    if with_router:
        rcat_ref, x1_ref, route_ref = rest
    else:
        (x1_ref,) = rest
    tm, d = x_ref.shape
    rg = tm // MERGE_ROW_GROUPS
    for grp in range(MERGE_ROW_GROUPS):
        rows = slice(grp * rg, (grp + 1) * rg)
        x = x_ref[rows, :]
        gates = jnp.dot(x.astype(BF16), wg_ref[...], preferred_element_type=F32)
        pa = jnp.dot(a_ref[rows, :], wba_ref[...], preferred_element_type=F32)
        ps = jnp.dot(s_ref[rows, :], wbs_ref[...], preferred_element_type=F32)
        merged = _sigmoid(gates[:, :d]) * pa + _sigmoid(gates[:, d:]) * ps
        y = jnp.dot(merged.astype(BF16), wo_ref[...], preferred_element_type=F32)
        x1 = _layer_norm(DEEPNORM_ALPHA * x + y, g_ref[...], b_ref[...])
        x1_ref[rows, :] = x1
        if with_router:
            hi, lo = _split_bf16(x1)
            both = jnp.dot(hi, rcat_ref[...], preferred_element_type=F32)
            logits = (both[:, :LANES] + jnp.dot(lo, rcat_ref[:, :LANES], preferred_element_type=F32)
                      + both[:, LANES:])
            lane = lax.broadcasted_iota(jnp.int32, logits.shape, 1).astype(F32)
            big = float(LANES)
            l1 = jnp.where(lane < N_EXPERTS, logits, -jnp.inf)
            m1 = jnp.max(l1, axis=-1, keepdims=True)
            i1 = jnp.min(jnp.where(l1 == m1, lane, big), axis=-1, keepdims=True)
            l2 = jnp.where(lane == i1, -jnp.inf, l1)
            m2 = jnp.max(l2, axis=-1, keepdims=True)
            i2 = jnp.min(jnp.where(l2 == m2, lane, big), axis=-1, keepdims=True)
            ed = jnp.exp(m2 - m1)
            g1 = 1.0 / (1.0 + ed)
            g2 = ed / (1.0 + ed)
            route = jnp.where(lane == 0.0, i1,
                              jnp.where(lane == 1.0, i2,
                                        jnp.where(lane == 2.0, g1, jnp.where(lane == 3.0, g2, 0.0))))
            route_ref[rows, :] = route


def _merge(x, a, s, wg, wba, wbs, wo, g, b, router=None, *, tm):
    t, d = x.shape
    row = lambda i: (i, 0)
    params = [wg, wba, wbs, wo, g, b] + list(router or ())
    in_specs = [pl.BlockSpec((tm, d), row), pl.BlockSpec((tm, a.shape[1]), row), pl.BlockSpec((tm, s.shape[1]), row)]
    in_specs += [_layer_spec(p) for p in params]
    args = [x, a, s] + [p.arr for p in params]
    out_specs = [pl.BlockSpec((tm, d), row)]
    out_shape = [jax.ShapeDtypeStruct((t, d), F32)]
    if router is not None:
        out_specs.append(pl.BlockSpec((tm, LANES), row))
        out_shape.append(jax.ShapeDtypeStruct((t, LANES), F32))
    return pl.pallas_call(
        functools.partial(_merge_body, with_router=router is not None),
        grid=(t // tm,),
        in_specs=in_specs, out_specs=out_specs, out_shape=out_shape,
        compiler_params=_cparams(("parallel",)),
        name="merge_ln1",
    )(*args)


def _ple(xb, p_ref, wpg_ref, wpp_ref):
    gate = _sigmoid(jnp.dot(xb, wpg_ref[...], preferred_element_type=F32))
    return gate * jnp.dot(p_ref[...].astype(BF16), wpp_ref[...], preferred_element_type=F32)


def _ffn_body(x_ref, p_ref, wg_ref, wu_ref, wd_ref, wpg_ref, wpp_ref, g_ref, b_ref, o_ref, acc):
    x1 = x_ref[...]
    xb = x1.astype(BF16)
    for c, (c0, cw) in enumerate(FFN_CHUNKS):
        gt = jnp.dot(xb, wg_ref[:, c0:c0 + cw], preferred_element_type=F32)
        up = jnp.dot(xb, wu_ref[:, c0:c0 + cw], preferred_element_type=F32)
        act = (gt * _sigmoid(gt) * up).astype(BF16)
        contrib = jnp.dot(act, wd_ref[c0:c0 + cw, :], preferred_element_type=F32)
        if c == 0:
            acc[...] = contrib
        else:
            acc[...] += contrib
    h = DEEPNORM_ALPHA * x1 + acc[...] + _ple(xb, p_ref, wpg_ref, wpp_ref)
    o_ref[...] = _layer_norm(h, g_ref[...], b_ref[...])


def _resident(shape, index_map):
    return pl.BlockSpec(shape, index_map, pipeline_mode=pl.Buffered(1))


def _ple_rows_spec(p, tm):
    return pl.BlockSpec((None, tm, p.arr.shape[2]), lambda i: (p.idx, i, 0))


def _dense_layer_body(x_ref, a_ref, s_ref, p_ref, wg_ref, wba_ref, wbs_ref, wo_ref, g1_ref, b1_ref,
                      fg_ref, fu_ref, fd_ref, wpg_ref, wpp_ref, g2_ref, b2_ref, w_next_ref,
                      o_ref, q_ref, kv_ref, u_ref, x1_buf, acc):
    _merge_body(x_ref, a_ref, s_ref, wg_ref, wba_ref, wbs_ref, wo_ref, g1_ref, b1_ref, x1_buf, with_router=False)
    _ffn_body(x1_buf, p_ref, fg_ref, fu_ref, fd_ref, wpg_ref, wpp_ref, g2_ref, b2_ref, o_ref, acc)
    _project(o_ref[...], w_next_ref, q_ref, kv_ref, u_ref)


def _dense_layer(x, a, s, p, wg, wba, wbs, wo, g1, b1, w_gu, w_down, li, wpg, wpp, g2, b2, w_next, *, tm):
    t, d = x.shape
    ff = w_down.shape[1]
    row = lambda i: (i, 0)
    merge_params = [wg, wba, wbs, wo, g1, b1]
    tail_params = [wpg, wpp, g2, b2, w_next]
    proj_specs, proj_shapes = _proj_outputs(t, tm, w_next.arr.shape[2])
    return pl.pallas_call(
        _dense_layer_body,
        grid=(t // tm,),
        in_specs=[pl.BlockSpec((tm, d), row), pl.BlockSpec((tm, a.shape[1]), row), pl.BlockSpec((tm, s.shape[1]), row),
                  _ple_rows_spec(p, tm)]
        + [_layer_spec(q, resident=True) for q in merge_params]
        + [_resident((None, d, ff), lambda i: (li, 0, 0)),
           _resident((None, d, ff), lambda i: (li, 0, 1)),
           _resident((None, ff, d), lambda i: (li, 0, 0))]
        + [_layer_spec(q, resident=True) for q in tail_params],
        out_specs=[pl.BlockSpec((tm, d), row)] + proj_specs,
        out_shape=[jax.ShapeDtypeStruct((t, d), F32)] + proj_shapes,
        scratch_shapes=[pltpu.VMEM((tm, d), F32), pltpu.VMEM((tm, d), F32)],
        compiler_params=_cparams(("parallel",)),
        name="dense_layer",
    )(x, a, s, p.arr, *[q.arr for q in merge_params], w_gu, w_gu, w_down, *[q.arr for q in tail_params])


def _moe_body(te_ref, nu_ref, src_ref, dst_ref, x_hbm, wgu_ref, wd_ref, y_hbm,
              xg, xb, acc, ybuf, gsem, ssem, *, tm, fw, spare_row):
    i = pl.program_id(0)
    nu = nu_ref[0]
    slot = i % 2
    other = 1 - slot
    ffe = wd_ref.shape[0]
    nc = ffe // fw
    per_chunk = tm // nc

    def gather(tile, r, buf):
        row = src_ref[tile * tm + r]
        pltpu.make_async_copy(x_hbm.at[pl.ds(row, 1), :], xg.at[buf, pl.ds(r, 1), :], gsem.at[buf]).start()

    def scatter(row, r, buf):
        pltpu.make_async_copy(ybuf.at[buf, pl.ds(r, 1), :], y_hbm.at[pl.ds(row, 1), :], ssem.at[buf]).start()

    def prev_tile_dst(r):
        return jnp.where(i > 0, dst_ref[jnp.maximum(i - 1, 0) * tm + r], spare_row + r)

    @pl.when(i == 0)
    def _():
        ybuf[...] = jnp.zeros_like(ybuf)

        def one(r, c):
            gather(0, r, 0)
            return c
        lax.fori_loop(0, tm, one, 0)

    @pl.when(i <= nu)
    def _():
        pltpu.make_async_copy(x_hbm.at[pl.ds(0, tm), :], xg.at[slot], gsem.at[slot]).wait()

    @pl.when(i < nu)
    def _():
        xb[...] = xg[slot].astype(BF16)
        x = xb[...]
        for c in range(nc):
            c0 = c * fw
            gt = jnp.dot(x, wgu_ref[:, c0:c0 + fw], preferred_element_type=F32)
            up = jnp.dot(x, wgu_ref[:, ffe + c0:ffe + c0 + fw], preferred_element_type=F32)
            act = (gt * _sigmoid(gt) * up).astype(BF16)
            part = jnp.dot(act, wd_ref[c0:c0 + fw, :], preferred_element_type=F32)
            if c == 0:
                acc[...] = part
            else:
                acc[...] += part
            r1 = tm if c == nc - 1 else (c + 1) * per_chunk
            for r in range(c * per_chunk, r1):
                gather(i + 1, r, other)
                scatter(prev_tile_dst(r), r, other)

    @pl.when(i == nu)
    def _():
        def one(r, c):
            scatter(dst_ref[(i - 1) * tm + r], r, other)
            return c
        lax.fori_loop(0, tm, one, 0)

    @pl.when((i >= 1) & (i <= nu + 1))
    def _():
        pltpu.make_async_copy(ybuf.at[slot], y_hbm.at[pl.ds(0, tm), :], ssem.at[slot]).wait()

    @pl.when(i < nu)
    def _():
        ybuf[slot] = _pack_bf16_pairs(acc[...])


def _moe_experts(x1, w_gu, w_down, li, tile_expert, n_used, src, dst, *, tm, fw, out_rows):
    t, d = x1.shape
    ffe = w_down.shape[2]
    nt = src.shape[0] // tm
    return pl.pallas_call(
        functools.partial(_moe_body, tm=tm, fw=fw, spare_row=out_rows - tm),
        grid_spec=pltpu.PrefetchScalarGridSpec(
            num_scalar_prefetch=4,
            grid=(nt,),
            in_specs=[pl.BlockSpec(memory_space=pl.ANY),
                      _resident((None, None, d, 2 * ffe), lambda i, te, nu, s, ds: (li, te[i], 0, 0)),
                      _resident((None, None, ffe, d), lambda i, te, nu, s, ds: (li, te[i], 0, 0))],
            out_specs=pl.BlockSpec(memory_space=pl.ANY),
            scratch_shapes=[pltpu.VMEM((2, tm, d), F32), pltpu.VMEM((tm, d), BF16), pltpu.VMEM((tm, d), F32),
                            pltpu.VMEM((2, tm, d // 2), jnp.uint32),
                            pltpu.SemaphoreType.DMA((2,)), pltpu.SemaphoreType.DMA((2,))]),
        out_shape=jax.ShapeDtypeStruct((out_rows, d // 2), jnp.uint32),
        compiler_params=_cparams(("arbitrary",)),
        name="moe_experts",
    )(tile_expert, n_used, src, dst, x1, w_gu, w_down)


def _moe_out_body(x_ref, route_ref, y0_ref, y1_ref, p_ref, wpg_ref, wpp_ref, g_ref, b_ref, *rest):
    o_ref = rest[0] if len(rest) == 1 else rest[1]
    x1 = x_ref[...]
    route = route_ref[...]
    ffn = route[:, 2:3] * _unpack_bf16_pairs(y0_ref[...]) + route[:, 3:4] * _unpack_bf16_pairs(y1_ref[...])
    h = DEEPNORM_ALPHA * x1 + ffn + _ple(x1.astype(BF16), p_ref, wpg_ref, wpp_ref)
    x2 = _layer_norm(h, g_ref[...], b_ref[...])
    o_ref[...] = x2
    if len(rest) > 1:
        _project(x2, rest[0], *rest[2:])


def _moe_out(x1, route, p, y, wpg, wpp, g, b, w_next=None, *, tm):
    t, d = x1.shape
    nt = t // tm
    row = lambda i: (i, 0)
    params = [wpg, wpp, g, b] + ([w_next] if w_next is not None else [])
    out_specs, out_shape = [pl.BlockSpec((tm, d), row)], [jax.ShapeDtypeStruct((t, d), F32)]
    if w_next is not None:
        specs, shapes = _proj_outputs(t, tm, w_next.arr.shape[2])
        out_specs, out_shape = out_specs + specs, out_shape + shapes
    out = pl.pallas_call(
        _moe_out_body,
        grid=(nt,),
        in_specs=[pl.BlockSpec((tm, d), row), pl.BlockSpec((tm, LANES), row),
                  pl.BlockSpec((tm, d // 2), row), pl.BlockSpec((tm, d // 2), lambda i: (nt + i, 0)),
                  _ple_rows_spec(p, tm)] + [_layer_spec(q) for q in params],
        out_specs=out_specs, out_shape=out_shape,
        compiler_params=_cparams(("parallel",)),
        name="moe_combine_ln2",
    )(x1, route, y, y, p.arr, *[q.arr for q in params])
    return out[0] if w_next is None else out


def _slot_table_body(pos_ref, init_hbm, dst_ref, sem, *, t, groups):
    @pl.when(pl.program_id(0) == 0)
    def _():
        copy = pltpu.make_async_copy(init_hbm, dst_ref, sem)
        copy.start()
        copy.wait()

    batch = SLOT_TABLE_BATCH
    base = pl.program_id(0) * (t // groups)

    def put(kb, c):
        tok0 = base + kb * batch
        ps = [pos_ref[2 * tok0 + j] for j in range(2 * batch)]
        for j, p in enumerate(ps):
            dst_ref[p] = (j % 2) * t + tok0 + j // 2
        return c
    lax.fori_loop(0, t // groups // batch, put, 0)


def _slot_table(pos, *, t, tm, r):
    groups = SLOT_TABLE_GROUPS
    assert t % (groups * SLOT_TABLE_BATCH) == 0
    init = 2 * t + jnp.arange(r, dtype=jnp.int32) % tm
    smem = pl.BlockSpec(memory_space=pltpu.SMEM)
    return pl.pallas_call(
        functools.partial(_slot_table_body, t=t, groups=groups),
        grid=(groups,),
        in_specs=[smem, pl.BlockSpec(memory_space=pl.ANY)], out_specs=smem,
        out_shape=jax.ShapeDtypeStruct((r,), jnp.int32),
        scratch_shapes=[pltpu.SemaphoreType.DMA],
        compiler_params=_cparams(("arbitrary",)),
        name="moe_slot_table",
    )(pos, init)


def _routing_tables(route, *, tm, n_experts):
    t = route.shape[0]
    eids = route[:, :2].astype(jnp.int32).reshape(-1)
    onehot = (eids[:, None] == jnp.arange(n_experts, dtype=jnp.int32)[None, :]).astype(jnp.int32)
    csum = jnp.cumsum(onehot, axis=0)
    rank = jnp.sum((csum - onehot) * onehot, axis=1)
    cnt = csum[-1]
    ntile = (cnt + tm - 1) // tm
    tile_end = jnp.cumsum(ntile)
    row_off = (tile_end - ntile) * tm
    pos = jnp.sum(onehot * row_off[None, :], axis=1) + rank
    r = 2 * t + (n_experts + 2) * tm
    dst = _slot_table(pos.astype(jnp.int32), t=t, tm=tm, r=r)
    src = jnp.where(dst < 2 * t, dst % t, 0)
    n_used = tile_end[-1:]
    tiles = jnp.arange(r // tm, dtype=jnp.int32)
    te = jnp.sum((tiles[:, None] >= tile_end[None, :]).astype(jnp.int32), axis=1)
    last = jnp.sum((n_used - 1 >= tile_end).astype(jnp.int32))
    te = jnp.where(tiles < n_used, te, last).astype(jnp.int32)
    out_rows = 2 * t + tm
    return te, n_used.astype(jnp.int32), src.astype(jnp.int32), dst.astype(jnp.int32), out_rows


def kernel(x, p, rel_bias, w_in, attn_sinks, ssm_lambda_re, ssm_lambda_im, ssm_log_dt, ssm_b_re, ssm_b_im,
           ssm_c_re, ssm_c_im, ssm_d, w_glu, b_glu, w_branch_attn, w_branch_ssm, w_out, ln1_g, ln1_b,
           ffn_w_gate_up, ffn_w_down, moe_router, moe_w_gate_up, moe_w_down, ple_w_proj, ple_w_gate,
           ln2_g, ln2_b):
    bsz, seq, d = x.shape
    t = bsz * seq
    depth = w_in.shape[0]

    xt = x.reshape(t, d)
    pt = p.reshape(depth, t, p.shape[-1])

    bias = _attn_bias_table(rel_bias)
    sinks = attn_sinks * LOG2_E
    ffn_gu = ffn_w_gate_up.astype(BF16)
    ffn_dn = ffn_w_down.astype(BF16)
    moe_gu = moe_w_gate_up.astype(BF16)
    moe_dn = moe_w_down.astype(BF16)

    aw, kw = ATTN_WIDTH, KV_WIDTH
    u0 = aw + 2 * kw
    w_proj = jnp.concatenate([_q_head_tiles(w_in[:, :, :aw]), w_in[:, :, aw:u0 + SSM_WIDTH]], axis=2).astype(BF16)
    w_gates = w_in[:, :, u0 + SSM_WIDTH:].astype(BF16)
    wba = w_branch_attn.reshape(depth, N_Q_HEADS, HEAD_DIM, d)[:, np.array(_attn_out_head_order())]
    wba = wba.reshape(depth, aw, d).astype(BF16)
    wbs, wo = w_branch_ssm.astype(BF16), w_out.astype(BF16)
    wpg, wpp = ple_w_gate.astype(BF16), ple_w_proj.astype(BF16)
    g1, b1 = ln1_g.reshape(depth, 1, d), ln1_b.reshape(depth, 1, d)
    g2, b2 = ln2_g.reshape(depth, 1, d), ln2_b.reshape(depth, 1, d)

    lbr, lbi, bbr, bbi = _ssm_params(ssm_lambda_re, ssm_lambda_im, ssm_log_dt, ssm_b_re, ssm_b_im)
    bre = _block_diag_halves(bbr).astype(BF16)
    bim = _block_diag_halves(bbi).astype(BF16)
    cre = _block_diag_halves(jnp.transpose(ssm_c_re, (0, 1, 3, 2))).astype(BF16)
    cim = _block_diag_halves(jnp.transpose(ssm_c_im, (0, 1, 3, 2))).astype(BF16)
    ssm_skip = ssm_d.reshape(depth, 1, -1)
    wglu, bglu = w_glu.astype(BF16), b_glu.reshape(depth, 1, -1)

    rt = jnp.pad(moe_router, ((0, 0), (0, 0), (0, LANES - N_EXPERTS)))
    rhi = rt.astype(BF16)
    rcat = jnp.concatenate([rhi, (rt - rhi.astype(F32)).astype(BF16)], axis=2)

    q, kv, u = _inproj(xt, _Layer(w_proj, 0), tm=TM_PROJ)
    for i in range(depth):
        at = lambda arr, idx=i: _Layer(arr, idx)
        w_next = _Layer(w_proj, i + 1) if i + 1 < depth else None
        a_out = _attention(q, kv, at(sinks), bias, seq=seq, batch=bsz)
        s_out = _ssm(u, at(bre), at(bim), at(cre), at(cim), at(lbr), at(lbi), at(ssm_skip), at(wglu), at(bglu),
                     batch=bsz, steps=SSM_STEPS, lane_chunk=SSM_LANE_CHUNK)
        merge_w = (at(w_gates), at(wba), at(wbs), at(wo), at(g1), at(b1))
        if i % 2 == 0:
            xt, q, kv, u = _dense_layer(xt, a_out, s_out, at(pt), *merge_w, ffn_gu, ffn_dn, i // 2,
                                        at(wpg), at(wpp), at(g2), at(b2), w_next, tm=TM_FFN)
        else:
            router = (_Layer(rcat, i // 2),)
            x1, route = _merge(xt, a_out, s_out, *merge_w, router, tm=TM_MERGE)
            te, n_used, src, dst, out_rows = _routing_tables(route, tm=TM_MOE, n_experts=N_EXPERTS)
            y = _moe_experts(x1, moe_gu, moe_dn, i // 2, te, n_used, src, dst, tm=TM_MOE, fw=FW_MOE,
                             out_rows=out_rows)
            out = _moe_out(x1, route, at(pt), y, at(wpg), at(wpp), at(g2), at(b2), w_next, tm=TM_FFN)
            xt, q, kv, u = out if w_next is not None else (out, None, None, None)

    return xt.reshape(bsz, seq, d)
```

```python
import functools
import math
from typing import NamedTuple

import jax
import jax.numpy as jnp
import numpy as np
from jax import lax
from jax.experimental import pallas as pl
from jax.experimental.pallas import tpu as pltpu

F32 = jnp.float32
BF16 = jnp.bfloat16

D_MODEL = 1024
BATCH = 16
SEQ = 2048
DEPTH = 4
HEAD_DIM = 64
N_Q_HEADS = 8
N_KV_HEADS = 2
ATTN_WIDTH = N_Q_HEADS * HEAD_DIM
KV_WIDTH = N_KV_HEADS * HEAD_DIM
ATTN_BLOCK = 128
N_BUCKETS = 32
MAX_DISTANCE = 128
SSM_WIDTH = 512
SSM_GROUP_CH = 16
SSM_GROUPS = SSM_WIDTH // SSM_GROUP_CH
SSM_STATE = 64
D_FF = 2816
N_EXPERTS = 8
D_FF_EXPERT = 3584
PLE_DIM = 256
DEEPNORM_ALPHA = (2 * DEPTH) ** 0.25
LN_EPS = 1e-5
NEG_INF = -1e30
LOG2_E = math.log2(math.e)

LANES = 128
VMEM_LIMIT_BYTES = 56 * 1024 * 1024

TM_PROJ = 1024
ATTN_BLOCKS_PER_STEP = 8
TM_MERGE = 512
MERGE_ROW_GROUPS = 2
TM_FFN = 512
TM_MOE = 512
FW_MOE = 512
SLOT_TABLE_GROUPS = 16
SLOT_TABLE_BATCH = 8
SSM_STEPS = 32
SSM_PERM_STEPS = 16
SSM_LANE_CHUNK = 512
FFN_CHUNKS = ((0, 768), (768, 768), (1536, 768), (2304, 512))


def _cparams(sem):
    return pltpu.CompilerParams(dimension_semantics=sem, vmem_limit_bytes=VMEM_LIMIT_BYTES)


class _Layer(NamedTuple):
    arr: jax.Array
    idx: int


def _layer_spec(p, resident=False):
    tail = (0,) * (p.arr.ndim - 1)
    mode = {"pipeline_mode": pl.Buffered(1)} if resident else {}
    return pl.BlockSpec((None,) + p.arr.shape[1:], lambda *_: (p.idx,) + tail, **mode)


def _layer_norm(h, g, b):
    mu = jnp.mean(h, axis=-1, keepdims=True)
    c = h - mu
    var = jnp.mean(c * c, axis=-1, keepdims=True)
    return c * lax.rsqrt(var + LN_EPS) * g + b


def _sigmoid(v):
    return 1.0 / (1.0 + jnp.exp(-v))


def _pack_bf16_pairs(v):
    c = v.shape[1] // 2
    lo = lax.bitcast_convert_type(v[:, :c].astype(BF16).astype(F32), jnp.uint32)
    hi = lax.bitcast_convert_type(v[:, c:].astype(BF16).astype(F32), jnp.uint32)
    return (lo >> 16) | (hi & jnp.uint32(0xFFFF0000))


def _unpack_bf16_pairs(w):
    lo = lax.bitcast_convert_type(w << 16, F32)
    hi = lax.bitcast_convert_type(w & jnp.uint32(0xFFFF0000), F32)
    return jnp.concatenate([lo, hi], axis=1)


def _gelu_tanh(v):
    return 0.5 * v * (1.0 + jnp.tanh(math.sqrt(2.0 / math.pi) * (v + 0.044715 * (v * v * v))))


def _inproj_body(x_ref, w_ref, q_ref, kv_ref, u_ref):
    _project(x_ref[...], w_ref, q_ref, kv_ref, u_ref)


def _project(x, w_ref, q_ref, kv_ref, u_ref):
    z = jnp.dot(x.astype(BF16), w_ref[...], preferred_element_type=F32)
    aw = q_ref.shape[1]
    kw = kv_ref.shape[1]
    q_ref[...] = (z[:, :aw] * (LOG2_E * HEAD_DIM ** -0.5)).astype(BF16)
    kv_ref[...] = z[:, aw:aw + kw].astype(BF16)
    u_ref[...] = z[:, aw + kw:].astype(BF16)


def _proj_outputs(t, tm, n):
    aw, kw = N_Q_HEADS * LANES, 2 * KV_WIDTH
    widths = (aw, kw, n - aw - kw)
    specs = [pl.BlockSpec((tm, w), lambda i: (i, 0)) for w in widths]
    shapes = [jax.ShapeDtypeStruct((t, w), BF16) for w in widths]
    return specs, shapes


def _inproj(x, w, *, tm):
    t, d = x.shape
    n = w.arr.shape[2]
    aw, kw = N_Q_HEADS * LANES, 2 * KV_WIDTH
    uw = n - aw - kw
    return pl.pallas_call(
        _inproj_body,
        grid=(t // tm,),
        in_specs=[pl.BlockSpec((tm, d), lambda i: (i, 0)), _layer_spec(w)],
        out_specs=[pl.BlockSpec((tm, aw), lambda i: (i, 0)),
                   pl.BlockSpec((tm, kw), lambda i: (i, 0)),
                   pl.BlockSpec((tm, uw), lambda i: (i, 0))],
        out_shape=[jax.ShapeDtypeStruct((t, aw), BF16),
                   jax.ShapeDtypeStruct((t, kw), BF16),
                   jax.ShapeDtypeStruct((t, uw), BF16)],
        compiler_params=_cparams(("parallel",)),
        name="inproj",
    )(x, w.arr)


def _attn_body(sink_ref, q_ref, kvc_ref, kvp_ref, bias0_ref, bias_ref, o_ref, *, blk, nbs, layer):
    group = N_Q_HEADS // N_KV_HEADS
    ks = slice(0, LANES)
    vs = slice(LANES, 2 * LANES)
    low = lax.broadcasted_iota(jnp.int32, (blk, LANES), 1) < HEAD_DIM
    for j in range(nbs):
        rows = slice(j * blk, (j + 1) * blk)
        b_ref = bias0_ref if j == 0 else bias_ref
        if j == 0:
            k = jnp.concatenate([kvp_ref[:, ks], kvc_ref[0:blk, ks]], axis=0)
            v = jnp.concatenate([kvp_ref[:, vs], kvc_ref[0:blk, vs]], axis=0)
        else:
            k = kvc_ref[(j - 1) * blk:(j + 1) * blk, ks]
            v = kvc_ref[(j - 1) * blk:(j + 1) * blk, vs]
        for tile in range(group):
            halves = []
            for g in range(N_KV_HEADS):
                h = tile + group * g
                qt = q_ref[rows, h * LANES:(h + 1) * LANES]
                s = lax.dot_general(qt, k, (((1,), (1,)), ((), ())), preferred_element_type=F32)
                s = s + b_ref[h]
                sink = sink_ref[layer, h]
                m = jnp.maximum(jnp.max(s, axis=-1, keepdims=True), sink)
                e = jnp.exp2(s - m)
                denom = jnp.sum(e, axis=-1, keepdims=True) + jnp.exp2(sink - m)
                halves.append(jnp.dot(e.astype(BF16), v, preferred_element_type=F32) / denom)
            o_ref[rows, tile * LANES:(tile + 1) * LANES] = jnp.where(low, halves[0], halves[1]).astype(BF16)


def _attention(q, kv, sinks, bias, *, seq, batch):
    assert KV_WIDTH == LANES
    t = q.shape[0]
    blk = ATTN_BLOCK
    nbs = ATTN_BLOCKS_PER_STEP
    ns = seq // (blk * nbs)
    qw, kw, aw = q.shape[1], kv.shape[1], ATTN_WIDTH
    table = (None, N_Q_HEADS, blk, 2 * blk)
    return pl.pallas_call(
        functools.partial(_attn_body, blk=blk, nbs=nbs, layer=sinks.idx),
        grid=(batch, ns),
        in_specs=[pl.BlockSpec(memory_space=pltpu.SMEM),
                  pl.BlockSpec((blk * nbs, qw), lambda b, n: (b * ns + n, 0)),
                  pl.BlockSpec((blk * nbs, kw), lambda b, n: (b * ns + n, 0)),
                  pl.BlockSpec((blk, kw), lambda b, n: ((b * ns + n) * nbs - jnp.minimum(n, 1), 0)),
                  pl.BlockSpec(table, lambda b, n: (jnp.minimum(n, 1), 0, 0, 0)),
                  pl.BlockSpec(table, lambda b, n: (1, 0, 0, 0))],
        out_specs=pl.BlockSpec((blk * nbs, aw), lambda b, n: (b * ns + n, 0)),
        out_shape=jax.ShapeDtypeStruct((t, aw), BF16),
        compiler_params=_cparams(("parallel", "parallel")),
        name="swa_attention",
    )(sinks.arr, q, kv, kv, bias, bias)


def _q_head_tiles(w):
    depth, d, _ = w.shape
    group = N_Q_HEADS // N_KV_HEADS
    wh = w.reshape(depth, d, N_Q_HEADS, 1, HEAD_DIM)
    half = jnp.arange(N_KV_HEADS)[None, None, None, :, None]
    kv_head = (jnp.arange(N_Q_HEADS) // group)[None, None, :, None, None]
    tiles = jnp.where(half == kv_head, wh, 0.0)
    return tiles.reshape(depth, d, N_Q_HEADS * LANES)


def _attn_out_head_order():
    group = N_Q_HEADS // N_KV_HEADS
    return [j + group * half for j in range(group) for half in range(N_KV_HEADS)]


def _attn_bias_table(rel_bias):
    blk = ATTN_BLOCK
    d = np.arange(blk, dtype=np.int32)
    max_exact = N_BUCKETS // 2
    d_f = np.maximum(d, 1).astype(np.float32)
    large = max_exact + (np.log(d_f / np.float32(max_exact)) / np.float32(math.log(MAX_DISTANCE / max_exact))
                         * np.float32(N_BUCKETS - max_exact)).astype(np.int32)
    large = np.minimum(large, N_BUCKETS - 1)
    bucket = np.where(d < max_exact, d, large)
    vals = jnp.transpose(rel_bias.astype(F32)[bucket], (1, 0))
    h = vals.shape[0]
    neg = jnp.full((h, blk), NEG_INF, F32)
    strip = jnp.concatenate([neg, vals[:, ::-1], neg], axis=1)
    rows = [strip[:, blk - 1 - i:blk - 1 - i + 2 * blk] for i in range(blk)]
    table = jnp.stack(rows, axis=1) * LOG2_E
    first = jnp.where(np.arange(2 * blk)[None, None, :] < blk, NEG_INF, table)
    return jnp.stack([first, table], axis=0)


def _ssm_param_body(lre_ref, lim_ref, ldt_ref, bre_ref, bim_ref, lbr_ref, lbi_ref, bbr_ref, bbi_ref):
    lre, lim = lre_ref[...], lim_ref[...]
    dt = jnp.exp(ldt_ref[...])
    mag = jnp.exp(lre * dt)
    lbr = mag * jnp.cos(lim * dt)
    lbi = mag * jnp.sin(lim * dt)
    lbr_ref[...] = lbr
    lbi_ref[...] = lbi
    nr, ni = lbr - 1.0, lbi
    inv = 1.0 / (lre * lre + lim * lim)
    cr = (nr * lre + ni * lim) * inv
    ci = (ni * lre - nr * lim) * inv
    br, bi = bre_ref[...], bim_ref[...]
    bbr_ref[...] = cr * br - ci * bi
    bbi_ref[...] = cr * bi + ci * br


def _ssm_params(lam_re, lam_im, log_dt, b_re, b_im):
    depth, g, p = lam_re.shape
    h = b_re.shape[3]
    n = depth * g
    ldt = jnp.broadcast_to(log_dt.reshape(n, 1, 1), (n, 1, p))
    brt = jnp.transpose(b_re, (0, 1, 3, 2)).reshape(n, h, p)
    bit = jnp.transpose(b_im, (0, 1, 3, 2)).reshape(n, h, p)
    sd = jax.ShapeDtypeStruct
    lbr, lbi, bbr, bbi = pl.pallas_call(
        _ssm_param_body,
        out_shape=[sd((n, 1, p), F32), sd((n, 1, p), F32), sd((n, h, p), F32), sd((n, h, p), F32)],
        name="ssm_params",
    )(lam_re.reshape(n, 1, p), lam_im.reshape(n, 1, p), ldt, brt, bit)
    return (lbr.reshape(depth, 1, g * p), lbi.reshape(depth, 1, g * p),
            bbr.reshape(depth, g, h, p), bbi.reshape(depth, g, h, p))


def _block_diag_halves(w):
    depth, g, a, b = w.shape
    hg = g // 2
    wh = w.reshape(depth, 2, hg, a, b)
    eye = jnp.eye(hg, dtype=bool)[None, None, :, None, :, None]
    full = jnp.where(eye, wh[:, :, :, :, None, :], 0.0)
    return full.reshape(depth, 2, hg * a, hg * b)


def _ssm_body(u_ref, perm_ref, permt_ref, bre_ref, bim_ref, cre_ref, cim_ref, lr_ref, li_ref, d_ref,
              wglu_ref, bglu_ref, *rest, steps, batch, lane_chunk, with_cast):
    if with_cast:
        wsrc_ref, o_ref, wdst_ref, xr0, xi0, ub0, xr1, xi1, ub1, sr, si = rest
    else:
        o_ref, xr0, xi0, ub0, xr1, xi1, ub1, sr, si = rest
    s = pl.program_id(0)

    @pl.when(s == 0)
    def _():
        for ref in (xr0, xi0, ub0, xr1, xi1, ub1, sr, si):
            ref[...] = jnp.zeros_like(ref)

    sub = SSM_PERM_STEPS
    rows_g = sub * batch

    def stage(fill, scan):
        xr_f, xi_f, ub_f = fill
        xr, xi, ub = scan
        parts = []
        for g in range(steps // sub):
            bm = jnp.concatenate([u_ref[b, g * sub:(g + 1) * sub, :] for b in range(batch)], axis=0)
            parts.append(jnp.dot(perm_ref[...], bm, preferred_element_type=F32).astype(BF16))
        u_new = jnp.concatenate(parts, axis=0)
        ub_f[...] = u_new
        hw = u_new.shape[1] // 2
        hs = xr.shape[1] // 2
        for hf in range(2):
            uh = u_new[:, hf * hw:(hf + 1) * hw]
            xr_f[:, hf * hs:(hf + 1) * hs] = jnp.dot(uh, bre_ref[hf], preferred_element_type=F32)
            xi_f[:, hf * hs:(hf + 1) * hs] = jnp.dot(uh, bim_ref[hf], preferred_element_type=F32)
        for c in range(xr.shape[1] // lane_chunk):
            cs = slice(c * lane_chunk, (c + 1) * lane_chunk)
            ar = lr_ref[:, cs]
            ai = li_ref[:, cs]
            pr, pi = sr[:, cs], si[:, cs]
            for t in range(steps):
                rows = slice(t * batch, (t + 1) * batch)
                nr = ar * pr - ai * pi + xr[rows, cs]
                ni = ar * pi + ai * pr + xi[rows, cs]
                xr[rows, cs] = nr
                xi[rows, cs] = ni
                pr, pi = nr, ni
            sr[:, cs] = pr
            si[:, cs] = pi
        ys = []
        for hf in range(2):
            xrb = xr[:, hf * hs:(hf + 1) * hs].astype(BF16)
            xib = xi[:, hf * hs:(hf + 1) * hs].astype(BF16)
            ys.append(jnp.dot(xrb, cre_ref[hf], preferred_element_type=F32)
                      - jnp.dot(xib, cim_ref[hf], preferred_element_type=F32))
        y = jnp.concatenate(ys, axis=1) + d_ref[...] * ub[...].astype(F32)
        y = _gelu_tanh(y)
        z = jnp.dot(y.astype(BF16), wglu_ref[...], preferred_element_type=F32) + bglu_ref[...]
        out = (y * _sigmoid(z)).astype(BF16)
        for g in range(steps // sub):
            bm = jnp.dot(permt_ref[...], out[g * rows_g:(g + 1) * rows_g], preferred_element_type=F32).astype(BF16)
            for b in range(batch):
                o_ref[b, g * sub:(g + 1) * sub, :] = bm[b * sub:(b + 1) * sub]
        if with_cast:
            wdst_ref[...] = wsrc_ref[...].astype(BF16)

    @pl.when(s % 2 == 0)
    def _():
        stage((xr0, xi0, ub0), (xr1, xi1, ub1))

    @pl.when(s % 2 == 1)
    def _():
        stage((xr1, xi1, ub1), (xr0, xi0, ub0))


def _time_major_permutation(batch, sub):
    n = batch * sub
    r = np.arange(n)
    p = np.zeros((n, n), np.float32)
    p[r, (r % batch) * sub + r // batch] = 1.0
    return p


def _ssm(u, bre, bim, cre, cim, lbr, lbi, d, wglu, bglu, cast=None, *, batch, steps, lane_chunk):
    t, w = u.shape
    seq = t // batch
    n = seq // steps
    ns = lbr.arr.shape[2]
    rows = steps * batch
    perm = _time_major_permutation(batch, SSM_PERM_STEPS)
    const2 = lambda i: (0, 0)
    params = (bre, bim, cre, cim, lbr, lbi, d, wglu, bglu)
    buf = [pltpu.VMEM((rows, ns), F32), pltpu.VMEM((rows, ns), F32), pltpu.VMEM((rows, w), BF16)]
    in_specs = [pl.BlockSpec((batch, steps, w), lambda i: (0, jnp.minimum(i, n - 1), 0)),
                pl.BlockSpec(perm.shape, const2), pl.BlockSpec(perm.shape, const2)]
    in_specs += [_layer_spec(p) for p in params]
    args = [u.reshape(batch, seq, w), jnp.asarray(perm, BF16), jnp.asarray(perm.T, BF16)] + [p.arr for p in params]
    out_specs = [pl.BlockSpec((batch, steps, w), lambda i: (0, jnp.maximum(i - 1, 0), 0))]
    out_shape = [jax.ShapeDtypeStruct((batch, seq, w), BF16)]
    if cast is not None:
        wsrc = cast
        _, ne, r, c = wsrc.arr.shape
        per_e = (n + 1) // ne
        rb = r // per_e
        nblk = ne * per_e
        assert per_e >= 1 and r % per_e == 0 and rb % 16 == 0
        blk = lambda i: jnp.minimum(i, nblk - 1)
        in_specs.append(pl.BlockSpec((None, None, rb, c), lambda i: (wsrc.idx, blk(i) // per_e, blk(i) % per_e, 0)))
        args.append(wsrc.arr)
        out_specs.append(pl.BlockSpec((None, rb, c), lambda i: (blk(i) // per_e, blk(i) % per_e, 0)))
        out_shape.append(jax.ShapeDtypeStruct((ne, r, c), BF16))
    out = pl.pallas_call(
        functools.partial(_ssm_body, steps=steps, batch=batch, lane_chunk=lane_chunk, with_cast=cast is not None),
        grid=(n + 1,),
        in_specs=in_specs, out_specs=out_specs, out_shape=out_shape,
        scratch_shapes=buf + buf + [pltpu.VMEM((batch, ns), F32), pltpu.VMEM((batch, ns), F32)],
        compiler_params=_cparams(("arbitrary",)),
        name="s5_scan",
    )(*args)
    if cast is None:
        return out[0].reshape(t, w)
    return out[0].reshape(t, w), out[1]


def _split_bf16(v):
    hi = v.astype(BF16)
    lo = (v - hi.astype(F32)).astype(BF16)
    return hi, lo


def _merge_body(x_ref, a_ref, s_ref, wg_ref, wba_ref, wbs_ref, wo_ref, g_ref, b_ref, *rest, with_router):
    if with_router:
        rcat_ref, x1_ref, route_ref = rest
    else:
        (x1_ref,) = rest
    tm, d = x_ref.shape
    rg = tm // MERGE_ROW_GROUPS
    for grp in range(MERGE_ROW_GROUPS):
        rows = slice(grp * rg, (grp + 1) * rg)
        x = x_ref[rows, :]
        gates = jnp.dot(x.astype(BF16), wg_ref[...], preferred_element_type=F32)
        pa = jnp.dot(a_ref[rows, :], wba_ref[...], preferred_element_type=F32)
        ps = jnp.dot(s_ref[rows, :], wbs_ref[...], preferred_element_type=F32)
        merged = _sigmoid(gates[:, :d]) * pa + _sigmoid(gates[:, d:]) * ps
        y = jnp.dot(merged.astype(BF16), wo_ref[...], preferred_element_type=F32)
        x1 = _layer_norm(DEEPNORM_ALPHA * x + y, g_ref[...], b_ref[...])
        x1_ref[rows, :] = x1
        if with_router:
            hi, lo = _split_bf16(x1)
            both = jnp.dot(hi, rcat_ref[...], preferred_element_type=F32)
            logits = (both[:, :LANES] + jnp.dot(lo, rcat_ref[:, :LANES], preferred_element_type=F32)
                      + both[:, LANES:])
            lane = lax.broadcasted_iota(jnp.int32, logits.shape, 1).astype(F32)
            big = float(LANES)
            l1 = jnp.where(lane < N_EXPERTS, logits, -jnp.inf)
            m1 = jnp.max(l1, axis=-1, keepdims=True)
            i1 = jnp.min(jnp.where(l1 == m1, lane, big), axis=-1, keepdims=True)
            l2 = jnp.where(lane == i1, -jnp.inf, l1)
            m2 = jnp.max(l2, axis=-1, keepdims=True)
            i2 = jnp.min(jnp.where(l2 == m2, lane, big), axis=-1, keepdims=True)
            ed = jnp.exp(m2 - m1)
            g1 = 1.0 / (1.0 + ed)
            g2 = ed / (1.0 + ed)
            route = jnp.where(lane == 0.0, i1,
                              jnp.where(lane == 1.0, i2,
                                        jnp.where(lane == 2.0, g1, jnp.where(lane == 3.0, g2, 0.0))))
            route_ref[rows, :] = route


def _merge(x, a, s, wg, wba, wbs, wo, g, b, router=None, *, tm):
    t, d = x.shape
    row = lambda i: (i, 0)
    params = [wg, wba, wbs, wo, g, b] + list(router or ())
    in_specs = [pl.BlockSpec((tm, d), row), pl.BlockSpec((tm, a.shape[1]), row), pl.BlockSpec((tm, s.shape[1]), row)]
    in_specs += [_layer_spec(p) for p in params]
    args = [x, a, s] + [p.arr for p in params]
    out_specs = [pl.BlockSpec((tm, d), row)]
    out_shape = [jax.ShapeDtypeStruct((t, d), F32)]
    if router is not None:
        out_specs.append(pl.BlockSpec((tm, LANES), row))
        out_shape.append(jax.ShapeDtypeStruct((t, LANES), F32))
    return pl.pallas_call(
        functools.partial(_merge_body, with_router=router is not None),
        grid=(t // tm,),
        in_specs=in_specs, out_specs=out_specs, out_shape=out_shape,
        compiler_params=_cparams(("parallel",)),
        name="merge_ln1",
    )(*args)


def _ple(xb, p_ref, wpg_ref, wpp_ref):
    gate = _sigmoid(jnp.dot(xb, wpg_ref[...], preferred_element_type=F32))
    return gate * jnp.dot(p_ref[...].astype(BF16), wpp_ref[...], preferred_element_type=F32)


def _ffn_body(x_ref, p_ref, wg_ref, wu_ref, wd_ref, wpg_ref, wpp_ref, g_ref, b_ref, o_ref, acc):
    x1 = x_ref[...]
    xb = x1.astype(BF16)
    for c, (c0, cw) in enumerate(FFN_CHUNKS):
        gt = jnp.dot(xb, wg_ref[:, c0:c0 + cw], preferred_element_type=F32)
        up = jnp.dot(xb, wu_ref[:, c0:c0 + cw], preferred_element_type=F32)
        act = (gt * _sigmoid(gt) * up).astype(BF16)
        contrib = jnp.dot(act, wd_ref[c0:c0 + cw, :], preferred_element_type=F32)
        if c == 0:
            acc[...] = contrib
        else:
            acc[...] += contrib
    h = DEEPNORM_ALPHA * x1 + acc[...] + _ple(xb, p_ref, wpg_ref, wpp_ref)
    o_ref[...] = _layer_norm(h, g_ref[...], b_ref[...])


def _resident(shape, index_map):
    return pl.BlockSpec(shape, index_map, pipeline_mode=pl.Buffered(1))


def _ple_rows_spec(p, tm):
    return pl.BlockSpec((None, tm, p.arr.shape[2]), lambda i: (p.idx, i, 0))


def _dense_layer_body(x_ref, a_ref, s_ref, p_ref, wg_ref, wba_ref, wbs_ref, wo_ref, g1_ref, b1_ref,
                      fg_ref, fu_ref, fd_ref, wpg_ref, wpp_ref, g2_ref, b2_ref, w_next_ref,
                      o_ref, q_ref, kv_ref, u_ref, x1_buf, acc):
    _merge_body(x_ref, a_ref, s_ref, wg_ref, wba_ref, wbs_ref, wo_ref, g1_ref, b1_ref, x1_buf, with_router=False)
    _ffn_body(x1_buf, p_ref, fg_ref, fu_ref, fd_ref, wpg_ref, wpp_ref, g2_ref, b2_ref, o_ref, acc)
    _project(o_ref[...], w_next_ref, q_ref, kv_ref, u_ref)


def _dense_layer(x, a, s, p, wg, wba, wbs, wo, g1, b1, w_gu, w_down, li, wpg, wpp, g2, b2, w_next, *, tm):
    t, d = x.shape
    ff = w_down.shape[1]
    row = lambda i: (i, 0)
    merge_params = [wg, wba, wbs, wo, g1, b1]
    tail_params = [wpg, wpp, g2, b2, w_next]
    proj_specs, proj_shapes = _proj_outputs(t, tm, w_next.arr.shape[2])
    return pl.pallas_call(
        _dense_layer_body,
        grid=(t // tm,),
        in_specs=[pl.BlockSpec((tm, d), row), pl.BlockSpec((tm, a.shape[1]), row), pl.BlockSpec((tm, s.shape[1]), row),
                  _ple_rows_spec(p, tm)]
        + [_layer_spec(q, resident=True) for q in merge_params]
        + [_resident((None, d, ff), lambda i: (li, 0, 0)),
           _resident((None, d, ff), lambda i: (li, 0, 1)),
           _resident((None, ff, d), lambda i: (li, 0, 0))]
        + [_layer_spec(q, resident=True) for q in tail_params],
        out_specs=[pl.BlockSpec((tm, d), row)] + proj_specs,
        out_shape=[jax.ShapeDtypeStruct((t, d), F32)] + proj_shapes,
        scratch_shapes=[pltpu.VMEM((tm, d), F32), pltpu.VMEM((tm, d), F32)],
        compiler_params=_cparams(("parallel",)),
        name="dense_layer",
    )(x, a, s, p.arr, *[q.arr for q in merge_params], w_gu, w_gu, w_down, *[q.arr for q in tail_params])


def _moe_body(te_ref, nu_ref, src_ref, dst_ref, x_hbm, wgu_ref, wd_ref, y_hbm,
              xg, xb, acc, ybuf, gsem, ssem, *, tm, fw, spare_row):
    i = pl.program_id(0)
    nu = nu_ref[0]
    slot = i % 2
    other = 1 - slot
    ffe = wd_ref.shape[0]
    nc = ffe // fw
    per_chunk = tm // nc

    def gather(tile, r, buf):
        row = src_ref[tile * tm + r]
        pltpu.make_async_copy(x_hbm.at[pl.ds(row, 1), :], xg.at[buf, pl.ds(r, 1), :], gsem.at[buf]).start()

    def scatter(row, r, buf):
        pltpu.make_async_copy(ybuf.at[buf, pl.ds(r, 1), :], y_hbm.at[pl.ds(row, 1), :], ssem.at[buf]).start()

    def prev_tile_dst(r):
        return jnp.where(i > 0, dst_ref[jnp.maximum(i - 1, 0) * tm + r], spare_row + r)

    @pl.when(i == 0)
    def _():
        ybuf[...] = jnp.zeros_like(ybuf)

        def one(r, c):
            gather(0, r, 0)
            return c
        lax.fori_loop(0, tm, one, 0)

    @pl.when(i <= nu)
    def _():
        pltpu.make_async_copy(x_hbm.at[pl.ds(0, tm), :], xg.at[slot], gsem.at[slot]).wait()

    @pl.when(i < nu)
    def _():
        xb[...] = xg[slot].astype(BF16)
        x = xb[...]
        for c in range(nc):
            c0 = c * fw
            gt = jnp.dot(x, wgu_ref[:, c0:c0 + fw], preferred_element_type=F32)
            up = jnp.dot(x, wgu_ref[:, ffe + c0:ffe + c0 + fw], preferred_element_type=F32)
            act = (gt * _sigmoid(gt) * up).astype(BF16)
            part = jnp.dot(act, wd_ref[c0:c0 + fw, :], preferred_element_type=F32)
            if c == 0:
                acc[...] = part
            else:
                acc[...] += part
            r1 = tm if c == nc - 1 else (c + 1) * per_chunk
            for r in range(c * per_chunk, r1):
                gather(i + 1, r, other)
                scatter(prev_tile_dst(r), r, other)

    @pl.when(i == nu)
    def _():
        def one(r, c):
            scatter(dst_ref[(i - 1) * tm + r], r, other)
            return c
        lax.fori_loop(0, tm, one, 0)

    @pl.when((i >= 1) & (i <= nu + 1))
    def _():
        pltpu.make_async_copy(ybuf.at[slot], y_hbm.at[pl.ds(0, tm), :], ssem.at[slot]).wait()

    @pl.when(i < nu)
    def _():
        ybuf[slot] = _pack_bf16_pairs(acc[...])


def _moe_experts(x1, w_gu, w_down, tile_expert, n_used, src, dst, *, tm, fw, out_rows):
    t, d = x1.shape
    ffe = w_down.shape[1]
    nt = src.shape[0] // tm
    return pl.pallas_call(
        functools.partial(_moe_body, tm=tm, fw=fw, spare_row=out_rows - tm),
        grid_spec=pltpu.PrefetchScalarGridSpec(
            num_scalar_prefetch=4,
            grid=(nt,),
            in_specs=[pl.BlockSpec(memory_space=pl.ANY),
                      _resident((None, d, 2 * ffe), lambda i, te, nu, s, ds: (te[i], 0, 0)),
                      _resident((None, ffe, d), lambda i, te, nu, s, ds: (te[i], 0, 0))],
            out_specs=pl.BlockSpec(memory_space=pl.ANY),
            scratch_shapes=[pltpu.VMEM((2, tm, d), F32), pltpu.VMEM((tm, d), BF16), pltpu.VMEM((tm, d), F32),
                            pltpu.VMEM((2, tm, d // 2), jnp.uint32),
                            pltpu.SemaphoreType.DMA((2,)), pltpu.SemaphoreType.DMA((2,))]),
        out_shape=jax.ShapeDtypeStruct((out_rows, d // 2), jnp.uint32),
        compiler_params=_cparams(("arbitrary",)),
        name="moe_experts",
    )(tile_expert, n_used, src, dst, x1, w_gu, w_down)


def _moe_out_body(x_ref, route_ref, y0_ref, y1_ref, p_ref, wpg_ref, wpp_ref, g_ref, b_ref, *rest):
    o_ref = rest[0] if len(rest) == 1 else rest[1]
    x1 = x_ref[...]
    route = route_ref[...]
    ffn = route[:, 2:3] * _unpack_bf16_pairs(y0_ref[...]) + route[:, 3:4] * _unpack_bf16_pairs(y1_ref[...])
    h = DEEPNORM_ALPHA * x1 + ffn + _ple(x1.astype(BF16), p_ref, wpg_ref, wpp_ref)
    x2 = _layer_norm(h, g_ref[...], b_ref[...])
    o_ref[...] = x2
    if len(rest) > 1:
        _project(x2, rest[0], *rest[2:])


def _moe_out(x1, route, p, y, wpg, wpp, g, b, w_next=None, *, tm):
    t, d = x1.shape
    nt = t // tm
    row = lambda i: (i, 0)
    params = [wpg, wpp, g, b] + ([w_next] if w_next is not None else [])
    out_specs, out_shape = [pl.BlockSpec((tm, d), row)], [jax.ShapeDtypeStruct((t, d), F32)]
    if w_next is not None:
        specs, shapes = _proj_outputs(t, tm, w_next.arr.shape[2])
        out_specs, out_shape = out_specs + specs, out_shape + shapes
    out = pl.pallas_call(
        _moe_out_body,
        grid=(nt,),
        in_specs=[pl.BlockSpec((tm, d), row), pl.BlockSpec((tm, LANES), row),
                  pl.BlockSpec((tm, d // 2), row), pl.BlockSpec((tm, d // 2), lambda i: (nt + i, 0)),
                  _ple_rows_spec(p, tm)] + [_layer_spec(q) for q in params],
        out_specs=out_specs, out_shape=out_shape,
        compiler_params=_cparams(("parallel",)),
        name="moe_combine_ln2",
    )(x1, route, y, y, p.arr, *[q.arr for q in params])
    return out[0] if w_next is None else out


def _slot_table_body(pos_ref, init_hbm, dst_ref, sem, *, t, groups):
    @pl.when(pl.program_id(0) == 0)
    def _():
        copy = pltpu.make_async_copy(init_hbm, dst_ref, sem)
        copy.start()
        copy.wait()

    batch = SLOT_TABLE_BATCH
    base = pl.program_id(0) * (t // groups)

    def put(kb, c):
        tok0 = base + kb * batch
        ps = [pos_ref[2 * tok0 + j] for j in range(2 * batch)]
        for j, p in enumerate(ps):
            dst_ref[p] = (j % 2) * t + tok0 + j // 2
        return c
    lax.fori_loop(0, t // groups // batch, put, 0)


def _slot_table(pos, *, t, tm, r):
    groups = SLOT_TABLE_GROUPS
    assert t % (groups * SLOT_TABLE_BATCH) == 0
    init = 2 * t + jnp.arange(r, dtype=jnp.int32) % tm
    smem = pl.BlockSpec(memory_space=pltpu.SMEM)
    return pl.pallas_call(
        functools.partial(_slot_table_body, t=t, groups=groups),
        grid=(groups,),
        in_specs=[smem, pl.BlockSpec(memory_space=pl.ANY)], out_specs=smem,
        out_shape=jax.ShapeDtypeStruct((r,), jnp.int32),
        scratch_shapes=[pltpu.SemaphoreType.DMA],
        compiler_params=_cparams(("arbitrary",)),
        name="moe_slot_table",
    )(pos, init)


def _routing_tables(route, *, tm, n_experts):
    t = route.shape[0]
    eids = route[:, :2].astype(jnp.int32).reshape(-1)
    onehot = (eids[:, None] == jnp.arange(n_experts, dtype=jnp.int32)[None, :]).astype(jnp.int32)
    csum = jnp.cumsum(onehot, axis=0)
    rank = jnp.sum((csum - onehot) * onehot, axis=1)
    cnt = csum[-1]
    ntile = (cnt + tm - 1) // tm
    tile_end = jnp.cumsum(ntile)
    row_off = (tile_end - ntile) * tm
    pos = jnp.sum(onehot * row_off[None, :], axis=1) + rank
    r = 2 * t + (n_experts + 2) * tm
    dst = _slot_table(pos.astype(jnp.int32), t=t, tm=tm, r=r)
    src = jnp.where(dst < 2 * t, dst % t, 0)
    n_used = tile_end[-1:]
    tiles = jnp.arange(r // tm, dtype=jnp.int32)
    te = jnp.sum((tiles[:, None] >= tile_end[None, :]).astype(jnp.int32), axis=1)
    last = jnp.sum((n_used - 1 >= tile_end).astype(jnp.int32))
    te = jnp.where(tiles < n_used, te, last).astype(jnp.int32)
    out_rows = 2 * t + tm
    return te, n_used.astype(jnp.int32), src.astype(jnp.int32), dst.astype(jnp.int32), out_rows


def kernel(x, p, rel_bias, w_in, attn_sinks, ssm_lambda_re, ssm_lambda_im, ssm_log_dt, ssm_b_re, ssm_b_im,
           ssm_c_re, ssm_c_im, ssm_d, w_glu, b_glu, w_branch_attn, w_branch_ssm, w_out, ln1_g, ln1_b,
           ffn_w_gate_up, ffn_w_down, moe_router, moe_w_gate_up, moe_w_down, ple_w_proj, ple_w_gate,
           ln2_g, ln2_b):
    bsz, seq, d = x.shape
    t = bsz * seq
    depth = w_in.shape[0]

    xt = x.reshape(t, d)
    pt = p.reshape(depth, t, p.shape[-1])

    bias = _attn_bias_table(rel_bias)
    sinks = attn_sinks * LOG2_E
    ffn_gu = ffn_w_gate_up.astype(BF16)
    ffn_dn = ffn_w_down.astype(BF16)

    aw, kw = ATTN_WIDTH, KV_WIDTH
    u0 = aw + 2 * kw
    w_proj = jnp.concatenate([_q_head_tiles(w_in[:, :, :aw]), w_in[:, :, aw:u0 + SSM_WIDTH]], axis=2).astype(BF16)
    w_gates = w_in[:, :, u0 + SSM_WIDTH:].astype(BF16)
    wba = w_branch_attn.reshape(depth, N_Q_HEADS, HEAD_DIM, d)[:, np.array(_attn_out_head_order())]
    wba = wba.reshape(depth, aw, d).astype(BF16)
    wbs, wo = w_branch_ssm.astype(BF16), w_out.astype(BF16)
    wpg, wpp = ple_w_gate.astype(BF16), ple_w_proj.astype(BF16)
    g1, b1 = ln1_g.reshape(depth, 1, d), ln1_b.reshape(depth, 1, d)
    g2, b2 = ln2_g.reshape(depth, 1, d), ln2_b.reshape(depth, 1, d)

    lbr, lbi, bbr, bbi = _ssm_params(ssm_lambda_re, ssm_lambda_im, ssm_log_dt, ssm_b_re, ssm_b_im)
    bre = _block_diag_halves(bbr).astype(BF16)
    bim = _block_diag_halves(bbi).astype(BF16)
    cre = _block_diag_halves(jnp.transpose(ssm_c_re, (0, 1, 3, 2))).astype(BF16)
    cim = _block_diag_halves(jnp.transpose(ssm_c_im, (0, 1, 3, 2))).astype(BF16)
    ssm_skip = ssm_d.reshape(depth, 1, -1)
    wglu, bglu = w_glu.astype(BF16), b_glu.reshape(depth, 1, -1)

    rt = jnp.pad(moe_router, ((0, 0), (0, 0), (0, LANES - N_EXPERTS)))
    rhi = rt.astype(BF16)
    rcat = jnp.concatenate([rhi, (rt - rhi.astype(F32)).astype(BF16)], axis=2)

    q, kv, u = _inproj(xt, _Layer(w_proj, 0), tm=TM_PROJ)
    for i in range(depth):
        at = lambda arr, idx=i: _Layer(arr, idx)
        w_next = _Layer(w_proj, i + 1) if i + 1 < depth else None
        a_out = _attention(q, kv, at(sinks), bias, seq=seq, batch=bsz)
        moe_cast = _Layer(moe_w_gate_up if i % 2 == 0 else moe_w_down, i // 2)
        s_out, moe_w = _ssm(u, at(bre), at(bim), at(cre), at(cim), at(lbr), at(lbi), at(ssm_skip), at(wglu),
                            at(bglu), moe_cast, batch=bsz, steps=SSM_STEPS, lane_chunk=SSM_LANE_CHUNK)
        if i % 2 == 0:
            moe_gu = moe_w
        merge_w = (at(w_gates), at(wba), at(wbs), at(wo), at(g1), at(b1))
        if i % 2 == 0:
            xt, q, kv, u = _dense_layer(xt, a_out, s_out, at(pt), *merge_w, ffn_gu, ffn_dn, i // 2,
                                        at(wpg), at(wpp), at(g2), at(b2), w_next, tm=TM_FFN)
        else:
            router = (_Layer(rcat, i // 2),)
            x1, route = _merge(xt, a_out, s_out, *merge_w, router, tm=TM_MERGE)
            te, n_used, src, dst, out_rows = _routing_tables(route, tm=TM_MOE, n_experts=N_EXPERTS)
            y = _moe_experts(x1, moe_gu, moe_w, te, n_used, src, dst, tm=TM_MOE, fw=FW_MOE, out_rows=out_rows)
            out = _moe_out(x1, route, at(pt), y, at(wpg), at(wpp), at(g2), at(b2), w_next, tm=TM_FFN)
            xt, q, kv, u = out if w_next is not None else (out, None, None, None)

    return xt.reshape(bsz, seq, d)
```

```python
import functools
import math
from typing import NamedTuple

import jax
import jax.numpy as jnp
import numpy as np
from jax import lax
from jax.experimental import pallas as pl
from jax.experimental.pallas import tpu as pltpu

F32 = jnp.float32
BF16 = jnp.bfloat16

DEPTH = 4
HEAD_DIM = 64
N_Q_HEADS = 8
N_KV_HEADS = 2
ATTN_WIDTH = N_Q_HEADS * HEAD_DIM
KV_WIDTH = N_KV_HEADS * HEAD_DIM
ATTN_BLOCK = 128
N_BUCKETS = 32
MAX_DISTANCE = 128
SSM_WIDTH = 512
N_EXPERTS = 8
DEEPNORM_ALPHA = (2 * DEPTH) ** 0.25
LN_EPS = 1e-5
NEG_INF = -1e30
LOG2_E = math.log2(math.e)

LANES = 128
BF16_TILE_ROWS = 16
VMEM_LIMIT_BYTES = 56 * 1024 * 1024

TM_PROJ = 1024
ATTN_BLOCKS_PER_STEP = 8
TM_MERGE = 512
MERGE_ROW_GROUPS = 2
TM_FFN = 512
TM_MOE = 512
FW_MOE = 1792
SLOT_TABLE_GROUPS = 16
SLOT_TABLE_BATCH = 8
SSM_STEPS = 32
SSM_PERM_STEPS = BF16_TILE_ROWS
SSM_LANE_CHUNK = 512
FFN_CHUNKS = ((0, 768), (768, 768), (1536, 768), (2304, 512))


def _cparams(sem):
    return pltpu.CompilerParams(dimension_semantics=sem, vmem_limit_bytes=VMEM_LIMIT_BYTES)


class _Layer(NamedTuple):
    arr: jax.Array
    idx: int


def _layer_spec(p, resident=False):
    tail = (0,) * (p.arr.ndim - 1)
    mode = {"pipeline_mode": pl.Buffered(1)} if resident else {}
    return pl.BlockSpec((None,) + p.arr.shape[1:], lambda *_: (p.idx,) + tail, **mode)


def _layer_norm(h, g, b):
    mu = jnp.mean(h, axis=-1, keepdims=True)
    c = h - mu
    var = jnp.mean(c * c, axis=-1, keepdims=True)
    return c * lax.rsqrt(var + LN_EPS) * g + b


def _sigmoid(v):
    return 1.0 / (1.0 + jnp.exp(-v))


def _pack_bf16_pairs(v):
    c = v.shape[1] // 2
    lo = lax.bitcast_convert_type(v[:, :c].astype(BF16).astype(F32), jnp.uint32)
    hi = lax.bitcast_convert_type(v[:, c:].astype(BF16).astype(F32), jnp.uint32)
    return (lo >> 16) | (hi & jnp.uint32(0xFFFF0000))


def _unpack_bf16_pairs(w):
    lo = lax.bitcast_convert_type(w << 16, F32)
    hi = lax.bitcast_convert_type(w & jnp.uint32(0xFFFF0000), F32)
    return jnp.concatenate([lo, hi], axis=1)


def _gelu_tanh(v):
    return 0.5 * v * (1.0 + jnp.tanh(math.sqrt(2.0 / math.pi) * (v + 0.044715 * (v * v * v))))


def _inproj_body(x_ref, w_ref, q_ref, kv_ref, u_ref):
    _project(x_ref[...], w_ref, q_ref, kv_ref, u_ref)


def _project(x, w_ref, q_ref, kv_ref, u_ref):
    z = jnp.dot(x.astype(BF16), w_ref[...], preferred_element_type=F32)
    aw = q_ref.shape[1]
    kw = kv_ref.shape[1]
    q_ref[...] = (z[:, :aw] * (LOG2_E * HEAD_DIM ** -0.5)).astype(BF16)
    kv_ref[...] = z[:, aw:aw + kw].astype(BF16)
    u_ref[...] = z[:, aw + kw:].astype(BF16)


def _proj_outputs(t, tm, n):
    aw, kw = N_Q_HEADS * LANES, 2 * KV_WIDTH
    widths = (aw, kw, n - aw - kw)
    specs = [pl.BlockSpec((tm, w), lambda i: (i, 0)) for w in widths]
    shapes = [jax.ShapeDtypeStruct((t, w), BF16) for w in widths]
    return specs, shapes


def _inproj(x, w, *, tm):
    t, d = x.shape
    n = w.arr.shape[2]
    aw, kw = N_Q_HEADS * LANES, 2 * KV_WIDTH
    uw = n - aw - kw
    return pl.pallas_call(
        _inproj_body,
        grid=(t // tm,),
        in_specs=[pl.BlockSpec((tm, d), lambda i: (i, 0)), _layer_spec(w)],
        out_specs=[pl.BlockSpec((tm, aw), lambda i: (i, 0)),
                   pl.BlockSpec((tm, kw), lambda i: (i, 0)),
                   pl.BlockSpec((tm, uw), lambda i: (i, 0))],
        out_shape=[jax.ShapeDtypeStruct((t, aw), BF16),
                   jax.ShapeDtypeStruct((t, kw), BF16),
                   jax.ShapeDtypeStruct((t, uw), BF16)],
        compiler_params=_cparams(("parallel",)),
        name="inproj",
    )(x, w.arr)


def _attn_body(sink_ref, q_ref, kvc_ref, kvp_ref, bias0_ref, bias_ref, o_ref, *, blk, nbs, layer):
    group = N_Q_HEADS // N_KV_HEADS
    ks = slice(0, LANES)
    vs = slice(LANES, 2 * LANES)
    low = lax.broadcasted_iota(jnp.int32, (blk, LANES), 1) < HEAD_DIM
    for j in range(nbs):
        rows = slice(j * blk, (j + 1) * blk)
        b_ref = bias0_ref if j == 0 else bias_ref
        if j == 0:
            k = jnp.concatenate([kvp_ref[:, ks], kvc_ref[0:blk, ks]], axis=0)
            v = jnp.concatenate([kvp_ref[:, vs], kvc_ref[0:blk, vs]], axis=0)
        else:
            k = kvc_ref[(j - 1) * blk:(j + 1) * blk, ks]
            v = kvc_ref[(j - 1) * blk:(j + 1) * blk, vs]
        for tile in range(group):
            halves = []
            for g in range(N_KV_HEADS):
                h = tile + group * g
                qt = q_ref[rows, h * LANES:(h + 1) * LANES]
                s = lax.dot_general(qt, k, (((1,), (1,)), ((), ())), preferred_element_type=F32)
                s = s + b_ref[h]
                sink = sink_ref[layer, h]
                m = jnp.maximum(jnp.max(s, axis=-1, keepdims=True), sink)
                e = jnp.exp2(s - m)
                denom = jnp.sum(e, axis=-1, keepdims=True) + jnp.exp2(sink - m)
                halves.append(jnp.dot(e.astype(BF16), v, preferred_element_type=F32) / denom)
            o_ref[rows, tile * LANES:(tile + 1) * LANES] = jnp.where(low, halves[0], halves[1]).astype(BF16)


def _attention(q, kv, sinks, bias, *, seq, batch):
    assert KV_WIDTH == LANES
    t = q.shape[0]
    blk = ATTN_BLOCK
    nbs = ATTN_BLOCKS_PER_STEP
    ns = seq // (blk * nbs)
    qw, kw, aw = q.shape[1], kv.shape[1], ATTN_WIDTH
    table = (None, N_Q_HEADS, blk, 2 * blk)
    return pl.pallas_call(
        functools.partial(_attn_body, blk=blk, nbs=nbs, layer=sinks.idx),
        grid=(batch, ns),
        in_specs=[pl.BlockSpec(memory_space=pltpu.SMEM),
                  pl.BlockSpec((blk * nbs, qw), lambda b, n: (b * ns + n, 0)),
                  pl.BlockSpec((blk * nbs, kw), lambda b, n: (b * ns + n, 0)),
                  pl.BlockSpec((blk, kw), lambda b, n: ((b * ns + n) * nbs - jnp.minimum(n, 1), 0)),
                  pl.BlockSpec(table, lambda b, n: (jnp.minimum(n, 1), 0, 0, 0)),
                  pl.BlockSpec(table, lambda b, n: (1, 0, 0, 0))],
        out_specs=pl.BlockSpec((blk * nbs, aw), lambda b, n: (b * ns + n, 0)),
        out_shape=jax.ShapeDtypeStruct((t, aw), BF16),
        compiler_params=_cparams(("parallel", "parallel")),
        name="swa_attention",
    )(sinks.arr, q, kv, kv, bias, bias)


def _q_head_tiles(w):
    depth, d, _ = w.shape
    group = N_Q_HEADS // N_KV_HEADS
    wh = w.reshape(depth, d, N_Q_HEADS, 1, HEAD_DIM)
    half = jnp.arange(N_KV_HEADS)[None, None, None, :, None]
    kv_head = (jnp.arange(N_Q_HEADS) // group)[None, None, :, None, None]
    tiles = jnp.where(half == kv_head, wh, 0.0)
    return tiles.reshape(depth, d, N_Q_HEADS * LANES)


def _attn_out_head_order():
    group = N_Q_HEADS // N_KV_HEADS
    return [j + group * half for j in range(group) for half in range(N_KV_HEADS)]


def _attn_bias_table(rel_bias):
    blk = ATTN_BLOCK
    d = np.arange(blk, dtype=np.int32)
    max_exact = N_BUCKETS // 2
    d_f = np.maximum(d, 1).astype(np.float32)
    large = max_exact + (np.log(d_f / np.float32(max_exact)) / np.float32(math.log(MAX_DISTANCE / max_exact))
                         * np.float32(N_BUCKETS - max_exact)).astype(np.int32)
    large = np.minimum(large, N_BUCKETS - 1)
    bucket = np.where(d < max_exact, d, large)
    vals = jnp.transpose(rel_bias.astype(F32)[bucket], (1, 0))
    h = vals.shape[0]
    neg = jnp.full((h, blk), NEG_INF, F32)
    strip = jnp.concatenate([neg, vals[:, ::-1], neg], axis=1)
    rows = [strip[:, blk - 1 - i:blk - 1 - i + 2 * blk] for i in range(blk)]
    table = jnp.stack(rows, axis=1) * LOG2_E
    first = jnp.where(np.arange(2 * blk)[None, None, :] < blk, NEG_INF, table)
    return jnp.stack([first, table], axis=0)


def _ssm_param_body(lre_ref, lim_ref, ldt_ref, bre_ref, bim_ref, lbr_ref, lbi_ref, bbr_ref, bbi_ref):
    lre, lim = lre_ref[...], lim_ref[...]
    dt = jnp.exp(ldt_ref[...])
    mag = jnp.exp(lre * dt)
    lbr = mag * jnp.cos(lim * dt)
    lbi = mag * jnp.sin(lim * dt)
    lbr_ref[...] = lbr
    lbi_ref[...] = lbi
    nr, ni = lbr - 1.0, lbi
    inv = 1.0 / (lre * lre + lim * lim)
    cr = (nr * lre + ni * lim) * inv
    ci = (ni * lre - nr * lim) * inv
    br, bi = bre_ref[...], bim_ref[...]
    bbr_ref[...] = cr * br - ci * bi
    bbi_ref[...] = cr * bi + ci * br


def _ssm_params(lam_re, lam_im, log_dt, b_re, b_im):
    depth, g, p = lam_re.shape
    h = b_re.shape[3]
    n = depth * g
    ldt = jnp.broadcast_to(log_dt.reshape(n, 1, 1), (n, 1, p))
    brt = jnp.transpose(b_re, (0, 1, 3, 2)).reshape(n, h, p)
    bit = jnp.transpose(b_im, (0, 1, 3, 2)).reshape(n, h, p)
    sd = jax.ShapeDtypeStruct
    lbr, lbi, bbr, bbi = pl.pallas_call(
        _ssm_param_body,
        out_shape=[sd((n, 1, p), F32), sd((n, 1, p), F32), sd((n, h, p), F32), sd((n, h, p), F32)],
        name="ssm_params",
    )(lam_re.reshape(n, 1, p), lam_im.reshape(n, 1, p), ldt, brt, bit)
    return (lbr.reshape(depth, 1, g * p), lbi.reshape(depth, 1, g * p),
            bbr.reshape(depth, g, h, p), bbi.reshape(depth, g, h, p))


def _block_diag_halves(w):
    depth, g, a, b = w.shape
    hg = g // 2
    wh = w.reshape(depth, 2, hg, a, b)
    eye = jnp.eye(hg, dtype=bool)[None, None, :, None, :, None]
    full = jnp.where(eye, wh[:, :, :, :, None, :], 0.0)
    return full.reshape(depth, 2, hg * a, hg * b)


def _ssm_body(u_ref, perm_ref, permt_ref, bre_ref, bim_ref, cre_ref, cim_ref, lr_ref, li_ref, d_ref,
              wglu_ref, bglu_ref, *rest, steps, batch, lane_chunk, with_cast):
    if with_cast:
        wsrc_ref, o_ref, wdst_ref, xr0, xi0, ub0, xr1, xi1, ub1, sr, si = rest
    else:
        o_ref, xr0, xi0, ub0, xr1, xi1, ub1, sr, si = rest
    s = pl.program_id(0)

    @pl.when(s == 0)
    def _():
        for ref in (xr0, xi0, ub0, xr1, xi1, ub1, sr, si):
            ref[...] = jnp.zeros_like(ref)

    sub = SSM_PERM_STEPS
    rows_g = sub * batch

    def stage(fill, scan):
        xr_f, xi_f, ub_f = fill
        xr, xi, ub = scan
        parts = []
        for g in range(steps // sub):
            bm = jnp.concatenate([u_ref[b, g * sub:(g + 1) * sub, :] for b in range(batch)], axis=0)
            parts.append(jnp.dot(perm_ref[...], bm, preferred_element_type=F32).astype(BF16))
        u_new = jnp.concatenate(parts, axis=0)
        ub_f[...] = u_new
        hw = u_new.shape[1] // 2
        hs = xr.shape[1] // 2
        for hf in range(2):
            uh = u_new[:, hf * hw:(hf + 1) * hw]
            xr_f[:, hf * hs:(hf + 1) * hs] = jnp.dot(uh, bre_ref[hf], preferred_element_type=F32)
            xi_f[:, hf * hs:(hf + 1) * hs] = jnp.dot(uh, bim_ref[hf], preferred_element_type=F32)
        for c in range(xr.shape[1] // lane_chunk):
            cs = slice(c * lane_chunk, (c + 1) * lane_chunk)
            ar = lr_ref[:, cs]
            ai = li_ref[:, cs]
            pr, pi = sr[:, cs], si[:, cs]
            for t in range(steps):
                rows = slice(t * batch, (t + 1) * batch)
                nr = ar * pr - ai * pi + xr[rows, cs]
                ni = ar * pi + ai * pr + xi[rows, cs]
                xr[rows, cs] = nr
                xi[rows, cs] = ni
                pr, pi = nr, ni
            sr[:, cs] = pr
            si[:, cs] = pi
        ys = []
        for hf in range(2):
            xrb = xr[:, hf * hs:(hf + 1) * hs].astype(BF16)
            xib = xi[:, hf * hs:(hf + 1) * hs].astype(BF16)
            ys.append(jnp.dot(xrb, cre_ref[hf], preferred_element_type=F32)
                      - jnp.dot(xib, cim_ref[hf], preferred_element_type=F32))
        y = jnp.concatenate(ys, axis=1) + d_ref[...] * ub[...].astype(F32)
        y = _gelu_tanh(y)
        z = jnp.dot(y.astype(BF16), wglu_ref[...], preferred_element_type=F32) + bglu_ref[...]
        out = (y * _sigmoid(z)).astype(BF16)
        for g in range(steps // sub):
            bm = jnp.dot(permt_ref[...], out[g * rows_g:(g + 1) * rows_g], preferred_element_type=F32).astype(BF16)
            for b in range(batch):
                o_ref[b, g * sub:(g + 1) * sub, :] = bm[b * sub:(b + 1) * sub]
        if with_cast:
            wdst_ref[...] = wsrc_ref[...].astype(BF16)

    @pl.when(s % 2 == 0)
    def _():
        stage((xr0, xi0, ub0), (xr1, xi1, ub1))

    @pl.when(s % 2 == 1)
    def _():
        stage((xr1, xi1, ub1), (xr0, xi0, ub0))


def _time_major_permutation(batch, sub):
    n = batch * sub
    r = np.arange(n)
    p = np.zeros((n, n), np.float32)
    p[r, (r % batch) * sub + r // batch] = 1.0
    return p


def _ssm(u, bre, bim, cre, cim, lbr, lbi, d, wglu, bglu, cast=None, *, batch, steps, lane_chunk):
    t, w = u.shape
    seq = t // batch
    n = seq // steps
    ns = lbr.arr.shape[2]
    rows = steps * batch
    perm = _time_major_permutation(batch, SSM_PERM_STEPS)
    const2 = lambda i: (0, 0)
    params = (bre, bim, cre, cim, lbr, lbi, d, wglu, bglu)
    buf = [pltpu.VMEM((rows, ns), F32), pltpu.VMEM((rows, ns), F32), pltpu.VMEM((rows, w), BF16)]
    in_specs = [pl.BlockSpec((batch, steps, w), lambda i: (0, jnp.minimum(i, n - 1), 0)),
                pl.BlockSpec(perm.shape, const2), pl.BlockSpec(perm.shape, const2)]
    in_specs += [_layer_spec(p) for p in params]
    args = [u.reshape(batch, seq, w), jnp.asarray(perm, BF16), jnp.asarray(perm.T, BF16)] + [p.arr for p in params]
    out_specs = [pl.BlockSpec((batch, steps, w), lambda i: (0, jnp.maximum(i - 1, 0), 0))]
    out_shape = [jax.ShapeDtypeStruct((batch, seq, w), BF16)]
    if cast is not None:
        wsrc = cast
        _, ne, r, c = wsrc.arr.shape
        per_e = (n + 1) // ne
        rb = r // per_e
        nblk = ne * per_e
        assert per_e >= 1 and r % per_e == 0 and rb % BF16_TILE_ROWS == 0
        blk = lambda i: jnp.minimum(i, nblk - 1)
        in_specs.append(pl.BlockSpec((None, None, rb, c), lambda i: (wsrc.idx, blk(i) // per_e, blk(i) % per_e, 0)))
        args.append(wsrc.arr)
        out_specs.append(pl.BlockSpec((None, rb, c), lambda i: (blk(i) // per_e, blk(i) % per_e, 0)))
        out_shape.append(jax.ShapeDtypeStruct((ne, r, c), BF16))
    out = pl.pallas_call(
        functools.partial(_ssm_body, steps=steps, batch=batch, lane_chunk=lane_chunk, with_cast=cast is not None),
        grid=(n + 1,),
        in_specs=in_specs, out_specs=out_specs, out_shape=out_shape,
        scratch_shapes=buf + buf + [pltpu.VMEM((batch, ns), F32), pltpu.VMEM((batch, ns), F32)],
        compiler_params=_cparams(("arbitrary",)),
        name="s5_scan",
    )(*args)
    if cast is None:
        return out[0].reshape(t, w)
    return out[0].reshape(t, w), out[1]


def _split_bf16(v):
    hi = v.astype(BF16)
    lo = (v - hi.astype(F32)).astype(BF16)
    return hi, lo


def _merge_body(x_ref, a_ref, s_ref, wg_ref, wba_ref, wbs_ref, wo_ref, g_ref, b_ref, *rest, with_router):
    if with_router:
        rcat_ref, x1_ref, route_ref = rest
    else:
        (x1_ref,) = rest
    tm, d = x_ref.shape
    rg = tm // MERGE_ROW_GROUPS
    for grp in range(MERGE_ROW_GROUPS):
        rows = slice(grp * rg, (grp + 1) * rg)
        x = x_ref[rows, :]
        gates = jnp.dot(x.astype(BF16), wg_ref[...], preferred_element_type=F32)
        pa = jnp.dot(a_ref[rows, :], wba_ref[...], preferred_element_type=F32)
        ps = jnp.dot(s_ref[rows, :], wbs_ref[...], preferred_element_type=F32)
        merged = _sigmoid(gates[:, :d]) * pa + _sigmoid(gates[:, d:]) * ps
        y = jnp.dot(merged.astype(BF16), wo_ref[...], preferred_element_type=F32)
        x1 = _layer_norm(DEEPNORM_ALPHA * x + y, g_ref[...], b_ref[...])
        x1_ref[rows, :] = x1
        if with_router:
            hi, lo = _split_bf16(x1)
            both = jnp.dot(hi, rcat_ref[...], preferred_element_type=F32)
            logits = (both[:, :LANES] + jnp.dot(lo, rcat_ref[:, :LANES], preferred_element_type=F32)
                      + both[:, LANES:])
            lane = lax.broadcasted_iota(jnp.int32, logits.shape, 1).astype(F32)
            big = float(LANES)
            l1 = jnp.where(lane < N_EXPERTS, logits, -jnp.inf)
            m1 = jnp.max(l1, axis=-1, keepdims=True)
            i1 = jnp.min(jnp.where(l1 == m1, lane, big), axis=-1, keepdims=True)
            l2 = jnp.where(lane == i1, -jnp.inf, l1)
            m2 = jnp.max(l2, axis=-1, keepdims=True)
            i2 = jnp.min(jnp.where(l2 == m2, lane, big), axis=-1, keepdims=True)
            ed = jnp.exp(m2 - m1)
            g1 = 1.0 / (1.0 + ed)
            g2 = ed / (1.0 + ed)
            route = jnp.where(lane == 0.0, i1,
                              jnp.where(lane == 1.0, i2,
                                        jnp.where(lane == 2.0, g1, jnp.where(lane == 3.0, g2, 0.0))))
            route_ref[rows, :] = route


def _merge(x, a, s, wg, wba, wbs, wo, g, b, router=None, *, tm):
    t, d = x.shape
    row = lambda i: (i, 0)
    params = [wg, wba, wbs, wo, g, b] + list(router or ())
    in_specs = [pl.BlockSpec((tm, d), row), pl.BlockSpec((tm, a.shape[1]), row), pl.BlockSpec((tm, s.shape[1]), row)]
    in_specs += [_layer_spec(p) for p in params]
    args = [x, a, s] + [p.arr for p in params]
    out_specs = [pl.BlockSpec((tm, d), row)]
    out_shape = [jax.ShapeDtypeStruct((t, d), F32)]
    if router is not None:
        out_specs.append(pl.BlockSpec((tm, LANES), row))
        out_shape.append(jax.ShapeDtypeStruct((t, LANES), F32))
    return pl.pallas_call(
        functools.partial(_merge_body, with_router=router is not None),
        grid=(t // tm,),
        in_specs=in_specs, out_specs=out_specs, out_shape=out_shape,
        compiler_params=_cparams(("parallel",)),
        name="merge_ln1",
    )(*args)


def _ple(xb, p_ref, wpg_ref, wpp_ref):
    gate = _sigmoid(jnp.dot(xb, wpg_ref[...], preferred_element_type=F32))
    return gate * jnp.dot(p_ref[...].astype(BF16), wpp_ref[...], preferred_element_type=F32)


def _ffn_body(x_ref, p_ref, wg_ref, wu_ref, wd_ref, wpg_ref, wpp_ref, g_ref, b_ref, o_ref, acc):
    x1 = x_ref[...]
    xb = x1.astype(BF16)
    for c, (c0, cw) in enumerate(FFN_CHUNKS):
        gt = jnp.dot(xb, wg_ref[:, c0:c0 + cw], preferred_element_type=F32)
        up = jnp.dot(xb, wu_ref[:, c0:c0 + cw], preferred_element_type=F32)
        act = (gt * _sigmoid(gt) * up).astype(BF16)
        contrib = jnp.dot(act, wd_ref[c0:c0 + cw, :], preferred_element_type=F32)
        if c == 0:
            acc[...] = contrib
        else:
            acc[...] += contrib
    h = DEEPNORM_ALPHA * x1 + acc[...] + _ple(xb, p_ref, wpg_ref, wpp_ref)
    o_ref[...] = _layer_norm(h, g_ref[...], b_ref[...])


def _resident(shape, index_map):
    return pl.BlockSpec(shape, index_map, pipeline_mode=pl.Buffered(1))


def _ple_rows_spec(p, tm):
    return pl.BlockSpec((None, tm, p.arr.shape[2]), lambda i: (p.idx, i, 0))


def _dense_layer_body(x_ref, a_ref, s_ref, p_ref, wg_ref, wba_ref, wbs_ref, wo_ref, g1_ref, b1_ref,
                      fg_ref, fu_ref, fd_ref, wpg_ref, wpp_ref, g2_ref, b2_ref, w_next_ref,
                      o_ref, q_ref, kv_ref, u_ref, x1_buf, acc):
    _merge_body(x_ref, a_ref, s_ref, wg_ref, wba_ref, wbs_ref, wo_ref, g1_ref, b1_ref, x1_buf, with_router=False)
    _ffn_body(x1_buf, p_ref, fg_ref, fu_ref, fd_ref, wpg_ref, wpp_ref, g2_ref, b2_ref, o_ref, acc)
    _project(o_ref[...], w_next_ref, q_ref, kv_ref, u_ref)


def _dense_layer(x, a, s, p, wg, wba, wbs, wo, g1, b1, w_gu, w_down, li, wpg, wpp, g2, b2, w_next, *, tm):
    t, d = x.shape
    ff = w_down.shape[1]
    row = lambda i: (i, 0)
    merge_params = [wg, wba, wbs, wo, g1, b1]
    tail_params = [wpg, wpp, g2, b2, w_next]
    proj_specs, proj_shapes = _proj_outputs(t, tm, w_next.arr.shape[2])
    return pl.pallas_call(
        _dense_layer_body,
        grid=(t // tm,),
        in_specs=[pl.BlockSpec((tm, d), row), pl.BlockSpec((tm, a.shape[1]), row), pl.BlockSpec((tm, s.shape[1]), row),
                  _ple_rows_spec(p, tm)]
        + [_layer_spec(q, resident=True) for q in merge_params]
        + [_resident((None, d, ff), lambda i: (li, 0, 0)),
           _resident((None, d, ff), lambda i: (li, 0, 1)),
           _resident((None, ff, d), lambda i: (li, 0, 0))]
        + [_layer_spec(q, resident=True) for q in tail_params],
        out_specs=[pl.BlockSpec((tm, d), row)] + proj_specs,
        out_shape=[jax.ShapeDtypeStruct((t, d), F32)] + proj_shapes,
        scratch_shapes=[pltpu.VMEM((tm, d), F32), pltpu.VMEM((tm, d), F32)],
        compiler_params=_cparams(("parallel",)),
        name="dense_layer",
    )(x, a, s, p.arr, *[q.arr for q in merge_params], w_gu, w_gu, w_down, *[q.arr for q in tail_params])


def _moe_body(te_ref, nu_ref, src_ref, dst_ref, x_hbm, wgu_ref, wd_ref, y_hbm,
              xg, xb, acc, ybuf, gsem, ssem, *, tm, fw, spare_row):
    i = pl.program_id(0)
    nu = nu_ref[0]
    slot = i % 2
    other = 1 - slot
    ffe = wd_ref.shape[0]
    nc = ffe // fw
    per_chunk = tm // nc

    def gather(tile, r, buf):
        row = src_ref[tile * tm + r]
        pltpu.make_async_copy(x_hbm.at[pl.ds(row, 1), :], xg.at[buf, pl.ds(r, 1), :], gsem.at[buf]).start()

    def scatter(row, r, buf):
        pltpu.make_async_copy(ybuf.at[buf, pl.ds(r, 1), :], y_hbm.at[pl.ds(row, 1), :], ssem.at[buf]).start()

    def prev_tile_dst(r):
        return jnp.where(i > 0, dst_ref[jnp.maximum(i - 1, 0) * tm + r], spare_row + r)

    @pl.when(i == 0)
    def _():
        ybuf[...] = jnp.zeros_like(ybuf)

        def one(r, c):
            gather(0, r, 0)
            return c
        lax.fori_loop(0, tm, one, 0)

    @pl.when(i <= nu)
    def _():
        pltpu.make_async_copy(x_hbm.at[pl.ds(0, tm), :], xg.at[slot], gsem.at[slot]).wait()

    @pl.when(i < nu)
    def _():
        xb[...] = xg[slot].astype(BF16)
        x = xb[...]
        for c in range(nc):
            c0 = c * fw
            gt = jnp.dot(x, wgu_ref[:, c0:c0 + fw], preferred_element_type=F32)
            up = jnp.dot(x, wgu_ref[:, ffe + c0:ffe + c0 + fw], preferred_element_type=F32)
            act = (gt * _sigmoid(gt) * up).astype(BF16)
            part = jnp.dot(act, wd_ref[c0:c0 + fw, :], preferred_element_type=F32)
            if c == 0:
                acc[...] = part
            else:
                acc[...] += part
            r1 = tm if c == nc - 1 else (c + 1) * per_chunk
            for r in range(c * per_chunk, r1):
                gather(i + 1, r, other)
                scatter(prev_tile_dst(r), r, other)

    @pl.when(i == nu)
    def _():
        def one(r, c):
            scatter(dst_ref[(i - 1) * tm + r], r, other)
            return c
        lax.fori_loop(0, tm, one, 0)

    @pl.when((i >= 1) & (i <= nu + 1))
    def _():
        pltpu.make_async_copy(ybuf.at[slot], y_hbm.at[pl.ds(0, tm), :], ssem.at[slot]).wait()

    @pl.when(i < nu)
    def _():
        ybuf[slot] = _pack_bf16_pairs(acc[...])


def _moe_experts(x1, w_gu, w_down, tile_expert, n_used, src, dst, *, tm, fw, out_rows):
    t, d = x1.shape
    ffe = w_down.shape[1]
    nt = src.shape[0] // tm
    return pl.pallas_call(
        functools.partial(_moe_body, tm=tm, fw=fw, spare_row=out_rows - tm),
        grid_spec=pltpu.PrefetchScalarGridSpec(
            num_scalar_prefetch=4,
            grid=(nt,),
            in_specs=[pl.BlockSpec(memory_space=pl.ANY),
                      _resident((None, d, 2 * ffe), lambda i, te, nu, s, ds: (te[i], 0, 0)),
                      _resident((None, ffe, d), lambda i, te, nu, s, ds: (te[i], 0, 0))],
            out_specs=pl.BlockSpec(memory_space=pl.ANY),
            scratch_shapes=[pltpu.VMEM((2, tm, d), F32), pltpu.VMEM((tm, d), BF16), pltpu.VMEM((tm, d), F32),
                            pltpu.VMEM((2, tm, d // 2), jnp.uint32),
                            pltpu.SemaphoreType.DMA((2,)), pltpu.SemaphoreType.DMA((2,))]),
        out_shape=jax.ShapeDtypeStruct((out_rows, d // 2), jnp.uint32),
        compiler_params=_cparams(("arbitrary",)),
        name="moe_experts",
    )(tile_expert, n_used, src, dst, x1, w_gu, w_down)


def _moe_out_body(x_ref, route_ref, y0_ref, y1_ref, p_ref, wpg_ref, wpp_ref, g_ref, b_ref, *rest):
    o_ref = rest[0] if len(rest) == 1 else rest[1]
    x1 = x_ref[...]
    route = route_ref[...]
    ffn = route[:, 2:3] * _unpack_bf16_pairs(y0_ref[...]) + route[:, 3:4] * _unpack_bf16_pairs(y1_ref[...])
    h = DEEPNORM_ALPHA * x1 + ffn + _ple(x1.astype(BF16), p_ref, wpg_ref, wpp_ref)
    x2 = _layer_norm(h, g_ref[...], b_ref[...])
    o_ref[...] = x2
    if len(rest) > 1:
        _project(x2, rest[0], *rest[2:])


def _moe_out(x1, route, p, y, wpg, wpp, g, b, w_next=None, *, tm):
    t, d = x1.shape
    nt = t // tm
    row = lambda i: (i, 0)
    params = [wpg, wpp, g, b] + ([w_next] if w_next is not None else [])
    out_specs, out_shape = [pl.BlockSpec((tm, d), row)], [jax.ShapeDtypeStruct((t, d), F32)]
    if w_next is not None:
        specs, shapes = _proj_outputs(t, tm, w_next.arr.shape[2])
        out_specs, out_shape = out_specs + specs, out_shape + shapes
    out = pl.pallas_call(
        _moe_out_body,
        grid=(nt,),
        in_specs=[pl.BlockSpec((tm, d), row), pl.BlockSpec((tm, LANES), row),
                  pl.BlockSpec((tm, d // 2), row), pl.BlockSpec((tm, d // 2), lambda i: (nt + i, 0)),
                  _ple_rows_spec(p, tm)] + [_layer_spec(q) for q in params],
        out_specs=out_specs, out_shape=out_shape,
        compiler_params=_cparams(("parallel",)),
        name="moe_combine_ln2",
    )(x1, route, y, y, p.arr, *[q.arr for q in params])
    return out[0] if w_next is None else out


def _slot_table_body(pos_ref, init_hbm, dst_ref, sem, *, t, groups):
    @pl.when(pl.program_id(0) == 0)
    def _():
        copy = pltpu.make_async_copy(init_hbm, dst_ref, sem)
        copy.start()
        copy.wait()

    batch = SLOT_TABLE_BATCH
    base = pl.program_id(0) * (t // groups)

    def put(kb, c):
        tok0 = base + kb * batch
        ps = [pos_ref[2 * tok0 + j] for j in range(2 * batch)]
        for j, p in enumerate(ps):
            dst_ref[p] = (j % 2) * t + tok0 + j // 2
        return c
    lax.fori_loop(0, t // groups // batch, put, 0)


def _slot_table(pos, *, t, tm, r):
    groups = SLOT_TABLE_GROUPS
    assert t % (groups * SLOT_TABLE_BATCH) == 0
    init = 2 * t + jnp.arange(r, dtype=jnp.int32) % tm
    smem = pl.BlockSpec(memory_space=pltpu.SMEM)
    return pl.pallas_call(
        functools.partial(_slot_table_body, t=t, groups=groups),
        grid=(groups,),
        in_specs=[smem, pl.BlockSpec(memory_space=pl.ANY)], out_specs=smem,
        out_shape=jax.ShapeDtypeStruct((r,), jnp.int32),
        scratch_shapes=[pltpu.SemaphoreType.DMA],
        compiler_params=_cparams(("arbitrary",)),
        name="moe_slot_table",
    )(pos, init)


def _routing_tables(route, *, tm, n_experts):
    t = route.shape[0]
    eids = route[:, :2].astype(jnp.int32).reshape(-1)
    onehot = (eids[:, None] == jnp.arange(n_experts, dtype=jnp.int32)[None, :]).astype(jnp.int32)
    csum = jnp.cumsum(onehot, axis=0)
    rank = jnp.sum((csum - onehot) * onehot, axis=1)
    cnt = csum[-1]
    ntile = (cnt + tm - 1) // tm
    tile_end = jnp.cumsum(ntile)
    row_off = (tile_end - ntile) * tm
    pos = jnp.sum(onehot * row_off[None, :], axis=1) + rank
    r = 2 * t + (n_experts + 2) * tm
    dst = _slot_table(pos.astype(jnp.int32), t=t, tm=tm, r=r)
    src = jnp.where(dst < 2 * t, dst % t, 0)
    n_used = tile_end[-1:]
    tiles = jnp.arange(r // tm, dtype=jnp.int32)
    te = jnp.sum((tiles[:, None] >= tile_end[None, :]).astype(jnp.int32), axis=1)
    last = jnp.sum((n_used - 1 >= tile_end).astype(jnp.int32))
    te = jnp.where(tiles < n_used, te, last).astype(jnp.int32)
    out_rows = 2 * t + tm
    return te, n_used.astype(jnp.int32), src.astype(jnp.int32), dst.astype(jnp.int32), out_rows


def kernel(x, p, rel_bias, w_in, attn_sinks, ssm_lambda_re, ssm_lambda_im, ssm_log_dt, ssm_b_re, ssm_b_im,
           ssm_c_re, ssm_c_im, ssm_d, w_glu, b_glu, w_branch_attn, w_branch_ssm, w_out, ln1_g, ln1_b,
           ffn_w_gate_up, ffn_w_down, moe_router, moe_w_gate_up, moe_w_down, ple_w_proj, ple_w_gate,
           ln2_g, ln2_b):
    bsz, seq, d = x.shape
    t = bsz * seq
    depth = w_in.shape[0]

    xt = x.reshape(t, d)
    pt = p.reshape(depth, t, p.shape[-1])

    bias = _attn_bias_table(rel_bias)
    sinks = attn_sinks * LOG2_E
    ffn_gu = ffn_w_gate_up.astype(BF16)
    ffn_dn = ffn_w_down.astype(BF16)

    aw, kw = ATTN_WIDTH, KV_WIDTH
    u0 = aw + 2 * kw
    w_proj = jnp.concatenate([_q_head_tiles(w_in[:, :, :aw]), w_in[:, :, aw:u0 + SSM_WIDTH]], axis=2).astype(BF16)
    w_gates = w_in[:, :, u0 + SSM_WIDTH:].astype(BF16)
    wba = w_branch_attn.reshape(depth, N_Q_HEADS, HEAD_DIM, d)[:, np.array(_attn_out_head_order())]
    wba = wba.reshape(depth, aw, d).astype(BF16)
    wbs, wo = w_branch_ssm.astype(BF16), w_out.astype(BF16)
    wpg, wpp = ple_w_gate.astype(BF16), ple_w_proj.astype(BF16)
    g1, b1 = ln1_g.reshape(depth, 1, d), ln1_b.reshape(depth, 1, d)
    g2, b2 = ln2_g.reshape(depth, 1, d), ln2_b.reshape(depth, 1, d)

    lbr, lbi, bbr, bbi = _ssm_params(ssm_lambda_re, ssm_lambda_im, ssm_log_dt, ssm_b_re, ssm_b_im)
    bre = _block_diag_halves(bbr).astype(BF16)
    bim = _block_diag_halves(bbi).astype(BF16)
    cre = _block_diag_halves(jnp.transpose(ssm_c_re, (0, 1, 3, 2))).astype(BF16)
    cim = _block_diag_halves(jnp.transpose(ssm_c_im, (0, 1, 3, 2))).astype(BF16)
    ssm_skip = ssm_d.reshape(depth, 1, -1)
    wglu, bglu = w_glu.astype(BF16), b_glu.reshape(depth, 1, -1)

    rt = jnp.pad(moe_router, ((0, 0), (0, 0), (0, LANES - N_EXPERTS)))
    rhi = rt.astype(BF16)
    rcat = jnp.concatenate([rhi, (rt - rhi.astype(F32)).astype(BF16)], axis=2)

    q, kv, u = _inproj(xt, _Layer(w_proj, 0), tm=TM_PROJ)
    for i in range(depth):
        at = lambda arr, idx=i: _Layer(arr, idx)
        w_next = _Layer(w_proj, i + 1) if i + 1 < depth else None
        a_out = _attention(q, kv, at(sinks), bias, seq=seq, batch=bsz)
        moe_cast = _Layer(moe_w_gate_up if i % 2 == 0 else moe_w_down, i // 2)
        s_out, moe_w = _ssm(u, at(bre), at(bim), at(cre), at(cim), at(lbr), at(lbi), at(ssm_skip), at(wglu),
                            at(bglu), moe_cast, batch=bsz, steps=SSM_STEPS, lane_chunk=SSM_LANE_CHUNK)
        if i % 2 == 0:
            moe_gu = moe_w
        merge_w = (at(w_gates), at(wba), at(wbs), at(wo), at(g1), at(b1))
        if i % 2 == 0:
            xt, q, kv, u = _dense_layer(xt, a_out, s_out, at(pt), *merge_w, ffn_gu, ffn_dn, i // 2,
                                        at(wpg), at(wpp), at(g2), at(b2), w_next, tm=TM_FFN)
        else:
            router = (_Layer(rcat, i // 2),)
            x1, route = _merge(xt, a_out, s_out, *merge_w, router, tm=TM_MERGE)
            te, n_used, src, dst, out_rows = _routing_tables(route, tm=TM_MOE, n_experts=N_EXPERTS)
            y = _moe_experts(x1, moe_gu, moe_w, te, n_used, src, dst, tm=TM_MOE, fw=FW_MOE, out_rows=out_rows)
            out = _moe_out(x1, route, at(pt), y, at(wpg), at(wpp), at(g2), at(b2), w_next, tm=TM_FFN)
            xt, q, kv, u = out if w_next is not None else (out, None, None, None)

    return xt.reshape(bsz, seq, d)
```

```python
import functools
import math
from typing import NamedTuple

import jax
import jax.numpy as jnp
import numpy as np
from jax import lax
from jax.experimental import pallas as pl
from jax.experimental.pallas import tpu as pltpu

F32 = jnp.float32
BF16 = jnp.bfloat16

DEPTH = 4
HEAD_DIM = 64
N_Q_HEADS = 8
N_KV_HEADS = 2
ATTN_WIDTH = N_Q_HEADS * HEAD_DIM
KV_WIDTH = N_KV_HEADS * HEAD_DIM
ATTN_BLOCK = 128
N_BUCKETS = 32
MAX_DISTANCE = 128
SSM_WIDTH = 512
N_EXPERTS = 8
DEEPNORM_ALPHA = (2 * DEPTH) ** 0.25
LN_EPS = 1e-5
NEG_INF = -1e30
LOG2_E = math.log2(math.e)

LANES = 128
BF16_TILE_ROWS = 16
VMEM_LIMIT_BYTES = 56 * 1024 * 1024

TM_PROJ = 1024
ATTN_BLOCKS_PER_STEP = 16
TM_MERGE = 1024
MERGE_GROUP_ROWS = 256
TM_FFN = 512
TM_MOE = 512
FW_MOE = 1792
SLOT_TABLE_GROUPS = 16
SLOT_TABLE_BATCH = 8
SSM_STEPS = 32
SSM_PERM_STEPS = BF16_TILE_ROWS
SSM_LANE_CHUNK = 512
FFN_CHUNKS = ((0, 768), (768, 768), (1536, 768), (2304, 512))


def _cparams(sem):
    return pltpu.CompilerParams(dimension_semantics=sem, vmem_limit_bytes=VMEM_LIMIT_BYTES)


class _Layer(NamedTuple):
    arr: jax.Array
    idx: int


def _layer_spec(p, resident=False):
    tail = (0,) * (p.arr.ndim - 1)
    mode = {"pipeline_mode": pl.Buffered(1)} if resident else {}
    return pl.BlockSpec((None,) + p.arr.shape[1:], lambda *_: (p.idx,) + tail, **mode)


def _layer_norm(h, g, b):
    mu = jnp.mean(h, axis=-1, keepdims=True)
    c = h - mu
    var = jnp.mean(c * c, axis=-1, keepdims=True)
    return c * lax.rsqrt(var + LN_EPS) * g + b


def _sigmoid(v):
    return 1.0 / (1.0 + jnp.exp(-v))


def _pack_bf16_pairs(v):
    c = v.shape[1] // 2
    lo = lax.bitcast_convert_type(v[:, :c].astype(BF16).astype(F32), jnp.uint32)
    hi = lax.bitcast_convert_type(v[:, c:].astype(BF16).astype(F32), jnp.uint32)
    return (lo >> 16) | (hi & jnp.uint32(0xFFFF0000))


def _unpack_bf16_pairs(w):
    lo = lax.bitcast_convert_type(w << 16, F32)
    hi = lax.bitcast_convert_type(w & jnp.uint32(0xFFFF0000), F32)
    return jnp.concatenate([lo, hi], axis=1)


def _gelu_tanh(v):
    return 0.5 * v * (1.0 + jnp.tanh(math.sqrt(2.0 / math.pi) * (v + 0.044715 * (v * v * v))))


def _inproj_body(x_ref, w_ref, q_ref, kv_ref, u_ref):
    _project(x_ref[...], w_ref, q_ref, kv_ref, u_ref)


def _project(x, w_ref, q_ref, kv_ref, u_ref):
    z = jnp.dot(x.astype(BF16), w_ref[...], preferred_element_type=F32)
    aw = q_ref.shape[1]
    kw = kv_ref.shape[1]
    q_ref[...] = (z[:, :aw] * (LOG2_E * HEAD_DIM ** -0.5)).astype(BF16)
    kv_ref[...] = z[:, aw:aw + kw].astype(BF16)
    u_ref[...] = z[:, aw + kw:].astype(BF16)


def _proj_outputs(t, tm, n):
    aw, kw = N_Q_HEADS * LANES, 2 * KV_WIDTH
    widths = (aw, kw, n - aw - kw)
    specs = [pl.BlockSpec((tm, w), lambda i: (i, 0)) for w in widths]
    shapes = [jax.ShapeDtypeStruct((t, w), BF16) for w in widths]
    return specs, shapes


def _inproj(x, w, *, tm):
    t, d = x.shape
    n = w.arr.shape[2]
    aw, kw = N_Q_HEADS * LANES, 2 * KV_WIDTH
    uw = n - aw - kw
    return pl.pallas_call(
        _inproj_body,
        grid=(t // tm,),
        in_specs=[pl.BlockSpec((tm, d), lambda i: (i, 0)), _layer_spec(w)],
        out_specs=[pl.BlockSpec((tm, aw), lambda i: (i, 0)),
                   pl.BlockSpec((tm, kw), lambda i: (i, 0)),
                   pl.BlockSpec((tm, uw), lambda i: (i, 0))],
        out_shape=[jax.ShapeDtypeStruct((t, aw), BF16),
                   jax.ShapeDtypeStruct((t, kw), BF16),
                   jax.ShapeDtypeStruct((t, uw), BF16)],
        compiler_params=_cparams(("parallel",)),
        name="inproj",
    )(x, w.arr)


def _attn_body(sink_ref, q_ref, kvc_ref, kvp_ref, bias0_ref, bias_ref, o_ref, *, blk, nbs, layer):
    group = N_Q_HEADS // N_KV_HEADS
    ks = slice(0, LANES)
    vs = slice(LANES, 2 * LANES)
    low = lax.broadcasted_iota(jnp.int32, (blk, LANES), 1) < HEAD_DIM
    for j in range(nbs):
        rows = slice(j * blk, (j + 1) * blk)
        b_ref = bias0_ref if j == 0 else bias_ref
        if j == 0:
            k = jnp.concatenate([kvp_ref[:, ks], kvc_ref[0:blk, ks]], axis=0)
            v = jnp.concatenate([kvp_ref[:, vs], kvc_ref[0:blk, vs]], axis=0)
        else:
            k = kvc_ref[(j - 1) * blk:(j + 1) * blk, ks]
            v = kvc_ref[(j - 1) * blk:(j + 1) * blk, vs]
        for tile in range(group):
            halves = []
            for g in range(N_KV_HEADS):
                h = tile + group * g
                qt = q_ref[rows, h * LANES:(h + 1) * LANES]
                s = lax.dot_general(qt, k, (((1,), (1,)), ((), ())), preferred_element_type=F32)
                s = s + b_ref[h]
                sink = sink_ref[layer, h]
                m = jnp.maximum(jnp.max(s, axis=-1, keepdims=True), sink)
                e = jnp.exp2(s - m)
                denom = jnp.sum(e, axis=-1, keepdims=True) + jnp.exp2(sink - m)
                halves.append(jnp.dot(e.astype(BF16), v, preferred_element_type=F32) / denom)
            o_ref[rows, tile * LANES:(tile + 1) * LANES] = jnp.where(low, halves[0], halves[1]).astype(BF16)


def _attention(q, kv, sinks, bias, *, seq, batch):
    assert KV_WIDTH == LANES
    t = q.shape[0]
    blk = ATTN_BLOCK
    nbs = ATTN_BLOCKS_PER_STEP
    ns = seq // (blk * nbs)
    qw, kw, aw = q.shape[1], kv.shape[1], ATTN_WIDTH
    table = (None, N_Q_HEADS, blk, 2 * blk)
    return pl.pallas_call(
        functools.partial(_attn_body, blk=blk, nbs=nbs, layer=sinks.idx),
        grid=(batch, ns),
        in_specs=[pl.BlockSpec(memory_space=pltpu.SMEM),
                  pl.BlockSpec((blk * nbs, qw), lambda b, n: (b * ns + n, 0)),
                  pl.BlockSpec((blk * nbs, kw), lambda b, n: (b * ns + n, 0)),
                  pl.BlockSpec((blk, kw), lambda b, n: ((b * ns + n) * nbs - jnp.minimum(n, 1), 0)),
                  pl.BlockSpec(table, lambda b, n: (jnp.minimum(n, 1), 0, 0, 0)),
                  pl.BlockSpec(table, lambda b, n: (1, 0, 0, 0))],
        out_specs=pl.BlockSpec((blk * nbs, aw), lambda b, n: (b * ns + n, 0)),
        out_shape=jax.ShapeDtypeStruct((t, aw), BF16),
        compiler_params=_cparams(("parallel", "parallel")),
        name="swa_attention",
    )(sinks.arr, q, kv, kv, bias, bias)


def _q_head_tiles(w):
    depth, d, _ = w.shape
    group = N_Q_HEADS // N_KV_HEADS
    wh = w.reshape(depth, d, N_Q_HEADS, 1, HEAD_DIM)
    half = jnp.arange(N_KV_HEADS)[None, None, None, :, None]
    kv_head = (jnp.arange(N_Q_HEADS) // group)[None, None, :, None, None]
    tiles = jnp.where(half == kv_head, wh, 0.0)
    return tiles.reshape(depth, d, N_Q_HEADS * LANES)


def _attn_out_head_order():
    group = N_Q_HEADS // N_KV_HEADS
    return [j + group * half for j in range(group) for half in range(N_KV_HEADS)]


def _attn_bias_table(rel_bias):
    blk = ATTN_BLOCK
    d = np.arange(blk, dtype=np.int32)
    max_exact = N_BUCKETS // 2
    d_f = np.maximum(d, 1).astype(np.float32)
    large = max_exact + (np.log(d_f / np.float32(max_exact)) / np.float32(math.log(MAX_DISTANCE / max_exact))
                         * np.float32(N_BUCKETS - max_exact)).astype(np.int32)
    large = np.minimum(large, N_BUCKETS - 1)
    bucket = np.where(d < max_exact, d, large)
    vals = jnp.transpose(rel_bias.astype(F32)[bucket], (1, 0))
    h = vals.shape[0]
    neg = jnp.full((h, blk), NEG_INF, F32)
    strip = jnp.concatenate([neg, vals[:, ::-1], neg], axis=1)
    rows = [strip[:, blk - 1 - i:blk - 1 - i + 2 * blk] for i in range(blk)]
    table = jnp.stack(rows, axis=1) * LOG2_E
    first = jnp.where(np.arange(2 * blk)[None, None, :] < blk, NEG_INF, table)
    return jnp.stack([first, table], axis=0)


def _ssm_param_body(lre_ref, lim_ref, ldt_ref, bre_ref, bim_ref, lbr_ref, lbi_ref, bbr_ref, bbi_ref):
    lre, lim = lre_ref[...], lim_ref[...]
    dt = jnp.exp(ldt_ref[...])
    mag = jnp.exp(lre * dt)
    lbr = mag * jnp.cos(lim * dt)
    lbi = mag * jnp.sin(lim * dt)
    lbr_ref[...] = lbr
    lbi_ref[...] = lbi
    nr, ni = lbr - 1.0, lbi
    inv = 1.0 / (lre * lre + lim * lim)
    cr = (nr * lre + ni * lim) * inv
    ci = (ni * lre - nr * lim) * inv
    br, bi = bre_ref[...], bim_ref[...]
    bbr_ref[...] = cr * br - ci * bi
    bbi_ref[...] = cr * bi + ci * br


def _ssm_params(lam_re, lam_im, log_dt, b_re, b_im):
    depth, g, p = lam_re.shape
    h = b_re.shape[3]
    n = depth * g
    ldt = jnp.broadcast_to(log_dt.reshape(n, 1, 1), (n, 1, p))
    brt = jnp.transpose(b_re, (0, 1, 3, 2)).reshape(n, h, p)
    bit = jnp.transpose(b_im, (0, 1, 3, 2)).reshape(n, h, p)
    sd = jax.ShapeDtypeStruct
    lbr, lbi, bbr, bbi = pl.pallas_call(
        _ssm_param_body,
        out_shape=[sd((n, 1, p), F32), sd((n, 1, p), F32), sd((n, h, p), F32), sd((n, h, p), F32)],
        name="ssm_params",
    )(lam_re.reshape(n, 1, p), lam_im.reshape(n, 1, p), ldt, brt, bit)
    return (lbr.reshape(depth, 1, g * p), lbi.reshape(depth, 1, g * p),
            bbr.reshape(depth, g, h, p), bbi.reshape(depth, g, h, p))


def _block_diag_halves(w):
    depth, g, a, b = w.shape
    hg = g // 2
    wh = w.reshape(depth, 2, hg, a, b)
    eye = jnp.eye(hg, dtype=bool)[None, None, :, None, :, None]
    full = jnp.where(eye, wh[:, :, :, :, None, :], 0.0)
    return full.reshape(depth, 2, hg * a, hg * b)


def _ssm_body(u_ref, perm_ref, permt_ref, bre_ref, bim_ref, cre_ref, cim_ref, lr_ref, li_ref, d_ref,
              wglu_ref, bglu_ref, *rest, steps, batch, lane_chunk, with_cast):
    if with_cast:
        wsrc_ref, o_ref, wdst_ref, xr0, xi0, ub0, xr1, xi1, ub1, sr, si = rest
    else:
        o_ref, xr0, xi0, ub0, xr1, xi1, ub1, sr, si = rest
    s = pl.program_id(0)

    @pl.when(s == 0)
    def _():
        for ref in (xr0, xi0, ub0, xr1, xi1, ub1, sr, si):
            ref[...] = jnp.zeros_like(ref)

    sub = SSM_PERM_STEPS
    rows_g = sub * batch

    def stage(fill, scan):
        xr_f, xi_f, ub_f = fill
        xr, xi, ub = scan
        parts = []
        for g in range(steps // sub):
            bm = jnp.concatenate([u_ref[b, g * sub:(g + 1) * sub, :] for b in range(batch)], axis=0)
            parts.append(jnp.dot(perm_ref[...], bm, preferred_element_type=F32).astype(BF16))
        u_new = jnp.concatenate(parts, axis=0)
        ub_f[...] = u_new
        hw = u_new.shape[1] // 2
        hs = xr.shape[1] // 2
        for hf in range(2):
            uh = u_new[:, hf * hw:(hf + 1) * hw]
            xr_f[:, hf * hs:(hf + 1) * hs] = jnp.dot(uh, bre_ref[hf], preferred_element_type=F32)
            xi_f[:, hf * hs:(hf + 1) * hs] = jnp.dot(uh, bim_ref[hf], preferred_element_type=F32)
        for c in range(xr.shape[1] // lane_chunk):
            cs = slice(c * lane_chunk, (c + 1) * lane_chunk)
            ar = lr_ref[:, cs]
            ai = li_ref[:, cs]
            pr, pi = sr[:, cs], si[:, cs]
            for t in range(steps):
                rows = slice(t * batch, (t + 1) * batch)
                nr = ar * pr - ai * pi + xr[rows, cs]
                ni = ar * pi + ai * pr + xi[rows, cs]
                xr[rows, cs] = nr
                xi[rows, cs] = ni
                pr, pi = nr, ni
            sr[:, cs] = pr
            si[:, cs] = pi
        ys = []
        for hf in range(2):
            xrb = xr[:, hf * hs:(hf + 1) * hs].astype(BF16)
            xib = xi[:, hf * hs:(hf + 1) * hs].astype(BF16)
            ys.append(jnp.dot(xrb, cre_ref[hf], preferred_element_type=F32)
                      - jnp.dot(xib, cim_ref[hf], preferred_element_type=F32))
        y = jnp.concatenate(ys, axis=1) + d_ref[...] * ub[...].astype(F32)
        y = _gelu_tanh(y)
        z = jnp.dot(y.astype(BF16), wglu_ref[...], preferred_element_type=F32) + bglu_ref[...]
        out = (y * _sigmoid(z)).astype(BF16)
        for g in range(steps // sub):
            bm = jnp.dot(permt_ref[...], out[g * rows_g:(g + 1) * rows_g], preferred_element_type=F32).astype(BF16)
            for b in range(batch):
                o_ref[b, g * sub:(g + 1) * sub, :] = bm[b * sub:(b + 1) * sub]
        if with_cast:
            wdst_ref[...] = wsrc_ref[...].astype(BF16)

    @pl.when(s % 2 == 0)
    def _():
        stage((xr0, xi0, ub0), (xr1, xi1, ub1))

    @pl.when(s % 2 == 1)
    def _():
        stage((xr1, xi1, ub1), (xr0, xi0, ub0))


def _time_major_permutation(batch, sub):
    n = batch * sub
    r = np.arange(n)
    p = np.zeros((n, n), np.float32)
    p[r, (r % batch) * sub + r // batch] = 1.0
    return p


def _ssm(u, bre, bim, cre, cim, lbr, lbi, d, wglu, bglu, cast=None, *, batch, steps, lane_chunk):
    t, w = u.shape
    seq = t // batch
    n = seq // steps
    ns = lbr.arr.shape[2]
    rows = steps * batch
    perm = _time_major_permutation(batch, SSM_PERM_STEPS)
    const2 = lambda i: (0, 0)
    params = (bre, bim, cre, cim, lbr, lbi, d, wglu, bglu)
    buf = [pltpu.VMEM((rows, ns), F32), pltpu.VMEM((rows, ns), F32), pltpu.VMEM((rows, w), BF16)]
    in_specs = [pl.BlockSpec((batch, steps, w), lambda i: (0, jnp.minimum(i, n - 1), 0)),
                pl.BlockSpec(perm.shape, const2), pl.BlockSpec(perm.shape, const2)]
    in_specs += [_layer_spec(p) for p in params]
    args = [u.reshape(batch, seq, w), jnp.asarray(perm, BF16), jnp.asarray(perm.T, BF16)] + [p.arr for p in params]
    out_specs = [pl.BlockSpec((batch, steps, w), lambda i: (0, jnp.maximum(i - 1, 0), 0))]
    out_shape = [jax.ShapeDtypeStruct((batch, seq, w), BF16)]
    if cast is not None:
        wsrc = cast
        _, ne, r, c = wsrc.arr.shape
        per_e = (n + 1) // ne
        rb = r // per_e
        nblk = ne * per_e
        assert per_e >= 1 and r % per_e == 0 and rb % BF16_TILE_ROWS == 0
        blk = lambda i: jnp.minimum(i, nblk - 1)
        in_specs.append(pl.BlockSpec((None, None, rb, c), lambda i: (wsrc.idx, blk(i) // per_e, blk(i) % per_e, 0)))
        args.append(wsrc.arr)
        out_specs.append(pl.BlockSpec((None, rb, c), lambda i: (blk(i) // per_e, blk(i) % per_e, 0)))
        out_shape.append(jax.ShapeDtypeStruct((ne, r, c), BF16))
    out = pl.pallas_call(
        functools.partial(_ssm_body, steps=steps, batch=batch, lane_chunk=lane_chunk, with_cast=cast is not None),
        grid=(n + 1,),
        in_specs=in_specs, out_specs=out_specs, out_shape=out_shape,
        scratch_shapes=buf + buf + [pltpu.VMEM((batch, ns), F32), pltpu.VMEM((batch, ns), F32)],
        compiler_params=_cparams(("arbitrary",)),
        name="s5_scan",
    )(*args)
    if cast is None:
        return out[0].reshape(t, w)
    return out[0].reshape(t, w), out[1]


def _split_bf16(v):
    hi = v.astype(BF16)
    lo = (v - hi.astype(F32)).astype(BF16)
    return hi, lo


def _merge_body(x_ref, a_ref, s_ref, wg_ref, wba_ref, wbs_ref, wo_ref, g_ref, b_ref, *rest, with_router):
    if with_router:
        rcat_ref, x1_ref, route_ref = rest
    else:
        (x1_ref,) = rest
    tm, d = x_ref.shape
    rg = MERGE_GROUP_ROWS
    for grp in range(tm // rg):
        rows = slice(grp * rg, (grp + 1) * rg)
        x = x_ref[rows, :]
        gates = jnp.dot(x.astype(BF16), wg_ref[...], preferred_element_type=F32)
        pa = jnp.dot(a_ref[rows, :], wba_ref[...], preferred_element_type=F32)
        ps = jnp.dot(s_ref[rows, :], wbs_ref[...], preferred_element_type=F32)
        merged = _sigmoid(gates[:, :d]) * pa + _sigmoid(gates[:, d:]) * ps
        y = jnp.dot(merged.astype(BF16), wo_ref[...], preferred_element_type=F32)
        x1 = _layer_norm(DEEPNORM_ALPHA * x + y, g_ref[...], b_ref[...])
        x1_ref[rows, :] = x1
        if with_router:
            hi, lo = _split_bf16(x1)
            both = jnp.dot(hi, rcat_ref[...], preferred_element_type=F32)
            logits = (both[:, :LANES] + jnp.dot(lo, rcat_ref[:, :LANES], preferred_element_type=F32)
                      + both[:, LANES:])
            lane = lax.broadcasted_iota(jnp.int32, logits.shape, 1).astype(F32)
            big = float(LANES)
            l1 = jnp.where(lane < N_EXPERTS, logits, -jnp.inf)
            m1 = jnp.max(l1, axis=-1, keepdims=True)
            i1 = jnp.min(jnp.where(l1 == m1, lane, big), axis=-1, keepdims=True)
            l2 = jnp.where(lane == i1, -jnp.inf, l1)
            m2 = jnp.max(l2, axis=-1, keepdims=True)
            i2 = jnp.min(jnp.where(l2 == m2, lane, big), axis=-1, keepdims=True)
            ed = jnp.exp(m2 - m1)
            g1 = 1.0 / (1.0 + ed)
            g2 = ed / (1.0 + ed)
            route = jnp.where(lane == 0.0, i1,
                              jnp.where(lane == 1.0, i2,
                                        jnp.where(lane == 2.0, g1, jnp.where(lane == 3.0, g2, 0.0))))
            route_ref[rows, :] = route


def _merge(x, a, s, wg, wba, wbs, wo, g, b, router=None, *, tm):
    t, d = x.shape
    row = lambda i: (i, 0)
    params = [wg, wba, wbs, wo, g, b] + list(router or ())
    in_specs = [pl.BlockSpec((tm, d), row), pl.BlockSpec((tm, a.shape[1]), row), pl.BlockSpec((tm, s.shape[1]), row)]
    in_specs += [_layer_spec(p) for p in params]
    args = [x, a, s] + [p.arr for p in params]
    out_specs = [pl.BlockSpec((tm, d), row)]
    out_shape = [jax.ShapeDtypeStruct((t, d), F32)]
    if router is not None:
        out_specs.append(pl.BlockSpec((tm, LANES), row))
        out_shape.append(jax.ShapeDtypeStruct((t, LANES), F32))
    return pl.pallas_call(
        functools.partial(_merge_body, with_router=router is not None),
        grid=(t // tm,),
        in_specs=in_specs, out_specs=out_specs, out_shape=out_shape,
        compiler_params=_cparams(("parallel",)),
        name="merge_ln1",
    )(*args)


def _ple(xb, p_ref, wpg_ref, wpp_ref):
    gate = _sigmoid(jnp.dot(xb, wpg_ref[...], preferred_element_type=F32))
    return gate * jnp.dot(p_ref[...].astype(BF16), wpp_ref[...], preferred_element_type=F32)


def _ffn_body(x_ref, p_ref, wg_ref, wu_ref, wd_ref, wpg_ref, wpp_ref, g_ref, b_ref, o_ref, acc):
    x1 = x_ref[...]
    xb = x1.astype(BF16)
    for c, (c0, cw) in enumerate(FFN_CHUNKS):
        gt = jnp.dot(xb, wg_ref[:, c0:c0 + cw], preferred_element_type=F32)
        up = jnp.dot(xb, wu_ref[:, c0:c0 + cw], preferred_element_type=F32)
        act = (gt * _sigmoid(gt) * up).astype(BF16)
        contrib = jnp.dot(act, wd_ref[c0:c0 + cw, :], preferred_element_type=F32)
        if c == 0:
            acc[...] = contrib
        else:
            acc[...] += contrib
    h = DEEPNORM_ALPHA * x1 + acc[...] + _ple(xb, p_ref, wpg_ref, wpp_ref)
    o_ref[...] = _layer_norm(h, g_ref[...], b_ref[...])


def _resident(shape, index_map):
    return pl.BlockSpec(shape, index_map, pipeline_mode=pl.Buffered(1))


def _ple_rows_spec(p, tm):
    return pl.BlockSpec((None, tm, p.arr.shape[2]), lambda i: (p.idx, i, 0))


def _dense_layer_body(x_ref, a_ref, s_ref, p_ref, wg_ref, wba_ref, wbs_ref, wo_ref, g1_ref, b1_ref,
                      fg_ref, fu_ref, fd_ref, wpg_ref, wpp_ref, g2_ref, b2_ref, w_next_ref,
                      o_ref, q_ref, kv_ref, u_ref, x1_buf, acc):
    _merge_body(x_ref, a_ref, s_ref, wg_ref, wba_ref, wbs_ref, wo_ref, g1_ref, b1_ref, x1_buf, with_router=False)
    _ffn_body(x1_buf, p_ref, fg_ref, fu_ref, fd_ref, wpg_ref, wpp_ref, g2_ref, b2_ref, o_ref, acc)
    _project(o_ref[...], w_next_ref, q_ref, kv_ref, u_ref)


def _dense_layer(x, a, s, p, wg, wba, wbs, wo, g1, b1, w_gu, w_down, li, wpg, wpp, g2, b2, w_next, *, tm):
    t, d = x.shape
    ff = w_down.shape[1]
    row = lambda i: (i, 0)
    merge_params = [wg, wba, wbs, wo, g1, b1]
    tail_params = [wpg, wpp, g2, b2, w_next]
    proj_specs, proj_shapes = _proj_outputs(t, tm, w_next.arr.shape[2])
    return pl.pallas_call(
        _dense_layer_body,
        grid=(t // tm,),
        in_specs=[pl.BlockSpec((tm, d), row), pl.BlockSpec((tm, a.shape[1]), row), pl.BlockSpec((tm, s.shape[1]), row),
                  _ple_rows_spec(p, tm)]
        + [_layer_spec(q, resident=True) for q in merge_params]
        + [_resident((None, d, ff), lambda i: (li, 0, 0)),
           _resident((None, d, ff), lambda i: (li, 0, 1)),
           _resident((None, ff, d), lambda i: (li, 0, 0))]
        + [_layer_spec(q, resident=True) for q in tail_params],
        out_specs=[pl.BlockSpec((tm, d), row)] + proj_specs,
        out_shape=[jax.ShapeDtypeStruct((t, d), F32)] + proj_shapes,
        scratch_shapes=[pltpu.VMEM((tm, d), F32), pltpu.VMEM((tm, d), F32)],
        compiler_params=_cparams(("parallel",)),
        name="dense_layer",
    )(x, a, s, p.arr, *[q.arr for q in merge_params], w_gu, w_gu, w_down, *[q.arr for q in tail_params])


def _moe_body(te_ref, nu_ref, src_ref, dst_ref, x_hbm, wgu_ref, wd_ref, y_hbm,
              xg, xb, acc, ybuf, gsem, ssem, *, tm, fw, spare_row):
    i = pl.program_id(0)
    nu = nu_ref[0]
    slot = i % 2
    other = 1 - slot
    ffe = wd_ref.shape[0]
    nc = ffe // fw
    per_chunk = tm // nc

    def gather(tile, r, buf):
        row = src_ref[tile * tm + r]
        pltpu.make_async_copy(x_hbm.at[pl.ds(row, 1), :], xg.at[buf, pl.ds(r, 1), :], gsem.at[buf]).start()

    def scatter(row, r, buf):
        pltpu.make_async_copy(ybuf.at[buf, pl.ds(r, 1), :], y_hbm.at[pl.ds(row, 1), :], ssem.at[buf]).start()

    def prev_tile_dst(r):
        return jnp.where(i > 0, dst_ref[jnp.maximum(i - 1, 0) * tm + r], spare_row + r)

    @pl.when(i == 0)
    def _():
        ybuf[...] = jnp.zeros_like(ybuf)

        def one(r, c):
            gather(0, r, 0)
            return c
        lax.fori_loop(0, tm, one, 0)

    @pl.when(i <= nu)
    def _():
        pltpu.make_async_copy(x_hbm.at[pl.ds(0, tm), :], xg.at[slot], gsem.at[slot]).wait()

    @pl.when(i < nu)
    def _():
        xb[...] = xg[slot].astype(BF16)
        x = xb[...]
        for c in range(nc):
            c0 = c * fw
            gt = jnp.dot(x, wgu_ref[:, c0:c0 + fw], preferred_element_type=F32)
            up = jnp.dot(x, wgu_ref[:, ffe + c0:ffe + c0 + fw], preferred_element_type=F32)
            act = (gt * _sigmoid(gt) * up).astype(BF16)
            part = jnp.dot(act, wd_ref[c0:c0 + fw, :], preferred_element_type=F32)
            if c == 0:
                acc[...] = part
            else:
                acc[...] += part
            r1 = tm if c == nc - 1 else (c + 1) * per_chunk
            for r in range(c * per_chunk, r1):
                gather(i + 1, r, other)
                scatter(prev_tile_dst(r), r, other)

    @pl.when(i == nu)
    def _():
        def one(r, c):
            scatter(dst_ref[(i - 1) * tm + r], r, other)
            return c
        lax.fori_loop(0, tm, one, 0)

    @pl.when((i >= 1) & (i <= nu + 1))
    def _():
        pltpu.make_async_copy(ybuf.at[slot], y_hbm.at[pl.ds(0, tm), :], ssem.at[slot]).wait()

    @pl.when(i < nu)
    def _():
        ybuf[slot] = _pack_bf16_pairs(acc[...])


def _moe_experts(x1, w_gu, w_down, tile_expert, n_used, src, dst, *, tm, fw, out_rows):
    t, d = x1.shape
    ffe = w_down.shape[1]
    nt = src.shape[0] // tm
    return pl.pallas_call(
        functools.partial(_moe_body, tm=tm, fw=fw, spare_row=out_rows - tm),
        grid_spec=pltpu.PrefetchScalarGridSpec(
            num_scalar_prefetch=4,
            grid=(nt,),
            in_specs=[pl.BlockSpec(memory_space=pl.ANY),
                      _resident((None, d, 2 * ffe), lambda i, te, nu, s, ds: (te[i], 0, 0)),
                      _resident((None, ffe, d), lambda i, te, nu, s, ds: (te[i], 0, 0))],
            out_specs=pl.BlockSpec(memory_space=pl.ANY),
            scratch_shapes=[pltpu.VMEM((2, tm, d), F32), pltpu.VMEM((tm, d), BF16), pltpu.VMEM((tm, d), F32),
                            pltpu.VMEM((2, tm, d // 2), jnp.uint32),
                            pltpu.SemaphoreType.DMA((2,)), pltpu.SemaphoreType.DMA((2,))]),
        out_shape=jax.ShapeDtypeStruct((out_rows, d // 2), jnp.uint32),
        compiler_params=_cparams(("arbitrary",)),
        name="moe_experts",
    )(tile_expert, n_used, src, dst, x1, w_gu, w_down)


def _moe_out_body(x_ref, route_ref, y0_ref, y1_ref, p_ref, wpg_ref, wpp_ref, g_ref, b_ref, *rest):
    o_ref = rest[0] if len(rest) == 1 else rest[1]
    x1 = x_ref[...]
    route = route_ref[...]
    ffn = route[:, 2:3] * _unpack_bf16_pairs(y0_ref[...]) + route[:, 3:4] * _unpack_bf16_pairs(y1_ref[...])
    h = DEEPNORM_ALPHA * x1 + ffn + _ple(x1.astype(BF16), p_ref, wpg_ref, wpp_ref)
    x2 = _layer_norm(h, g_ref[...], b_ref[...])
    o_ref[...] = x2
    if len(rest) > 1:
        _project(x2, rest[0], *rest[2:])


def _moe_out(x1, route, p, y, wpg, wpp, g, b, w_next=None, *, tm):
    t, d = x1.shape
    nt = t // tm
    row = lambda i: (i, 0)
    params = [wpg, wpp, g, b] + ([w_next] if w_next is not None else [])
    out_specs, out_shape = [pl.BlockSpec((tm, d), row)], [jax.ShapeDtypeStruct((t, d), F32)]
    if w_next is not None:
        specs, shapes = _proj_outputs(t, tm, w_next.arr.shape[2])
        out_specs, out_shape = out_specs + specs, out_shape + shapes
    out = pl.pallas_call(
        _moe_out_body,
        grid=(nt,),
        in_specs=[pl.BlockSpec((tm, d), row), pl.BlockSpec((tm, LANES), row),
                  pl.BlockSpec((tm, d // 2), row), pl.BlockSpec((tm, d // 2), lambda i: (nt + i, 0)),
                  _ple_rows_spec(p, tm)] + [_layer_spec(q) for q in params],
        out_specs=out_specs, out_shape=out_shape,
        compiler_params=_cparams(("parallel",)),
        name="moe_combine_ln2",
    )(x1, route, y, y, p.arr, *[q.arr for q in params])
    return out[0] if w_next is None else out


def _slot_table_body(pos_ref, init_hbm, dst_ref, sem, *, t, groups):
    @pl.when(pl.program_id(0) == 0)
    def _():
        copy = pltpu.make_async_copy(init_hbm, dst_ref, sem)
        copy.start()
        copy.wait()

    batch = SLOT_TABLE_BATCH
    base = pl.program_id(0) * (t // groups)

    def put(kb, c):
        tok0 = base + kb * batch
        ps = [pos_ref[2 * tok0 + j] for j in range(2 * batch)]
        for j, p in enumerate(ps):
            dst_ref[p] = (j % 2) * t + tok0 + j // 2
        return c
    lax.fori_loop(0, t // groups // batch, put, 0)


def _slot_table(pos, *, t, tm, r):
    groups = SLOT_TABLE_GROUPS
    assert t % (groups * SLOT_TABLE_BATCH) == 0
    init = 2 * t + jnp.arange(r, dtype=jnp.int32) % tm
    smem = pl.BlockSpec(memory_space=pltpu.SMEM)
    return pl.pallas_call(
        functools.partial(_slot_table_body, t=t, groups=groups),
        grid=(groups,),
        in_specs=[smem, pl.BlockSpec(memory_space=pl.ANY)], out_specs=smem,
        out_shape=jax.ShapeDtypeStruct((r,), jnp.int32),
        scratch_shapes=[pltpu.SemaphoreType.DMA],
        compiler_params=_cparams(("arbitrary",)),
        name="moe_slot_table",
    )(pos, init)


def _routing_tables(route, *, tm, n_experts):
    t = route.shape[0]
    eids = route[:, :2].astype(jnp.int32).reshape(-1)
    onehot = (eids[:, None] == jnp.arange(n_experts, dtype=jnp.int32)[None, :]).astype(jnp.int32)
    csum = jnp.cumsum(onehot, axis=0)
    rank = jnp.sum((csum - onehot) * onehot, axis=1)
    cnt = csum[-1]
    ntile = (cnt + tm - 1) // tm
    tile_end = jnp.cumsum(ntile)
    row_off = (tile_end - ntile) * tm
    pos = jnp.sum(onehot * row_off[None, :], axis=1) + rank
    r = 2 * t + (n_experts + 2) * tm
    dst = _slot_table(pos.astype(jnp.int32), t=t, tm=tm, r=r)
    src = jnp.where(dst < 2 * t, dst % t, 0)
    n_used = tile_end[-1:]
    tiles = jnp.arange(r // tm, dtype=jnp.int32)
    te = jnp.sum((tiles[:, None] >= tile_end[None, :]).astype(jnp.int32), axis=1)
    last = jnp.sum((n_used - 1 >= tile_end).astype(jnp.int32))
    te = jnp.where(tiles < n_used, te, last).astype(jnp.int32)
    out_rows = 2 * t + tm
    return te, n_used.astype(jnp.int32), src.astype(jnp.int32), dst.astype(jnp.int32), out_rows


def kernel(x, p, rel_bias, w_in, attn_sinks, ssm_lambda_re, ssm_lambda_im, ssm_log_dt, ssm_b_re, ssm_b_im,
           ssm_c_re, ssm_c_im, ssm_d, w_glu, b_glu, w_branch_attn, w_branch_ssm, w_out, ln1_g, ln1_b,
           ffn_w_gate_up, ffn_w_down, moe_router, moe_w_gate_up, moe_w_down, ple_w_proj, ple_w_gate,
           ln2_g, ln2_b):
    bsz, seq, d = x.shape
    t = bsz * seq
    depth = w_in.shape[0]

    xt = x.reshape(t, d)
    pt = p.reshape(depth, t, p.shape[-1])

    bias = _attn_bias_table(rel_bias)
    sinks = attn_sinks * LOG2_E
    ffn_gu = ffn_w_gate_up.astype(BF16)
    ffn_dn = ffn_w_down.astype(BF16)

    aw, kw = ATTN_WIDTH, KV_WIDTH
    u0 = aw + 2 * kw
    w_proj = jnp.concatenate([_q_head_tiles(w_in[:, :, :aw]), w_in[:, :, aw:u0 + SSM_WIDTH]], axis=2).astype(BF16)
    w_gates = w_in[:, :, u0 + SSM_WIDTH:].astype(BF16)
    wba = w_branch_attn.reshape(depth, N_Q_HEADS, HEAD_DIM, d)[:, np.array(_attn_out_head_order())]
    wba = wba.reshape(depth, aw, d).astype(BF16)
    wbs, wo = w_branch_ssm.astype(BF16), w_out.astype(BF16)
    wpg, wpp = ple_w_gate.astype(BF16), ple_w_proj.astype(BF16)
    g1, b1 = ln1_g.reshape(depth, 1, d), ln1_b.reshape(depth, 1, d)
    g2, b2 = ln2_g.reshape(depth, 1, d), ln2_b.reshape(depth, 1, d)

    lbr, lbi, bbr, bbi = _ssm_params(ssm_lambda_re, ssm_lambda_im, ssm_log_dt, ssm_b_re, ssm_b_im)
    bre = _block_diag_halves(bbr).astype(BF16)
    bim = _block_diag_halves(bbi).astype(BF16)
    cre = _block_diag_halves(jnp.transpose(ssm_c_re, (0, 1, 3, 2))).astype(BF16)
    cim = _block_diag_halves(jnp.transpose(ssm_c_im, (0, 1, 3, 2))).astype(BF16)
    ssm_skip = ssm_d.reshape(depth, 1, -1)
    wglu, bglu = w_glu.astype(BF16), b_glu.reshape(depth, 1, -1)

    rt = jnp.pad(moe_router, ((0, 0), (0, 0), (0, LANES - N_EXPERTS)))
    rhi = rt.astype(BF16)
    rcat = jnp.concatenate([rhi, (rt - rhi.astype(F32)).astype(BF16)], axis=2)

    q, kv, u = _inproj(xt, _Layer(w_proj, 0), tm=TM_PROJ)
    for i in range(depth):
        at = lambda arr, idx=i: _Layer(arr, idx)
        w_next = _Layer(w_proj, i + 1) if i + 1 < depth else None
        a_out = _attention(q, kv, at(sinks), bias, seq=seq, batch=bsz)
        moe_cast = _Layer(moe_w_gate_up if i % 2 == 0 else moe_w_down, i // 2)
        s_out, moe_w = _ssm(u, at(bre), at(bim), at(cre), at(cim), at(lbr), at(lbi), at(ssm_skip), at(wglu),
                            at(bglu), moe_cast, batch=bsz, steps=SSM_STEPS, lane_chunk=SSM_LANE_CHUNK)
        if i % 2 == 0:
            moe_gu = moe_w
        merge_w = (at(w_gates), at(wba), at(wbs), at(wo), at(g1), at(b1))
        if i % 2 == 0:
            xt, q, kv, u = _dense_layer(xt, a_out, s_out, at(pt), *merge_w, ffn_gu, ffn_dn, i // 2,
                                        at(wpg), at(wpp), at(g2), at(b2), w_next, tm=TM_FFN)
        else:
            router = (_Layer(rcat, i // 2),)
            x1, route = _merge(xt, a_out, s_out, *merge_w, router, tm=TM_MERGE)
            te, n_used, src, dst, out_rows = _routing_tables(route, tm=TM_MOE, n_experts=N_EXPERTS)
            y = _moe_experts(x1, moe_gu, moe_w, te, n_used, src, dst, tm=TM_MOE, fw=FW_MOE, out_rows=out_rows)
            out = _moe_out(x1, route, at(pt), y, at(wpg), at(wpp), at(g2), at(b2), w_next, tm=TM_FFN)
            xt, q, kv, u = out if w_next is not None else (out, None, None, None)

    return xt.reshape(bsz, seq, d)
```
